```python
import jax
import jax.numpy as jnp
from jax import lax
import numpy as np

D_MODEL = 2048
BATCH = 8
SEQ = 4096
DEPTH = 4

D_MIX = D_MODEL
RWKV_HEAD_DIM = 64
RWKV_WIDTH = D_MIX // 2
RWKV_HEADS = RWKV_WIDTH // RWKV_HEAD_DIM
DECAY_RANK = 64
AAA_RANK = 64
GATE_RANK = 128
RWKV_SHIFT_WIDTH = 3 * RWKV_WIDTH + DECAY_RANK + AAA_RANK + GATE_RANK
RWKV_GN_EPS = 64e-5
CONV_WIDTH = D_MIX // 4
CONV_K = 3
HGRN_HEAD_DIM = 128
HGRN_WIDTH = D_MIX - RWKV_WIDTH - CONV_WIDTH
HGRN_HEADS = HGRN_WIDTH // HGRN_HEAD_DIM
HGRN_CHUNK = 64
HGRN_NORM_EPS = 1e-5
HGRN_MIN_FORGET = 1e-30
PROJ_WIDTH = RWKV_SHIFT_WIDTH + 3 * CONV_WIDTH + 4 * HGRN_WIDTH
D_FF = ((8 * D_MODEL // 3 + 127) // 128) * 128
FFN_RESIDUAL_SCALE = 0.5
RMS_EPS = 1e-6

kernel_name = 'hybrid_rwkv7_shortconv_hgrn2_macaron'


def _split(t, sizes):
    offs, acc = [], 0
    for s in sizes[:-1]:
        acc += s
        offs.append(acc)
    return jnp.split(t, offs, axis=-1)


def rms_norm(x, w):
    xf = x.astype(jnp.float32)
    y = xf * lax.rsqrt(jnp.mean(xf * xf, axis=-1, keepdims=True) + RMS_EPS)
    return (y * w.astype(jnp.float32)).astype(x.dtype)


def swiglu(x, w_gu, w_down):
    gate, up = jnp.split(x @ w_gu, 2, axis=-1)
    return (jax.nn.silu(gate) * up) @ w_down


def _shift_prev(t):
    return jnp.pad(t, ((0, 0), (1, 0), (0, 0)))[:, :-1]


def rwkv7_time_mix(feat, mu, w_up, w0, a_up, a0, g_up, k_k, k_a, r_k, ln_w, ln_b):
    B_, T, _ = feat.shape
    H, N = RWKV_HEADS, RWKV_HEAD_DIM
    f32 = jnp.float32
    feat = feat.astype(f32)
    feat = feat + (_shift_prev(feat) - feat) * mu.astype(f32)
    r, k, v, wd, ad, gd = _split(feat, (RWKV_WIDTH, RWKV_WIDTH, RWKV_WIDTH, DECAY_RANK, AAA_RANK, GATE_RANK))
    w = -jax.nn.softplus(-(w0.astype(f32) + jnp.tanh(wd) @ w_up.astype(f32))) - 0.5
    decay = jnp.exp(-jnp.exp(w))
    a = jax.nn.sigmoid(a0.astype(f32) + ad @ a_up.astype(f32))
    g = jax.nn.sigmoid(gd) @ g_up.astype(f32)
    heads = lambda t: t.reshape(B_, T, H, N)
    kk = heads(k * k_k.astype(f32))
    kk = kk / jnp.maximum(jnp.sqrt(jnp.sum(kk * kk, axis=-1, keepdims=True)), 1e-12)
    k = k * (1.0 + (a - 1.0) * k_a.astype(f32))
    r, k, v, decay, a = heads(r), heads(k), heads(v), heads(decay), heads(a)

    def step(S, inp):
        r_t, w_t, k_t, v_t, av_t, bv_t = inp
        sa = jnp.einsum('bhvk,bhk->bhv', S, av_t)
        S = S * w_t[:, :, None, :] + sa[..., None] * bv_t[:, :, None, :] + v_t[..., None] * k_t[:, :, None, :]
        return S, jnp.einsum('bhvk,bhk->bhv', S, r_t)

    S0 = jnp.zeros((B_, H, N, N), f32)
    xs = tuple(jnp.swapaxes(t, 0, 1) for t in (r, decay, k, v, -kk, kk * a))
    _, o = lax.scan(step, S0, xs)
    o = jnp.swapaxes(o, 0, 1)
    mean = jnp.mean(o, axis=-1, keepdims=True)
    var = jnp.mean(jnp.square(o - mean), axis=-1, keepdims=True)
    o = ((o - mean) * lax.rsqrt(var + RWKV_GN_EPS)).reshape(B_, T, RWKV_WIDTH) * ln_w.astype(f32) + ln_b.astype(f32)
    bonus = jnp.sum(r * k * r_k.astype(f32), axis=-1, keepdims=True) * v
    return (o + bonus.reshape(B_, T, RWKV_WIDTH)) * g


def short_conv_mix(c_gate, xin, b_gate, conv_w):
    T = xin.shape[1]
    z = c_gate * xin
    zp = jnp.pad(z, ((0, 0), (CONV_K - 1, 0), (0, 0)))
    y = zp[:, :T] * conv_w[0]
    for j in range(1, CONV_K):
        y = y + zp[:, j:j + T] * conv_w[j]
    return (b_gate * y).astype(jnp.float32)


def hgrn2_mix(q, f, i, og, lower_bound, norm_w):
    B_, T, _ = q.shape
    H, K, L = HGRN_HEADS, HGRN_HEAD_DIM, HGRN_CHUNK
    f32 = jnp.float32
    q = jax.nn.silu(q.astype(f32))
    f = f.astype(f32)
    lb = lower_bound.astype(f32)
    forget = lb + (1.0 - lb) * jax.nn.sigmoid(f)
    log_f = jnp.log(jnp.maximum(forget, HGRN_MIN_FORGET))
    k = (1.0 - lb) * jax.nn.sigmoid(-f)

    def chunks(t):
        return t.reshape(B_, T // L, L, H, K).transpose(1, 0, 3, 2, 4)

    causal = jnp.tril(jnp.ones((L, L), dtype=bool))[:, :, None]

    def chunk_step(S, inp):
        q_c, k_c, i_c, g_c = inp
        G = jnp.cumsum(g_c, axis=2)
        o_inter = jnp.einsum('bhtk,bhkv->bhtv', q_c * jnp.exp(G), S)
        diff = G[:, :, :, None, :] - G[:, :, None, :, :]
        rel = jnp.where(causal, jnp.exp(jnp.where(causal, diff, 0.0)), 0.0)
        scores = jnp.einsum('bhtk,bhsk,bhtsk->bhts', q_c, k_c, rel)
        o_intra = jnp.einsum('bhts,bhsv->bhtv', scores, i_c)
        G_end = G[:, :, -1:, :]
        S = jnp.exp(G_end[:, :, 0, :, None]) * S + jnp.einsum('bhsk,bhsv->bhkv', k_c * jnp.exp(G_end - G), i_c)
        return S, o_inter + o_intra

    S0 = jnp.zeros((B_, H, K, K), f32)
    _, o = lax.scan(chunk_step, S0, (chunks(q), chunks(k), chunks(i.astype(f32)), chunks(log_f)))
    o = o.transpose(1, 0, 3, 2, 4).reshape(B_, T, H, K)
    o = o * lax.rsqrt(jnp.mean(o * o, axis=-1, keepdims=True) + HGRN_NORM_EPS) * norm_w.astype(f32).reshape(H, K)
    return o.reshape(B_, T, HGRN_WIDTH) * jax.nn.silu(og.astype(f32))


def setup_inputs(seed: int = 0) -> dict:
    key = jax.random.key(seed)
    ks = jax.random.split(key, 32)
    f32 = jnp.float32
    Ld, D = DEPTH, D_MODEL

    def nrm(k, shape, scale):
        return jax.random.normal(k, shape, f32) * scale

    def gain(k, shape):
        return 1.0 + 0.02 * jax.random.normal(k, shape, f32)

    return {
        'x': nrm(ks[0], (BATCH, SEQ, D), 1.0),
        'ffn1_norm': gain(ks[1], (Ld, D)),
        'ffn1_w_gu': nrm(ks[2], (Ld, D, 2 * D_FF), D ** -0.5),
        'ffn1_w_down': nrm(ks[3], (Ld, D_FF, D), D_FF ** -0.5),
        'mix_norm': gain(ks[4], (Ld, D)),
        'w_in': nrm(ks[5], (Ld, D, PROJ_WIDTH), D ** -0.5),
        'rwkv_mu': jax.random.uniform(ks[6], (Ld, RWKV_SHIFT_WIDTH), f32, 0.0, 1.0),
        'rwkv_w_up': nrm(ks[7], (Ld, DECAY_RANK, RWKV_WIDTH), 0.5 * DECAY_RANK ** -0.5),
        'rwkv_w0': jax.random.uniform(ks[8], (Ld, RWKV_WIDTH), f32, -6.0, 0.0),
        'rwkv_a_up': nrm(ks[9], (Ld, AAA_RANK, RWKV_WIDTH), 0.5 * AAA_RANK ** -0.5),
        'rwkv_a0': nrm(ks[10], (Ld, RWKV_WIDTH), 0.1),
        'rwkv_g_up': nrm(ks[11], (Ld, GATE_RANK, RWKV_WIDTH), GATE_RANK ** -0.5),
        'rwkv_k_k': 0.85 + 0.05 * jax.random.normal(ks[12], (Ld, RWKV_WIDTH), f32),
        'rwkv_k_a': gain(ks[13], (Ld, RWKV_WIDTH)),
        'rwkv_r_k': nrm(ks[14], (Ld, RWKV_HEADS, RWKV_HEAD_DIM), 0.1),
        'rwkv_ln_w': gain(ks[15], (Ld, RWKV_WIDTH)),
        'rwkv_ln_b': nrm(ks[16], (Ld, RWKV_WIDTH), 0.02),
        'conv_w': nrm(ks[17], (Ld, CONV_K, CONV_WIDTH), CONV_K ** -0.5),
        'hgrn_lb': nrm(ks[18], (Ld, HGRN_WIDTH), 0.5),
        'hgrn_norm': gain(ks[19], (Ld, HGRN_WIDTH)),
        'w_out': nrm(ks[20], (Ld, D_MIX, D), D_MIX ** -0.5),
        'ffn2_norm': gain(ks[21], (Ld, D)),
        'ffn2_w_gu': nrm(ks[22], (Ld, D, 2 * D_FF), D ** -0.5),
        'ffn2_w_down': nrm(ks[23], (Ld, D_FF, D), D_FF ** -0.5),
        'final_norm': gain(ks[24], (D,)),
    }


def reference(x, ffn1_norm, ffn1_w_gu, ffn1_w_down, mix_norm, w_in, rwkv_mu, rwkv_w_up, rwkv_w0,
              rwkv_a_up, rwkv_a0, rwkv_g_up, rwkv_k_k, rwkv_k_a, rwkv_r_k, rwkv_ln_w, rwkv_ln_b,
              conv_w, hgrn_lb, hgrn_norm, w_out, ffn2_norm, ffn2_w_gu, ffn2_w_down, final_norm):
    p = jax.nn.softmax(hgrn_lb.astype(jnp.float32), axis=0)
    lower_bounds = jnp.cumsum(p, axis=0) - p[0]
    h = x
    for l in range(DEPTH):
        h = h + FFN_RESIDUAL_SCALE * swiglu(rms_norm(h, ffn1_norm[l]), ffn1_w_gu[l], ffn1_w_down[l])
        proj = rms_norm(h, mix_norm[l]) @ w_in[l]
        p_rwkv, p_conv, p_hgrn = _split(proj, (RWKV_SHIFT_WIDTH, 3 * CONV_WIDTH, 4 * HGRN_WIDTH))
        y_rwkv = rwkv7_time_mix(p_rwkv, rwkv_mu[l], rwkv_w_up[l], rwkv_w0[l], rwkv_a_up[l], rwkv_a0[l],
                                rwkv_g_up[l], rwkv_k_k[l], rwkv_k_a[l], rwkv_r_k[l], rwkv_ln_w[l], rwkv_ln_b[l])
        c_gate, xin, b_gate = jnp.split(p_conv, 3, axis=-1)
        y_conv = short_conv_mix(c_gate, xin, b_gate, conv_w[l])
        q, f, i, og = jnp.split(p_hgrn, 4, axis=-1)
        y_hgrn = hgrn2_mix(q, f, i, og, lower_bounds[l], hgrn_norm[l])
        y = jnp.concatenate([y_rwkv, y_conv, y_hgrn], axis=-1).astype(h.dtype)
        h = h + y @ w_out[l]
        h = h + FFN_RESIDUAL_SCALE * swiglu(rms_norm(h, ffn2_norm[l]), ffn2_w_gu[l], ffn2_w_down[l])
    return rms_norm(h, final_norm)
```

```python
import functools

import jax
import jax.numpy as jnp
from jax import lax
from jax.experimental import pallas as pl
from jax.experimental.pallas import tpu as pltpu

F32 = jnp.float32
BF16 = jnp.bfloat16

RWKV_HEAD_DIM = 64
RWKV_WIDTH = 1024
DECAY_RANK = 64
AAA_RANK = 64
GATE_RANK = 128
RWKV_LOW = DECAY_RANK + AAA_RANK + GATE_RANK
RWKV_SHIFT_WIDTH = 3 * RWKV_WIDTH + RWKV_LOW
RWKV_GN_EPS = 64e-5
CONV_WIDTH = 512
CONV_K = 3
HGRN_HEAD_DIM = 128
HGRN_WIDTH = 512
HGRN_NORM_EPS = 1e-5
HGRN_MIN_FORGET = 1e-30
FFN_RESIDUAL_SCALE = 0.5
RMS_EPS = 1e-6

LANES = 128
RWKV_CHUNK = 64
HGRN_BLOCK = 16
VMEM_LIMIT = 56 * 1024 * 1024


def _dot(a, b):
    return jnp.dot(a, b, preferred_element_type=F32)


def _dot_nt(a, b):
    return lax.dot_general(a, b, (((1,), (1,)), ((), ())), preferred_element_type=F32)


def _rms(x, w):
    return x * lax.rsqrt(jnp.mean(x * x, axis=-1, keepdims=True) + RMS_EPS) * w


def _split_bf16(x):
    hi = x.astype(BF16)
    lo = (x - hi.astype(F32)).astype(BF16)
    return hi, lo


def _ffn_kernel(x_ref, nw_ref, wg_ref, wu_ref, wd_ref, fw_ref, o_ref, xn_ref, acc_ref, *, final):
    j = pl.program_id(1)

    @pl.when(j == 0)
    def _():
        xn_ref[...] = _rms(x_ref[...], nw_ref[...]).astype(BF16)
        acc_ref[...] = jnp.zeros_like(acc_ref)

    xn = xn_ref[...]
    g = _dot(xn, wg_ref[...])
    u = _dot(xn, wu_ref[...])
    a = (g * jax.nn.sigmoid(g) * u).astype(BF16)
    acc_ref[...] += _dot(a, wd_ref[...])

    @pl.when(j == pl.num_programs(1) - 1)
    def _():
        h = x_ref[...] + FFN_RESIDUAL_SCALE * acc_ref[...]
        if final:
            h = _rms(h, fw_ref[...])
        o_ref[...] = h


def _ffn(h, norm_w, wg, wu, wd, final_w, *, final, tm=512, tf=512):
    m, d = h.shape
    ffp = wg.shape[1]
    grid = (m // tm, ffp // tf)
    return pl.pallas_call(
        functools.partial(_ffn_kernel, final=final),
        grid=grid,
        in_specs=[
            pl.BlockSpec((tm, d), lambda i, j: (i, 0)),
            pl.BlockSpec((1, d), lambda i, j: (0, 0)),
            pl.BlockSpec((d, tf), lambda i, j: (0, j)),
            pl.BlockSpec((d, tf), lambda i, j: (0, j)),
            pl.BlockSpec((tf, d), lambda i, j: (j, 0)),
            pl.BlockSpec((1, d), lambda i, j: (0, 0)),
        ],
        out_specs=pl.BlockSpec((tm, d), lambda i, j: (i, 0)),
        out_shape=jax.ShapeDtypeStruct((m, d), F32),
        scratch_shapes=[pltpu.VMEM((tm, d), BF16), pltpu.VMEM((tm, d), F32)],
        compiler_params=pltpu.CompilerParams(
            dimension_semantics=("parallel", "arbitrary"), vmem_limit_bytes=VMEM_LIMIT),
        name="ffn",
    )(h, norm_w, wg, wu, wd, final_w)


def _proj_kernel(x_ref, nw_ref, w_ref, o_ref, xn_ref):
    @pl.when(pl.program_id(1) == 0)
    def _():
        xn_ref[...] = _rms(x_ref[...], nw_ref[...]).astype(BF16)

    o_ref[...] = _dot(xn_ref[...], w_ref[...])


def _proj(h, norm_w, w, *, tm=512, tn=768):
    m, d = h.shape
    n = w.shape[1]
    return pl.pallas_call(
        _proj_kernel,
        grid=(m // tm, n // tn),
        in_specs=[
            pl.BlockSpec((tm, d), lambda i, j: (i, 0)),
            pl.BlockSpec((1, d), lambda i, j: (0, 0)),
            pl.BlockSpec((d, tn), lambda i, j: (0, j)),
        ],
        out_specs=pl.BlockSpec((tm, tn), lambda i, j: (i, j)),
        out_shape=jax.ShapeDtypeStruct((m, n), F32),
        scratch_shapes=[pltpu.VMEM((tm, d), BF16)],
        compiler_params=pltpu.CompilerParams(
            dimension_semantics=("parallel", "arbitrary"), vmem_limit_bytes=VMEM_LIMIT),
        name="proj",
    )(h, norm_w, w)


def _outproj_kernel(h_ref, yr_ref, yc_ref, yh_ref, wr_ref, wc_ref, wh_ref, o_ref):
    acc = _dot(yr_ref[...], wr_ref[...])
    acc += _dot(yc_ref[...], wc_ref[...])
    acc += _dot(yh_ref[...], wh_ref[...])
    o_ref[...] = h_ref[...] + acc


def _outproj(h, yr, yc, yh, wr, wc, wh, *, tm=512):
    m, d = h.shape
    row = lambda i: (i, 0)
    fixed = lambda i: (0, 0)
    return pl.pallas_call(
        _outproj_kernel,
        grid=(m // tm,),
        in_specs=[
            pl.BlockSpec((tm, d), row),
            pl.BlockSpec((tm, yr.shape[1]), row),
            pl.BlockSpec((tm, yc.shape[1]), row),
            pl.BlockSpec((tm, yh.shape[1]), row),
            pl.BlockSpec(wr.shape, fixed),
            pl.BlockSpec(wc.shape, fixed),
            pl.BlockSpec(wh.shape, fixed),
        ],
        out_specs=pl.BlockSpec((tm, d), row),
        out_shape=jax.ShapeDtypeStruct((m, d), F32),
        compiler_params=pltpu.CompilerParams(
            dimension_semantics=("parallel",), vmem_limit_bytes=VMEM_LIMIT),
        name="outproj",
    )(h, yr, yc, yh, wr, wc, wh)


def _prev_rows(x, carry, k=1):
    rows = lax.broadcasted_iota(jnp.int32, (8, x.shape[1]), 0)
    prev = pltpu.roll(x, k, 0)
    head = jnp.where(rows < k, pltpu.roll(carry, k, 0), prev[:8])
    return jnp.concatenate([head, prev[8:]], axis=0)


def _seg_sum(x, ones_bd):
    hi, lo = _split_bf16(x)
    return _dot(hi, ones_bd) + _dot(lo, ones_bd)


def _rwkv_chunk(lw, r, k2, v, av, bv, st, c):
    hi, lo = _split_bf16(lw)
    gcat = _dot(c["tri"], jnp.concatenate([hi, lo], axis=1))
    g = gcat[:, :LANES] + gcat[:, LANES:]
    gl = g[RWKV_CHUNK - 1:RWKV_CHUNK, :]
    e_g = jnp.exp(g)
    e_ng = jnp.exp(-g)
    e_d = jnp.exp(gl - g)
    at = av * jnp.exp(g - lw)
    rt = r * e_g
    bt = bv * e_ng
    kt = k2 * e_ng
    b_end = bv * e_d
    k_end = k2 * e_d

    m0, m1 = c["m0"], c["m1"]
    stack = lambda x: jnp.concatenate([x * m0, x * m1], axis=0)
    ats, rts, bts, kts, bes, kes, vs = (stack(x) for x in (at, rt, bt, kt, b_end, k_end, v))

    p = _dot_nt(jnp.concatenate([ats, rts], axis=0).astype(BF16),
                jnp.concatenate([bts, kts], axis=0).astype(BF16))
    n = 2 * RWKV_CHUNK
    a_ab = jnp.where(c["strict"], p[:n, :n], 0.0)
    a_ak = jnp.where(c["strict"], p[:n, n:], 0.0)
    a_rb = jnp.where(c["incl"], p[n:, :n], 0.0)
    a_rk = jnp.where(c["incl"], p[n:, n:], 0.0)

    inv = c["eye"] + a_ab
    q = a_ab
    for _ in range(5):
        qb = q.astype(BF16)
        q = _dot(qb, qb)
        inv = inv + _dot(inv.astype(BF16), q.astype(BF16))

    vsb = vs.astype(BF16)
    akv_rkv = _dot(jnp.concatenate([a_ak, a_rk], axis=0).astype(BF16), vsb)
    tt = _dot(inv.astype(BF16), jnp.concatenate([ats, akv_rkv[:n]], axis=1).astype(BF16))
    ttb = tt.astype(BF16)
    rr = _dot(a_rb.astype(BF16), ttb)
    rq = rts + rr[:, :LANES]
    oc = rr[:, LANES:] + akv_rkv[n:]
    bb = _dot(tt.T.astype(BF16), bes.astype(BF16))
    nct = bb[LANES:] + _dot(vs.T.astype(BF16), kes.astype(BF16))

    stb = st.astype(BF16)
    o_stack = _dot_nt(rq.astype(BF16), stb) + oc
    st_new = st * jnp.exp(gl) + _dot(stb, bb[:LANES].astype(BF16)) + nct
    o = o_stack[:RWKV_CHUNK] + o_stack[RWKV_CHUNK:]
    return o, st_new


def _rwkv_kernel(r_ref, k_ref, v_ref, low_ref, mur_ref, muk_ref, muv_ref, mul_ref,
                 ww_ref, wa_ref, wg_ref, w0_ref, a0_ref, kk_ref, ka_ref, rk_ref, lnw_ref, lnb_ref,
                 y_ref, cr_ref, ck_ref, cv_ref, cl_ref, st_ref, *, tb):
    @pl.when(pl.program_id(2) == 0)
    def _():
        cr_ref[...] = jnp.zeros_like(cr_ref)
        ck_ref[...] = jnp.zeros_like(ck_ref)
        cv_ref[...] = jnp.zeros_like(cv_ref)
        cl_ref[...] = jnp.zeros_like(cl_ref)
        st_ref[...] = jnp.zeros_like(st_ref)

    def lerp(x_ref, c_ref, mu_ref):
        x = x_ref[0]
        prev = _prev_rows(x, c_ref[...])
        c_ref[...] = x[tb - 8:tb, :]
        return x + (prev - x) * mu_ref[...]

    r = lerp(r_ref, cr_ref, mur_ref)
    k = lerp(k_ref, ck_ref, muk_ref)
    v = lerp(v_ref, cv_ref, muv_ref)
    low = lerp(low_ref, cl_ref, mul_ref)
    wa_in = low[:, :LANES]
    xw = w0_ref[...] + _dot(jnp.tanh(wa_in).astype(BF16), ww_ref[...])
    lw = -jnp.exp(jnp.float32(-0.5)) * jax.nn.sigmoid(xw)
    a = jax.nn.sigmoid(a0_ref[...] + _dot(wa_in.astype(BF16), wa_ref[...]))
    gate = _dot(jax.nn.sigmoid(low[:, LANES:]).astype(BF16), wg_ref[...])

    lane = lax.broadcasted_iota(jnp.int32, (1, LANES), 1)
    m0 = (lane < RWKV_HEAD_DIM).astype(F32)
    m1 = 1.0 - m0
    ri = lax.broadcasted_iota(jnp.int32, (LANES, LANES), 0)
    ci = lax.broadcasted_iota(jnp.int32, (LANES, LANES), 1)
    same = (ri < RWKV_HEAD_DIM) == (ci < RWKV_HEAD_DIM)
    ones_bd = same.astype(F32).astype(BF16)
    ti = lax.broadcasted_iota(jnp.int32, (RWKV_CHUNK, RWKV_CHUNK), 0)
    si = lax.broadcasted_iota(jnp.int32, (RWKV_CHUNK, RWKV_CHUNK), 1)
    consts = dict(
        m0=m0, m1=m1,
        strict=same & (ri > ci), incl=same & (ri >= ci),
        eye=(ri == ci).astype(F32),
        tri=(ti >= si).astype(F32).astype(BF16),
    )

    kk = k * kk_ref[...]
    kk = kk / jnp.maximum(jnp.sqrt(_seg_sum(kk * kk, ones_bd)), 1e-12)
    k2 = k * (1.0 + (a - 1.0) * ka_ref[...])
    av = -kk
    bv = kk * a

    st = st_ref[...]
    outs = []
    for ci_ in range(tb // RWKV_CHUNK):
        sl = slice(ci_ * RWKV_CHUNK, (ci_ + 1) * RWKV_CHUNK)
        o, st = _rwkv_chunk(lw[sl], r[sl], k2[sl], v[sl], av[sl], bv[sl], st, consts)
        outs.append(o)
    st_ref[...] = st
    o = jnp.concatenate(outs, axis=0)

    inv_n = 1.0 / RWKV_HEAD_DIM
    mean = _seg_sum(o, ones_bd) * inv_n
    oc = o - mean
    var = _seg_sum(oc * oc, ones_bd) * inv_n
    o = oc * lax.rsqrt(var + RWKV_GN_EPS) * lnw_ref[...] + lnb_ref[...]
    bonus = _seg_sum(r * k2 * rk_ref[...], ones_bd) * v
    y_ref[0] = ((o + bonus) * gate).astype(y_ref.dtype)


def _rwkv(proj3, mu, ww, wa, wg, w0, a0, k_k, k_a, r_k, ln_w, ln_b, *, tb=256):
    b, t, _ = proj3.shape
    npairs = RWKV_WIDTH // LANES
    off_k = RWKV_WIDTH // LANES
    off_v = 2 * RWKV_WIDTH // LANES
    off_low = 3 * RWKV_WIDTH // RWKV_LOW
    vec = lambda off: pl.BlockSpec((1, LANES), lambda bi, p, ti: (0, off + p))
    pvec = pl.BlockSpec((1, LANES), lambda bi, p, ti: (0, p))
    wspec = pl.BlockSpec((LANES, LANES), lambda bi, p, ti: (0, p))
    return pl.pallas_call(
        functools.partial(_rwkv_kernel, tb=tb),
        grid=(b, npairs, t // tb),
        in_specs=[
            pl.BlockSpec((1, tb, LANES), lambda bi, p, ti: (bi, ti, p)),
            pl.BlockSpec((1, tb, LANES), lambda bi, p, ti: (bi, ti, off_k + p)),
            pl.BlockSpec((1, tb, LANES), lambda bi, p, ti: (bi, ti, off_v + p)),
            pl.BlockSpec((1, tb, RWKV_LOW), lambda bi, p, ti: (bi, ti, off_low)),
            vec(0), vec(off_k), vec(off_v),
            pl.BlockSpec((1, RWKV_LOW), lambda bi, p, ti: (0, off_low)),
            wspec, wspec, wspec,
            pvec, pvec, pvec, pvec, pvec, pvec, pvec,
        ],
        out_specs=pl.BlockSpec((1, tb, LANES), lambda bi, p, ti: (bi, ti, p)),
        out_shape=jax.ShapeDtypeStruct((b, t, RWKV_WIDTH), BF16),
        scratch_shapes=[pltpu.VMEM((8, LANES), F32), pltpu.VMEM((8, LANES), F32), pltpu.VMEM((8, LANES), F32),
                        pltpu.VMEM((8, RWKV_LOW), F32), pltpu.VMEM((LANES, LANES), F32)],
        compiler_params=pltpu.CompilerParams(
            dimension_semantics=("parallel", "parallel", "arbitrary"), vmem_limit_bytes=VMEM_LIMIT),
        name="rwkv7",
    )(proj3, proj3, proj3, proj3, mu, mu, mu, mu, ww, wa, wg, w0, a0, k_k, k_a, r_k, ln_w, ln_b)


def _conv_kernel(c_ref, x_ref, b_ref, w_ref, y_ref, carry_ref):
    @pl.when(pl.program_id(2) == 0)
    def _():
        carry_ref[...] = jnp.zeros_like(carry_ref)

    z = c_ref[0] * x_ref[0]
    tb = z.shape[0]
    w = w_ref[...]
    y = z * w[CONV_K - 1:CONV_K, :]
    for j in range(1, CONV_K):
        y = y + _prev_rows(z, carry_ref[...], j) * w[CONV_K - 1 - j:CONV_K - j, :]
    carry_ref[...] = z[tb - 8:tb, :]
    y_ref[0] = (b_ref[0] * y).astype(y_ref.dtype)


def _conv(proj3, conv_w, *, tb=512):
    b, t, _ = proj3.shape
    nblk = CONV_WIDTH // LANES
    off = RWKV_SHIFT_WIDTH // LANES
    spec = lambda o: pl.BlockSpec((1, tb, LANES), lambda bi, p, ti: (bi, ti, o + p))
    return pl.pallas_call(
        _conv_kernel,
        grid=(b, nblk, t // tb),
        in_specs=[spec(off), spec(off + nblk), spec(off + 2 * nblk),
                  pl.BlockSpec((CONV_K, LANES), lambda bi, p, ti: (0, p))],
        out_specs=pl.BlockSpec((1, tb, LANES), lambda bi, p, ti: (bi, ti, p)),
        out_shape=jax.ShapeDtypeStruct((b, t, CONV_WIDTH), BF16),
        scratch_shapes=[pltpu.VMEM((8, LANES), F32)],
        compiler_params=pltpu.CompilerParams(
            dimension_semantics=("parallel", "parallel", "arbitrary"), vmem_limit_bytes=VMEM_LIMIT),
        name="shortconv",
    )(proj3, proj3, proj3, conv_w)


def _hgrn_kernel(q_ref, f_ref, i_ref, og_ref, lb_ref, nw_ref, y_ref, st_ref, *, layer, tb):
    @pl.when(pl.program_id(2) == 0)
    def _():
        st_ref[...] = jnp.zeros_like(st_ref)

    lbp = lb_ref[...]
    e = jnp.exp(lbp - jnp.max(lbp, axis=0, keepdims=True))
    pr = e / jnp.sum(e, axis=0, keepdims=True)
    lb = jnp.zeros((1, LANES), F32)
    for l in range(1, layer + 1):
        lb = lb + pr[l:l + 1, :]

    qr = q_ref[0]
    q = qr * jax.nn.sigmoid(qr)
    f = f_ref[0]
    forget = lb + (1.0 - lb) * jax.nn.sigmoid(f)
    logf = jnp.log(jnp.maximum(forget, HGRN_MIN_FORGET))
    kx = (1.0 - lb) * jax.nn.sigmoid(-f)
    iv = i_ref[0]

    n = HGRN_BLOCK
    ti = lax.broadcasted_iota(jnp.int32, (n, n), 0)
    si = lax.broadcasted_iota(jnp.int32, (n, n), 1)
    tri = (ti >= si).astype(F32).astype(BF16)
    trow = lax.broadcasted_iota(jnp.int32, (n, 1), 0)

    st = st_ref[...]
    outs = []
    for bi in range(tb // n):
        sl = slice(bi * n, (bi + 1) * n)
        lf, qb, kb, ib = logf[sl], q[sl], kx[sl], iv[sl]
        hi, lo = _split_bf16(lf)
        gcat = _dot(tri, jnp.concatenate([hi, lo], axis=1))
        g = gcat[:, :LANES] + gcat[:, LANES:]
        gl = g[n - 1:n, :]
        o = _dot_nt((qb * jnp.exp(g)).astype(BF16), st.astype(BF16))
        for s in range(n):
            w = jnp.sum(qb * kb[s:s + 1, :] * jnp.exp(g - g[s:s + 1, :]), axis=-1, keepdims=True)
            o = o + jnp.where(trow >= s, w, 0.0) * ib[s:s + 1, :]
        kd = kb * jnp.exp(gl - g)
        st = st * jnp.exp(gl) + _dot(ib.T.astype(BF16), kd.astype(BF16))
        outs.append(o)
    st_ref[...] = st
    o = jnp.concatenate(outs, axis=0)
    o = o * lax.rsqrt(jnp.mean(o * o, axis=-1, keepdims=True) + HGRN_NORM_EPS) * nw_ref[...]
    og = og_ref[0]
    y_ref[0] = (o * (og * jax.nn.sigmoid(og))).astype(y_ref.dtype)


def _hgrn(proj3, hgrn_lb, norm_w, *, layer, tb=256):
    b, t, _ = proj3.shape
    nh = HGRN_WIDTH // HGRN_HEAD_DIM
    off = (RWKV_SHIFT_WIDTH + 3 * CONV_WIDTH) // LANES
    spec = lambda o: pl.BlockSpec((1, tb, LANES), lambda bi, p, ti: (bi, ti, o + p))
    depth = hgrn_lb.shape[0]
    return pl.pallas_call(
        functools.partial(_hgrn_kernel, layer=layer, tb=tb),
        grid=(b, nh, t // tb),
        in_specs=[spec(off), spec(off + nh), spec(off + 2 * nh), spec(off + 3 * nh),
                  pl.BlockSpec((depth, LANES), lambda bi, p, ti: (0, p)),
                  pl.BlockSpec((1, LANES), lambda bi, p, ti: (0, p))],
        out_specs=pl.BlockSpec((1, tb, LANES), lambda bi, p, ti: (bi, ti, p)),
        out_shape=jax.ShapeDtypeStruct((b, t, HGRN_WIDTH), BF16),
        scratch_shapes=[pltpu.VMEM((LANES, LANES), F32)],
        compiler_params=pltpu.CompilerParams(
            dimension_semantics=("parallel", "parallel", "arbitrary"), vmem_limit_bytes=VMEM_LIMIT),
        name="hgrn2",
    )(proj3, proj3, proj3, proj3, hgrn_lb, norm_w)


def _pad_cols(w, n):
    return jnp.pad(w, ((0, 0), (0, n - w.shape[1])))


def kernel(x, ffn1_norm, ffn1_w_gu, ffn1_w_down, mix_norm, w_in, rwkv_mu, rwkv_w_up, rwkv_w0,
           rwkv_a_up, rwkv_a0, rwkv_g_up, rwkv_k_k, rwkv_k_a, rwkv_r_k, rwkv_ln_w, rwkv_ln_b,
           conv_w, hgrn_lb, hgrn_norm, w_out, ffn2_norm, ffn2_w_gu, ffn2_w_down, final_norm):
    b, t, d = x.shape
    depth = ffn1_norm.shape[0]
    d_ff = ffn1_w_down.shape[1]
    ffp = -(-d_ff // 512) * 512
    m = b * t
    h = x.reshape(m, d)
    row = lambda v: v.reshape(1, -1)
    final_w = row(final_norm)

    def ffn_weights(w_gu, w_down):
        wg = _pad_cols(w_gu[:, :d_ff], ffp).astype(BF16)
        wu = _pad_cols(w_gu[:, d_ff:], ffp).astype(BF16)
        wd = jnp.pad(w_down, ((0, ffp - d_ff), (0, 0))).astype(BF16)
        return wg, wu, wd

    zeros_low = jnp.zeros((LANES - DECAY_RANK, RWKV_WIDTH), F32)
    for l in range(depth):
        wg, wu, wd = ffn_weights(ffn1_w_gu[l], ffn1_w_down[l])
        h = _ffn(h, row(ffn1_norm[l]), wg, wu, wd, final_w, final=False)

        proj = _proj(h, row(mix_norm[l]), w_in[l].astype(BF16))
        proj3 = proj.reshape(b, t, -1)
        ww = jnp.concatenate([rwkv_w_up[l], zeros_low], axis=0).astype(BF16)
        wa = jnp.concatenate([zeros_low, rwkv_a_up[l]], axis=0).astype(BF16)
        y_r = _rwkv(proj3, row(rwkv_mu[l]), ww, wa, rwkv_g_up[l].astype(BF16),
                    row(rwkv_w0[l]), row(rwkv_a0[l]), row(rwkv_k_k[l]), row(rwkv_k_a[l]),
                    row(rwkv_r_k[l]), row(rwkv_ln_w[l]), row(rwkv_ln_b[l]))
        y_c = _conv(proj3, conv_w[l])
        y_h = _hgrn(proj3, hgrn_lb, row(hgrn_norm[l]), layer=l)
        wo = w_out[l].astype(BF16)
        h = _outproj(h, y_r.reshape(m, -1), y_c.reshape(m, -1), y_h.reshape(m, -1),
                     wo[:RWKV_WIDTH], wo[RWKV_WIDTH:RWKV_WIDTH + CONV_WIDTH], wo[RWKV_WIDTH + CONV_WIDTH:])

        wg, wu, wd = ffn_weights(ffn2_w_gu[l], ffn2_w_down[l])
        h = _ffn(h, row(ffn2_norm[l]), wg, wu, wd, final_w, final=(l == depth - 1))
    return h.reshape(b, t, d)
```

```python
import functools

import jax
import jax.numpy as jnp
from jax import lax
from jax.experimental import pallas as pl
from jax.experimental.pallas import tpu as pltpu

F32 = jnp.float32
BF16 = jnp.bfloat16

RWKV_HEAD_DIM = 64
RWKV_WIDTH = 1024
DECAY_RANK = 64
AAA_RANK = 64
GATE_RANK = 128
RWKV_LOW = DECAY_RANK + AAA_RANK + GATE_RANK
RWKV_SHIFT_WIDTH = 3 * RWKV_WIDTH + RWKV_LOW
RWKV_GN_EPS = 64e-5
CONV_WIDTH = 512
CONV_K = 3
HGRN_HEAD_DIM = 128
HGRN_WIDTH = 512
HGRN_NORM_EPS = 1e-5
HGRN_MIN_FORGET = 1e-30
FFN_RESIDUAL_SCALE = 0.5
RMS_EPS = 1e-6

LANES = 128
RWKV_CHUNK = 64
HGRN_BLOCK = 16
VMEM_LIMIT = 56 * 1024 * 1024


def _dot(a, b):
    return jnp.dot(a, b, preferred_element_type=F32)


def _dot_nt(a, b):
    return lax.dot_general(a, b, (((1,), (1,)), ((), ())), preferred_element_type=F32)


def _rms(x, w):
    return x * lax.rsqrt(jnp.mean(x * x, axis=-1, keepdims=True) + RMS_EPS) * w


def _split_bf16(x):
    hi = x.astype(BF16)
    lo = (x - hi.astype(F32)).astype(BF16)
    return hi, lo


def _ffn_kernel(x_ref, nw_ref, wg_ref, wu_ref, wd_ref, fw_ref, o_ref, xn_ref, acc_ref, *, final):
    j = pl.program_id(1)

    @pl.when(j == 0)
    def _():
        xn_ref[...] = _rms(x_ref[...], nw_ref[...]).astype(BF16)
        acc_ref[...] = jnp.zeros_like(acc_ref)

    xn = xn_ref[...]
    g = _dot(xn, wg_ref[...])
    u = _dot(xn, wu_ref[...])
    a = (g * jax.nn.sigmoid(g) * u).astype(BF16)
    acc_ref[...] += _dot(a, wd_ref[...])

    @pl.when(j == pl.num_programs(1) - 1)
    def _():
        h = x_ref[...] + FFN_RESIDUAL_SCALE * acc_ref[...]
        if final:
            h = _rms(h, fw_ref[...])
        o_ref[...] = h


def _ffn(h, norm_w, wg, wu, wd, final_w, *, final, tm=512, tf=512):
    m, d = h.shape
    ffp = wg.shape[1]
    grid = (m // tm, ffp // tf)
    return pl.pallas_call(
        functools.partial(_ffn_kernel, final=final),
        grid=grid,
        in_specs=[
            pl.BlockSpec((tm, d), lambda i, j: (i, 0)),
            pl.BlockSpec((1, d), lambda i, j: (0, 0)),
            pl.BlockSpec((d, tf), lambda i, j: (0, j)),
            pl.BlockSpec((d, tf), lambda i, j: (0, j)),
            pl.BlockSpec((tf, d), lambda i, j: (j, 0)),
            pl.BlockSpec((1, d), lambda i, j: (0, 0)),
        ],
        out_specs=pl.BlockSpec((tm, d), lambda i, j: (i, 0)),
        out_shape=jax.ShapeDtypeStruct((m, d), F32),
        scratch_shapes=[pltpu.VMEM((tm, d), BF16), pltpu.VMEM((tm, d), F32)],
        compiler_params=pltpu.CompilerParams(
            dimension_semantics=("parallel", "arbitrary"), vmem_limit_bytes=VMEM_LIMIT),
        name="ffn",
    )(h, norm_w, wg, wu, wd, final_w)


def _proj_kernel(x_ref, nw_ref, w_ref, o_ref, xn_ref):
    @pl.when(pl.program_id(1) == 0)
    def _():
        xn_ref[...] = _rms(x_ref[...], nw_ref[...]).astype(BF16)

    o_ref[...] = _dot(xn_ref[...], w_ref[...])


def _proj(h, norm_w, w, *, tm=512, tn=768):
    m, d = h.shape
    n = w.shape[1]
    return pl.pallas_call(
        _proj_kernel,
        grid=(m // tm, n // tn),
        in_specs=[
            pl.BlockSpec((tm, d), lambda i, j: (i, 0)),
            pl.BlockSpec((1, d), lambda i, j: (0, 0)),
            pl.BlockSpec((d, tn), lambda i, j: (0, j)),
        ],
        out_specs=pl.BlockSpec((tm, tn), lambda i, j: (i, j)),
        out_shape=jax.ShapeDtypeStruct((m, n), F32),
        scratch_shapes=[pltpu.VMEM((tm, d), BF16)],
        compiler_params=pltpu.CompilerParams(
            dimension_semantics=("parallel", "arbitrary"), vmem_limit_bytes=VMEM_LIMIT),
        name="proj",
    )(h, norm_w, w)


def _outproj_kernel(h_ref, yr_ref, yc_ref, yh_ref, wr_ref, wc_ref, wh_ref, o_ref):
    acc = _dot(yr_ref[...], wr_ref[...])
    acc += _dot(yc_ref[...], wc_ref[...])
    acc += _dot(yh_ref[...], wh_ref[...])
    o_ref[...] = h_ref[...] + acc


def _outproj(h, yr, yc, yh, wr, wc, wh, *, tm=512):
    m, d = h.shape
    row = lambda i: (i, 0)
    fixed = lambda i: (0, 0)
    return pl.pallas_call(
        _outproj_kernel,
        grid=(m // tm,),
        in_specs=[
            pl.BlockSpec((tm, d), row),
            pl.BlockSpec((tm, yr.shape[1]), row),
            pl.BlockSpec((tm, yc.shape[1]), row),
            pl.BlockSpec((tm, yh.shape[1]), row),
            pl.BlockSpec(wr.shape, fixed),
            pl.BlockSpec(wc.shape, fixed),
            pl.BlockSpec(wh.shape, fixed),
        ],
        out_specs=pl.BlockSpec((tm, d), row),
        out_shape=jax.ShapeDtypeStruct((m, d), F32),
        compiler_params=pltpu.CompilerParams(
            dimension_semantics=("parallel",), vmem_limit_bytes=VMEM_LIMIT),
        name="outproj",
    )(h, yr, yc, yh, wr, wc, wh)


def _prev_rows(x, carry, k=1):
    rows = lax.broadcasted_iota(jnp.int32, (8, x.shape[1]), 0)
    prev = pltpu.roll(x, k, 0)
    head = jnp.where(rows < k, pltpu.roll(carry, k, 0), prev[:8])
    return jnp.concatenate([head, prev[8:]], axis=0)


def _seg_sum(x, ones_bd):
    hi, lo = _split_bf16(x)
    return _dot(hi, ones_bd) + _dot(lo, ones_bd)


def _rwkv_chunks(lw, r, k2, v, av, bv, st, c):
    nc = len(lw)
    n = 2 * RWKV_CHUNK
    m0, m1 = c["m0"], c["m1"]
    stack = lambda x: jnp.concatenate([x * m0, x * m1], axis=0)
    each = lambda f, *ls: [f(*xs) for xs in zip(*ls)]

    def cumsum(x):
        hi, lo = _split_bf16(x)
        gcat = _dot(c["tri"], jnp.concatenate([hi, lo], axis=1))
        return gcat[:, :LANES] + gcat[:, LANES:]

    g = each(cumsum, lw)
    gl = [x[RWKV_CHUNK - 1:RWKV_CHUNK, :] for x in g]
    e_g = each(jnp.exp, g)
    e_ng = each(lambda x: jnp.exp(-x), g)
    e_d = each(lambda x, y: jnp.exp(x - y), gl, g)
    ats = each(lambda a_, g_, l_: stack(a_ * jnp.exp(g_ - l_)), av, g, lw)
    rts = each(lambda r_, e_: stack(r_ * e_), r, e_g)
    bts = each(lambda b_, e_: stack(b_ * e_), bv, e_ng)
    kts = each(lambda k_, e_: stack(k_ * e_), k2, e_ng)
    bes = each(lambda b_, e_: stack(b_ * e_).astype(BF16), bv, e_d)
    kes = each(lambda k_, e_: stack(k_ * e_).astype(BF16), k2, e_d)
    vs = each(stack, v)
    vsb = [x.astype(BF16) for x in vs]

    p = each(lambda a_, r_, b_, k_: _dot_nt(jnp.concatenate([a_, r_], axis=0).astype(BF16),
                                            jnp.concatenate([b_, k_], axis=0).astype(BF16)),
             ats, rts, bts, kts)
    a_ab = [jnp.where(c["strict"], x[:n, :n], 0.0) for x in p]
    a_kr = [jnp.concatenate([jnp.where(c["strict"], x[:n, n:], 0.0),
                             jnp.where(c["incl"], x[n:, n:], 0.0)], axis=0).astype(BF16) for x in p]
    a_rb = [jnp.where(c["incl"], x[n:, :n], 0.0).astype(BF16) for x in p]

    inv = [c["eye"] + x for x in a_ab]
    q = a_ab
    for _ in range(5):
        qb = [x.astype(BF16) for x in q]
        q = [_dot(x, x) for x in qb]
        inv = each(lambda i_, q_: i_ + _dot(i_.astype(BF16), q_.astype(BF16)), inv, q)

    akv_rkv = each(_dot, a_kr, vsb)
    tt = each(lambda i_, a_, x_: _dot(i_.astype(BF16), jnp.concatenate([a_, x_[:n]], axis=1).astype(BF16)),
              inv, ats, akv_rkv)
    rr = each(lambda a_, t_: _dot(a_, t_.astype(BF16)), a_rb, tt)
    rq = each(lambda r_, x_: (r_ + x_[:, :LANES]).astype(BF16), rts, rr)
    oc = each(lambda x_, y_: x_[:, LANES:] + y_[n:], rr, akv_rkv)
    bb = each(lambda t_, b_: _dot(t_.T.astype(BF16), b_), tt, bes)
    bta = [x[:LANES].astype(BF16) for x in bb]
    nct = each(lambda x_, v_, k_: x_[LANES:] + _dot(v_.T.astype(BF16), k_), bb, vs, kes)
    e_gl = each(jnp.exp, gl)

    outs = []
    for i in range(nc):
        stb = st.astype(BF16)
        o_stack = _dot_nt(rq[i], stb) + oc[i]
        st = st * e_gl[i] + _dot(stb, bta[i]) + nct[i]
        outs.append(o_stack[:RWKV_CHUNK] + o_stack[RWKV_CHUNK:])
    return outs, st


def _rwkv_kernel(r_ref, k_ref, v_ref, low_ref, mur_ref, muk_ref, muv_ref, mul_ref,
                 ww_ref, wa_ref, wg_ref, w0_ref, a0_ref, kk_ref, ka_ref, rk_ref, lnw_ref, lnb_ref,
                 y_ref, cr_ref, ck_ref, cv_ref, cl_ref, st_ref, *, tb):
    @pl.when(pl.program_id(2) == 0)
    def _():
        cr_ref[...] = jnp.zeros_like(cr_ref)
        ck_ref[...] = jnp.zeros_like(ck_ref)
        cv_ref[...] = jnp.zeros_like(cv_ref)
        cl_ref[...] = jnp.zeros_like(cl_ref)
        st_ref[...] = jnp.zeros_like(st_ref)

    def lerp(x_ref, c_ref, mu_ref):
        x = x_ref[0]
        prev = _prev_rows(x, c_ref[...])
        c_ref[...] = x[tb - 8:tb, :]
        return x + (prev - x) * mu_ref[...]

    r = lerp(r_ref, cr_ref, mur_ref)
    k = lerp(k_ref, ck_ref, muk_ref)
    v = lerp(v_ref, cv_ref, muv_ref)
    low = lerp(low_ref, cl_ref, mul_ref)
    wa_in = low[:, :LANES]
    xw = w0_ref[...] + _dot(jnp.tanh(wa_in).astype(BF16), ww_ref[...])
    lw = -jnp.exp(jnp.float32(-0.5)) * jax.nn.sigmoid(xw)
    a = jax.nn.sigmoid(a0_ref[...] + _dot(wa_in.astype(BF16), wa_ref[...]))
    gate = _dot(jax.nn.sigmoid(low[:, LANES:]).astype(BF16), wg_ref[...])

    lane = lax.broadcasted_iota(jnp.int32, (1, LANES), 1)
    m0 = (lane < RWKV_HEAD_DIM).astype(F32)
    m1 = 1.0 - m0
    ri = lax.broadcasted_iota(jnp.int32, (LANES, LANES), 0)
    ci = lax.broadcasted_iota(jnp.int32, (LANES, LANES), 1)
    same = (ri < RWKV_HEAD_DIM) == (ci < RWKV_HEAD_DIM)
    ones_bd = same.astype(F32).astype(BF16)
    ti = lax.broadcasted_iota(jnp.int32, (RWKV_CHUNK, RWKV_CHUNK), 0)
    si = lax.broadcasted_iota(jnp.int32, (RWKV_CHUNK, RWKV_CHUNK), 1)
    consts = dict(
        m0=m0, m1=m1,
        strict=same & (ri > ci), incl=same & (ri >= ci),
        eye=(ri == ci).astype(F32),
        tri=(ti >= si).astype(F32).astype(BF16),
    )

    kk = k * kk_ref[...]
    kk = kk / jnp.maximum(jnp.sqrt(_seg_sum(kk * kk, ones_bd)), 1e-12)
    k2 = k * (1.0 + (a - 1.0) * ka_ref[...])
    av = -kk
    bv = kk * a

    chunks = lambda x: [x[i * RWKV_CHUNK:(i + 1) * RWKV_CHUNK] for i in range(tb // RWKV_CHUNK)]
    outs, st = _rwkv_chunks(chunks(lw), chunks(r), chunks(k2), chunks(v), chunks(av), chunks(bv), st_ref[...], consts)
    st_ref[...] = st
    o = jnp.concatenate(outs, axis=0)

    inv_n = 1.0 / RWKV_HEAD_DIM
    mean = _seg_sum(o, ones_bd) * inv_n
    oc = o - mean
    var = _seg_sum(oc * oc, ones_bd) * inv_n
    o = oc * lax.rsqrt(var + RWKV_GN_EPS) * lnw_ref[...] + lnb_ref[...]
    bonus = _seg_sum(r * k2 * rk_ref[...], ones_bd) * v
    y_ref[0] = ((o + bonus) * gate).astype(y_ref.dtype)


def _rwkv(proj3, mu, ww, wa, wg, w0, a0, k_k, k_a, r_k, ln_w, ln_b, *, tb=512):
    b, t, _ = proj3.shape
    npairs = RWKV_WIDTH // LANES
    off_k = RWKV_WIDTH // LANES
    off_v = 2 * RWKV_WIDTH // LANES
    off_low = 3 * RWKV_WIDTH // RWKV_LOW
    vec = lambda off: pl.BlockSpec((1, LANES), lambda bi, p, ti: (0, off + p))
    pvec = pl.BlockSpec((1, LANES), lambda bi, p, ti: (0, p))
    wspec = pl.BlockSpec((LANES, LANES), lambda bi, p, ti: (0, p))
    return pl.pallas_call(
        functools.partial(_rwkv_kernel, tb=tb),
        grid=(b, npairs, t // tb),
        in_specs=[
            pl.BlockSpec((1, tb, LANES), lambda bi, p, ti: (bi, ti, p)),
            pl.BlockSpec((1, tb, LANES), lambda bi, p, ti: (bi, ti, off_k + p)),
            pl.BlockSpec((1, tb, LANES), lambda bi, p, ti: (bi, ti, off_v + p)),
            pl.BlockSpec((1, tb, RWKV_LOW), lambda bi, p, ti: (bi, ti, off_low)),
            vec(0), vec(off_k), vec(off_v),
            pl.BlockSpec((1, RWKV_LOW), lambda bi, p, ti: (0, off_low)),
            wspec, wspec, wspec,
            pvec, pvec, pvec, pvec, pvec, pvec, pvec,
        ],
        out_specs=pl.BlockSpec((1, tb, LANES), lambda bi, p, ti: (bi, ti, p)),
        out_shape=jax.ShapeDtypeStruct((b, t, RWKV_WIDTH), BF16),
        scratch_shapes=[pltpu.VMEM((8, LANES), F32), pltpu.VMEM((8, LANES), F32), pltpu.VMEM((8, LANES), F32),
                        pltpu.VMEM((8, RWKV_LOW), F32), pltpu.VMEM((LANES, LANES), F32)],
        compiler_params=pltpu.CompilerParams(
            dimension_semantics=("parallel", "parallel", "arbitrary"), vmem_limit_bytes=VMEM_LIMIT),
        name="rwkv7",
    )(proj3, proj3, proj3, proj3, mu, mu, mu, mu, ww, wa, wg, w0, a0, k_k, k_a, r_k, ln_w, ln_b)


def _conv_kernel(c_ref, x_ref, b_ref, w_ref, y_ref, carry_ref):
    @pl.when(pl.program_id(2) == 0)
    def _():
        carry_ref[...] = jnp.zeros_like(carry_ref)

    z = c_ref[0] * x_ref[0]
    tb = z.shape[0]
    w = w_ref[...]
    y = z * w[CONV_K - 1:CONV_K, :]
    for j in range(1, CONV_K):
        y = y + _prev_rows(z, carry_ref[...], j) * w[CONV_K - 1 - j:CONV_K - j, :]
    carry_ref[...] = z[tb - 8:tb, :]
    y_ref[0] = (b_ref[0] * y).astype(y_ref.dtype)


def _conv(proj3, conv_w, *, tb=512):
    b, t, _ = proj3.shape
    nblk = CONV_WIDTH // LANES
    off = RWKV_SHIFT_WIDTH // LANES
    spec = lambda o: pl.BlockSpec((1, tb, LANES), lambda bi, p, ti: (bi, ti, o + p))
    return pl.pallas_call(
        _conv_kernel,
        grid=(b, nblk, t // tb),
        in_specs=[spec(off), spec(off + nblk), spec(off + 2 * nblk),
                  pl.BlockSpec((CONV_K, LANES), lambda bi, p, ti: (0, p))],
        out_specs=pl.BlockSpec((1, tb, LANES), lambda bi, p, ti: (bi, ti, p)),
        out_shape=jax.ShapeDtypeStruct((b, t, CONV_WIDTH), BF16),
        scratch_shapes=[pltpu.VMEM((8, LANES), F32)],
        compiler_params=pltpu.CompilerParams(
            dimension_semantics=("parallel", "parallel", "arbitrary"), vmem_limit_bytes=VMEM_LIMIT),
        name="shortconv",
    )(proj3, proj3, proj3, conv_w)


def _hgrn_kernel(q_ref, f_ref, i_ref, og_ref, lb_ref, nw_ref, y_ref, st_ref, *, layer, tb):
    @pl.when(pl.program_id(2) == 0)
    def _():
        st_ref[...] = jnp.zeros_like(st_ref)

    lbp = lb_ref[...]
    e = jnp.exp(lbp - jnp.max(lbp, axis=0, keepdims=True))
    pr = e / jnp.sum(e, axis=0, keepdims=True)
    lb = jnp.zeros((1, LANES), F32)
    for l in range(1, layer + 1):
        lb = lb + pr[l:l + 1, :]

    qr = q_ref[0]
    q = qr * jax.nn.sigmoid(qr)
    f = f_ref[0]
    forget = lb + (1.0 - lb) * jax.nn.sigmoid(f)
    logf = jnp.log(jnp.maximum(forget, HGRN_MIN_FORGET))
    kx = (1.0 - lb) * jax.nn.sigmoid(-f)
    iv = i_ref[0]

    n = HGRN_BLOCK
    nb = tb // n
    ti = lax.broadcasted_iota(jnp.int32, (tb, tb), 0)
    si = lax.broadcasted_iota(jnp.int32, (tb, tb), 1)
    same = (ti ^ si) < n
    sums = jnp.concatenate([(same & (ti >= si)).astype(F32).astype(BF16), same.astype(F32).astype(BF16)], axis=0)
    hi, lo = _split_bf16(logf)
    gg = _dot(sums, jnp.concatenate([hi, lo], axis=1))
    g = gg[:tb, :LANES] + gg[:tb, LANES:]
    gl = gg[tb:, :LANES] + gg[tb:, LANES:]
    qd = (q * jnp.exp(g)).astype(BF16)
    kd = (kx * jnp.exp(gl - g)).astype(BF16)
    dec = jnp.exp(gl)
    trow = lax.broadcasted_iota(jnp.int32, (n, 1), 0)
    blocks = lambda x: [x[i * n:(i + 1) * n] for i in range(nb)]
    gb, qb, kb, ib, qdb, kdb = (blocks(x) for x in (g, q, kx, iv, qd, kd))

    upd = [_dot(i_.T.astype(BF16), k_) for i_, k_ in zip(ib, kdb)]
    st = st_ref[...]
    sts = []
    for i in range(nb):
        sts.append(st.astype(BF16))
        st = st * dec[i * n:i * n + 1, :] + upd[i]
    st_ref[...] = st
    outs = [_dot_nt(q_, s_) for q_, s_ in zip(qdb, sts)]
    for s in range(n):
        for i in range(nb):
            w = jnp.sum(qb[i] * kb[i][s:s + 1, :] * jnp.exp(gb[i] - gb[i][s:s + 1, :]), axis=-1, keepdims=True)
            outs[i] = outs[i] + jnp.where(trow >= s, w, 0.0) * ib[i][s:s + 1, :]
    o = jnp.concatenate(outs, axis=0)
    o = o * lax.rsqrt(jnp.mean(o * o, axis=-1, keepdims=True) + HGRN_NORM_EPS) * nw_ref[...]
    og = og_ref[0]
    y_ref[0] = (o * (og * jax.nn.sigmoid(og))).astype(y_ref.dtype)


def _hgrn(proj3, hgrn_lb, norm_w, *, layer, tb=256):
    b, t, _ = proj3.shape
    nh = HGRN_WIDTH // HGRN_HEAD_DIM
    off = (RWKV_SHIFT_WIDTH + 3 * CONV_WIDTH) // LANES
    spec = lambda o: pl.BlockSpec((1, tb, LANES), lambda bi, p, ti: (bi, ti, o + p))
    depth = hgrn_lb.shape[0]
    return pl.pallas_call(
        functools.partial(_hgrn_kernel, layer=layer, tb=tb),
        grid=(b, nh, t // tb),
        in_specs=[spec(off), spec(off + nh), spec(off + 2 * nh), spec(off + 3 * nh),
                  pl.BlockSpec((depth, LANES), lambda bi, p, ti: (0, p)),
                  pl.BlockSpec((1, LANES), lambda bi, p, ti: (0, p))],
        out_specs=pl.BlockSpec((1, tb, LANES), lambda bi, p, ti: (bi, ti, p)),
        out_shape=jax.ShapeDtypeStruct((b, t, HGRN_WIDTH), BF16),
        scratch_shapes=[pltpu.VMEM((LANES, LANES), F32)],
        compiler_params=pltpu.CompilerParams(
            dimension_semantics=("parallel", "parallel", "arbitrary"), vmem_limit_bytes=VMEM_LIMIT),
        name="hgrn2",
    )(proj3, proj3, proj3, proj3, hgrn_lb, norm_w)


def _pad_cols(w, n):
    return jnp.pad(w, ((0, 0), (0, n - w.shape[1])))


def kernel(x, ffn1_norm, ffn1_w_gu, ffn1_w_down, mix_norm, w_in, rwkv_mu, rwkv_w_up, rwkv_w0,
           rwkv_a_up, rwkv_a0, rwkv_g_up, rwkv_k_k, rwkv_k_a, rwkv_r_k, rwkv_ln_w, rwkv_ln_b,
           conv_w, hgrn_lb, hgrn_norm, w_out, ffn2_norm, ffn2_w_gu, ffn2_w_down, final_norm):
    b, t, d = x.shape
    depth = ffn1_norm.shape[0]
    d_ff = ffn1_w_down.shape[1]
    ffp = -(-d_ff // 512) * 512
    m = b * t
    h = x.reshape(m, d)
    row = lambda v: v.reshape(1, -1)
    final_w = row(final_norm)

    def ffn_weights(w_gu, w_down):
        wg = _pad_cols(w_gu[:, :d_ff], ffp).astype(BF16)
        wu = _pad_cols(w_gu[:, d_ff:], ffp).astype(BF16)
        wd = jnp.pad(w_down, ((0, ffp - d_ff), (0, 0))).astype(BF16)
        return wg, wu, wd

    zeros_low = jnp.zeros((LANES - DECAY_RANK, RWKV_WIDTH), F32)
    for l in range(depth):
        wg, wu, wd = ffn_weights(ffn1_w_gu[l], ffn1_w_down[l])
        h = _ffn(h, row(ffn1_norm[l]), wg, wu, wd, final_w, final=False)

        proj = _proj(h, row(mix_norm[l]), w_in[l].astype(BF16))
        proj3 = proj.reshape(b, t, -1)
        ww = jnp.concatenate([rwkv_w_up[l], zeros_low], axis=0).astype(BF16)
        wa = jnp.concatenate([zeros_low, rwkv_a_up[l]], axis=0).astype(BF16)
        y_r = _rwkv(proj3, row(rwkv_mu[l]), ww, wa, rwkv_g_up[l].astype(BF16),
                    row(rwkv_w0[l]), row(rwkv_a0[l]), row(rwkv_k_k[l]), row(rwkv_k_a[l]),
                    row(rwkv_r_k[l]), row(rwkv_ln_w[l]), row(rwkv_ln_b[l]))
        y_c = _conv(proj3, conv_w[l])
        y_h = _hgrn(proj3, hgrn_lb, row(hgrn_norm[l]), layer=l)
        wo = w_out[l].astype(BF16)
        h = _outproj(h, y_r.reshape(m, -1), y_c.reshape(m, -1), y_h.reshape(m, -1),
                     wo[:RWKV_WIDTH], wo[RWKV_WIDTH:RWKV_WIDTH + CONV_WIDTH], wo[RWKV_WIDTH + CONV_WIDTH:])

        wg, wu, wd = ffn_weights(ffn2_w_gu[l], ffn2_w_down[l])
        h = _ffn(h, row(ffn2_norm[l]), wg, wu, wd, final_w, final=(l == depth - 1))
    return h.reshape(b, t, d)
```

```python
import functools

import jax
import jax.numpy as jnp
from jax import lax
from jax.experimental import pallas as pl
from jax.experimental.pallas import tpu as pltpu

F32 = jnp.float32
BF16 = jnp.bfloat16

RWKV_HEAD_DIM = 64
RWKV_WIDTH = 1024
DECAY_RANK = 64
AAA_RANK = 64
GATE_RANK = 128
RWKV_LOW = DECAY_RANK + AAA_RANK + GATE_RANK
RWKV_SHIFT_WIDTH = 3 * RWKV_WIDTH + RWKV_LOW
RWKV_GN_EPS = 64e-5
CONV_WIDTH = 512
CONV_K = 3
HGRN_HEAD_DIM = 128
HGRN_WIDTH = 512
HGRN_NORM_EPS = 1e-5
HGRN_MIN_FORGET = 1e-30
FFN_RESIDUAL_SCALE = 0.5
RMS_EPS = 1e-6

LANES = 128
RWKV_CHUNK = 64
HGRN_BLOCK = 16
VMEM_LIMIT = 56 * 1024 * 1024


def _dot(a, b):
    return jnp.dot(a, b, preferred_element_type=F32)


def _dot_nt(a, b):
    return lax.dot_general(a, b, (((1,), (1,)), ((), ())), preferred_element_type=F32)


def _rms(x, w):
    return x * lax.rsqrt(jnp.mean(x * x, axis=-1, keepdims=True) + RMS_EPS) * w


def _split_bf16(x):
    hi = x.astype(BF16)
    lo = (x - hi.astype(F32)).astype(BF16)
    return hi, lo


def _ffn_kernel(x_ref, nw_ref, wg_ref, wu_ref, wd_ref, fw_ref, o_ref, xn_ref, *, final):
    j = pl.program_id(1)

    @pl.when(j == 0)
    def _():
        xn_ref[...] = _rms(x_ref[...], nw_ref[...]).astype(BF16)
        o_ref[...] = jnp.zeros_like(o_ref)

    xn = xn_ref[...]
    g = _dot(xn, wg_ref[...])
    u = _dot(xn, wu_ref[...])
    a = (g * jax.nn.sigmoid(g) * u).astype(BF16)
    o_ref[...] += _dot(a, wd_ref[...])

    @pl.when(j == pl.num_programs(1) - 1)
    def _():
        h = x_ref[...] + FFN_RESIDUAL_SCALE * o_ref[...]
        if final:
            h = _rms(h, fw_ref[...])
        o_ref[...] = h


def _ffn(h, norm_w, wg, wu, wd, final_w, *, final, tm=None, tf=512):
    m, d = h.shape
    if tm is None:
        tm = 512 if final else 1024
    ffp = wg.shape[1]
    grid = (m // tm, ffp // tf)
    return pl.pallas_call(
        functools.partial(_ffn_kernel, final=final),
        grid=grid,
        in_specs=[
            pl.BlockSpec((tm, d), lambda i, j: (i, 0)),
            pl.BlockSpec((1, d), lambda i, j: (0, 0)),
            pl.BlockSpec((d, tf), lambda i, j: (0, j)),
            pl.BlockSpec((d, tf), lambda i, j: (0, j)),
            pl.BlockSpec((tf, d), lambda i, j: (j, 0)),
            pl.BlockSpec((1, d), lambda i, j: (0, 0)),
        ],
        out_specs=pl.BlockSpec((tm, d), lambda i, j: (i, 0)),
        out_shape=jax.ShapeDtypeStruct((m, d), F32),
        scratch_shapes=[pltpu.VMEM((tm, d), BF16)],
        compiler_params=pltpu.CompilerParams(
            dimension_semantics=("parallel", "arbitrary"), vmem_limit_bytes=VMEM_LIMIT),
        name="ffn",
    )(h, norm_w, wg, wu, wd, final_w)


def _proj_kernel(x_ref, nw_ref, w_ref, o_ref, xn_ref):
    @pl.when(pl.program_id(1) == 0)
    def _():
        xn_ref[...] = _rms(x_ref[...], nw_ref[...]).astype(BF16)

    o_ref[...] = _dot(xn_ref[...], w_ref[...])


def _proj(h, norm_w, w, *, tm=1024, tn=768):
    m, d = h.shape
    n = w.shape[1]
    return pl.pallas_call(
        _proj_kernel,
        grid=(m // tm, n // tn),
        in_specs=[
            pl.BlockSpec((tm, d), lambda i, j: (i, 0)),
            pl.BlockSpec((1, d), lambda i, j: (0, 0)),
            pl.BlockSpec((d, tn), lambda i, j: (0, j)),
        ],
        out_specs=pl.BlockSpec((tm, tn), lambda i, j: (i, j)),
        out_shape=jax.ShapeDtypeStruct((m, n), F32),
        scratch_shapes=[pltpu.VMEM((tm, d), BF16)],
        compiler_params=pltpu.CompilerParams(
            dimension_semantics=("parallel", "arbitrary"), vmem_limit_bytes=VMEM_LIMIT),
        name="proj",
    )(h, norm_w, w)


def _outproj_kernel(h_ref, yr_ref, yc_ref, yh_ref, wr_ref, wc_ref, wh_ref, o_ref):
    acc = _dot(yr_ref[...], wr_ref[...])
    acc += _dot(yc_ref[...], wc_ref[...])
    acc += _dot(yh_ref[...], wh_ref[...])
    o_ref[...] = h_ref[...] + acc


def _outproj(h, yr, yc, yh, wr, wc, wh, *, tm=512):
    m, d = h.shape
    row = lambda i: (i, 0)
    fixed = lambda i: (0, 0)
    return pl.pallas_call(
        _outproj_kernel,
        grid=(m // tm,),
        in_specs=[
            pl.BlockSpec((tm, d), row),
            pl.BlockSpec((tm, yr.shape[1]), row),
            pl.BlockSpec((tm, yc.shape[1]), row),
            pl.BlockSpec((tm, yh.shape[1]), row),
            pl.BlockSpec(wr.shape, fixed),
            pl.BlockSpec(wc.shape, fixed),
            pl.BlockSpec(wh.shape, fixed),
        ],
        out_specs=pl.BlockSpec((tm, d), row),
        out_shape=jax.ShapeDtypeStruct((m, d), F32),
        compiler_params=pltpu.CompilerParams(
            dimension_semantics=("parallel",), vmem_limit_bytes=VMEM_LIMIT),
        name="outproj",
    )(h, yr, yc, yh, wr, wc, wh)


def _prev_rows(x, carry, k=1):
    rows = lax.broadcasted_iota(jnp.int32, (8, x.shape[1]), 0)
    prev = pltpu.roll(x, k, 0)
    head = jnp.where(rows < k, pltpu.roll(carry, k, 0), prev[:8])
    return jnp.concatenate([head, prev[8:]], axis=0)


def _seg_sum(x, ones_bd):
    return _dot(x.astype(BF16), ones_bd)


def _rwkv_prepare(lw, r, k2, v, av, bv, c, tick):
    n = 2 * RWKV_CHUNK
    m0, m1 = c["m0"], c["m1"]
    stack = lambda x: jnp.concatenate([x * m0, x * m1], axis=0)
    each = lambda f, *ls: [f(*xs) for xs in zip(*ls)]

    def cumsum(x):
        hi, lo = _split_bf16(x)
        gcat = _dot(c["tri"], jnp.concatenate([hi, lo], axis=1))
        return gcat[:, :LANES] + gcat[:, LANES:]

    g = each(cumsum, lw)
    gl = [x[RWKV_CHUNK - 1:RWKV_CHUNK, :] for x in g]
    e_g = each(jnp.exp, g)
    e_ng = each(lambda x: jnp.exp(-x), g)
    e_d = each(lambda x, y: jnp.exp(x - y), gl, g)
    ats = each(lambda a_, g_, l_: stack(a_ * jnp.exp(g_ - l_)), av, g, lw)
    rts = each(lambda r_, e_: stack(r_ * e_), r, e_g)
    bts = each(lambda b_, e_: stack(b_ * e_), bv, e_ng)
    kts = each(lambda k_, e_: stack(k_ * e_), k2, e_ng)
    bes = each(lambda b_, e_: stack(b_ * e_).astype(BF16), bv, e_d)
    kes = each(lambda k_, e_: stack(k_ * e_).astype(BF16), k2, e_d)
    vs = each(stack, v)
    vsb = [x.astype(BF16) for x in vs]

    p = each(lambda a_, r_, b_, k_: _dot_nt(jnp.concatenate([a_, r_], axis=0).astype(BF16),
                                            jnp.concatenate([b_, k_], axis=0).astype(BF16)),
             ats, rts, bts, kts)
    a_ab = [jnp.where(c["strict"], x[:n, :n], 0.0) for x in p]
    a_kr = [jnp.concatenate([jnp.where(c["strict"], x[:n, n:], 0.0),
                             jnp.where(c["incl"], x[n:, n:], 0.0)], axis=0).astype(BF16) for x in p]
    a_rb = [jnp.where(c["incl"], x[n:, :n], 0.0).astype(BF16) for x in p]
    tick()

    inv = [c["eye"] + x for x in a_ab]
    q = a_ab
    for _ in range(5):
        qb = [x.astype(BF16) for x in q]
        q = [_dot(x, x) for x in qb]
        inv = each(lambda i_, q_: i_ + _dot(i_.astype(BF16), q_.astype(BF16)), inv, q)
        tick()

    akv_rkv = each(_dot, a_kr, vsb)
    tick()
    tt = each(lambda i_, a_, x_: _dot(i_.astype(BF16), jnp.concatenate([a_, x_[:n]], axis=1).astype(BF16)),
              inv, ats, akv_rkv)
    tick()
    rr = each(lambda a_, t_: _dot(a_, t_.astype(BF16)), a_rb, tt)
    rq = each(lambda r_, x_: (r_ + x_[:, :LANES]).astype(BF16), rts, rr)
    oc = each(lambda x_, y_: x_[:, LANES:] + y_[n:], rr, akv_rkv)
    bb = each(lambda t_, b_: _dot(t_.T.astype(BF16), b_), tt, bes)
    bta = [x[:LANES].astype(BF16) for x in bb]
    nct = each(lambda x_, v_, k_: x_[LANES:] + _dot(v_.T.astype(BF16), k_), bb, vs, kes)
    e_gl = each(jnp.exp, gl)
    return rq, oc, bta, nct, e_gl


def _rwkv_kernel(r_ref, k_ref, v_ref, low_ref, mur_ref, muk_ref, muv_ref, mul_ref,
                 ww_ref, wa_ref, wg_ref, w0_ref, a0_ref, kk_ref, ka_ref, rk_ref, lnw_ref, lnb_ref,
                 y_ref, cr_ref, ck_ref, cv_ref, cl_ref, st_ref,
                 rq_ref, oc_ref, bta_ref, nct_ref, egl_ref, bonus_ref, gate_ref, *, tb, nt):
    s = pl.program_id(0)
    nc = tb // RWKV_CHUNK

    @pl.when(s == 0)
    def _():
        for ref in (cr_ref, ck_ref, cv_ref, cl_ref, st_ref, rq_ref, oc_ref, bta_ref, nct_ref, egl_ref,
                    bonus_ref, gate_ref):
            ref[...] = jnp.zeros_like(ref)

    lane = lax.broadcasted_iota(jnp.int32, (1, LANES), 1)
    m0 = (lane < RWKV_HEAD_DIM).astype(F32)
    m1 = 1.0 - m0
    ri = lax.broadcasted_iota(jnp.int32, (LANES, LANES), 0)
    ci = lax.broadcasted_iota(jnp.int32, (LANES, LANES), 1)
    same = (ri < RWKV_HEAD_DIM) == (ci < RWKV_HEAD_DIM)
    ones_bd = same.astype(F32).astype(BF16)
    ti = lax.broadcasted_iota(jnp.int32, (RWKV_CHUNK, RWKV_CHUNK), 0)
    si = lax.broadcasted_iota(jnp.int32, (RWKV_CHUNK, RWKV_CHUNK), 1)
    consts = dict(
        m0=m0, m1=m1,
        strict=same & (ri > ci), incl=same & (ri >= ci),
        eye=(ri == ci).astype(F32),
        tri=(ti >= si).astype(F32).astype(BF16),
    )

    rd = lax.rem(s + 1, 2)
    prev_first = lax.rem(jnp.maximum(s - 1, 0), nt) == 0
    state = [st_ref[...] * jnp.where(prev_first, 0.0, 1.0)]
    outs = []
    inv_n = 1.0 / RWKV_HEAD_DIM
    own = same.astype(F32)

    def tick():
        i = len(outs)
        if i == nc:
            return
        st = state[0]
        stb = st.astype(BF16)
        o_stack = _dot_nt(rq_ref[rd, i], stb) + oc_ref[rd, i]
        state[0] = st * egl_ref[rd, i][0:1, :] + _dot(stb, bta_ref[rd, i]) + nct_ref[rd, i]
        mean = jnp.sum(o_stack, axis=-1, keepdims=True) * inv_n
        cen = (o_stack - mean) * own
        var = jnp.sum(cen * cen, axis=-1, keepdims=True) * inv_n
        on = cen * lax.rsqrt(var + RWKV_GN_EPS)
        outs.append(on[:RWKV_CHUNK] + on[RWKV_CHUNK:])

    wr = lax.rem(s, 2)
    keep = jnp.where(lax.rem(jnp.minimum(s, pl.num_programs(0) - 2), nt) == 0, 0.0, 1.0)

    def lerp(x_ref, c_ref, mu_ref):
        x = x_ref[0]
        prev = _prev_rows(x, c_ref[...] * keep)
        c_ref[...] = x[tb - 8:tb, :]
        return x + (prev - x) * mu_ref[...]

    r = lerp(r_ref, cr_ref, mur_ref)
    k = lerp(k_ref, ck_ref, muk_ref)
    v = lerp(v_ref, cv_ref, muv_ref)
    low = lerp(low_ref, cl_ref, mul_ref)
    wa_in = low[:, :LANES]
    xw = w0_ref[...] + _dot(jnp.tanh(wa_in).astype(BF16), ww_ref[...])
    lw = -jnp.exp(jnp.float32(-0.5)) * jax.nn.sigmoid(xw)
    a = jax.nn.sigmoid(a0_ref[...] + _dot(wa_in.astype(BF16), wa_ref[...]))
    gate = _dot(jax.nn.sigmoid(low[:, LANES:]).astype(BF16), wg_ref[...])

    kk = k * kk_ref[...]
    kk = kk / jnp.maximum(jnp.sqrt(_seg_sum(kk * kk, ones_bd)), 1e-12)
    k2 = k * (1.0 + (a - 1.0) * ka_ref[...])
    av = -kk
    bv = kk * a
    bonus = _seg_sum(r * k2 * rk_ref[...], ones_bd) * v

    chunks = lambda x: [x[i * RWKV_CHUNK:(i + 1) * RWKV_CHUNK] for i in range(nc)]
    rq, oc, bta, nct, e_gl = _rwkv_prepare(chunks(lw), chunks(r), chunks(k2), chunks(v), chunks(av), chunks(bv),
                                           consts, tick)
    while len(outs) < nc:
        tick()

    st_ref[...] = state[0]
    o = jnp.concatenate(outs, axis=0) * lnw_ref[...] + lnb_ref[...]
    y_ref[0] = ((o + bonus_ref[rd]) * gate_ref[rd]).astype(y_ref.dtype)

    gate_ref[wr] = gate
    bonus_ref[wr] = bonus
    for i in range(nc):
        rq_ref[wr, i] = rq[i]
        oc_ref[wr, i] = oc[i]
        bta_ref[wr, i] = bta[i]
        nct_ref[wr, i] = nct[i]
        egl_ref[wr, i] = jnp.broadcast_to(e_gl[i], (8, LANES))


def _rwkv(proj3, mu, ww, wa, wg, w0, a0, k_k, k_a, r_k, ln_w, ln_b, *, tb=512):
    b, t, _ = proj3.shape
    npairs = RWKV_WIDTH // LANES
    nt = t // tb
    nc = tb // RWKV_CHUNK
    nblocks = b * npairs * nt
    off_k = RWKV_WIDTH // LANES
    off_v = 2 * RWKV_WIDTH // LANES
    off_low = 3 * RWKV_WIDTH // RWKV_LOW

    def where(item):
        return item // (nt * npairs), item % nt, (item // nt) % npairs

    cur = lambda s: where(jnp.minimum(s, nblocks - 1))
    lag = lambda s: where(jnp.maximum(s - 1, 0))
    act = lambda off: pl.BlockSpec((1, tb, LANES), lambda s: (cur(s)[0], cur(s)[1], off + cur(s)[2]))
    vec = lambda off: pl.BlockSpec((1, LANES), lambda s: (0, off + cur(s)[2]))
    lagvec = pl.BlockSpec((1, LANES), lambda s: (0, lag(s)[2]))
    wspec = pl.BlockSpec((LANES, LANES), lambda s: (0, cur(s)[2]))
    return pl.pallas_call(
        functools.partial(_rwkv_kernel, tb=tb, nt=nt),
        grid=(nblocks + 1,),
        in_specs=[
            act(0), act(off_k), act(off_v),
            pl.BlockSpec((1, tb, RWKV_LOW), lambda s: (cur(s)[0], cur(s)[1], off_low)),
            vec(0), vec(off_k), vec(off_v),
            pl.BlockSpec((1, RWKV_LOW), lambda s: (0, off_low)),
            wspec, wspec, wspec,
            vec(0), vec(0), vec(0), vec(0), vec(0), lagvec, lagvec,
        ],
        out_specs=pl.BlockSpec((1, tb, LANES), lambda s: lag(s)),
        out_shape=jax.ShapeDtypeStruct((b, t, RWKV_WIDTH), BF16),
        scratch_shapes=[pltpu.VMEM((8, LANES), F32), pltpu.VMEM((8, LANES), F32), pltpu.VMEM((8, LANES), F32),
                        pltpu.VMEM((8, RWKV_LOW), F32), pltpu.VMEM((LANES, LANES), F32),
                        pltpu.VMEM((2, nc, LANES, LANES), BF16), pltpu.VMEM((2, nc, LANES, LANES), F32),
                        pltpu.VMEM((2, nc, LANES, LANES), BF16), pltpu.VMEM((2, nc, LANES, LANES), F32),
                        pltpu.VMEM((2, nc, 8, LANES), F32),
                        pltpu.VMEM((2, tb, LANES), F32), pltpu.VMEM((2, tb, LANES), F32)],
        compiler_params=pltpu.CompilerParams(
            dimension_semantics=("arbitrary",), vmem_limit_bytes=VMEM_LIMIT),
        name="rwkv7",
    )(proj3, proj3, proj3, proj3, mu, mu, mu, mu, ww, wa, wg, w0, a0, k_k, k_a, r_k, ln_w, ln_b)


def _conv_kernel(c_ref, x_ref, b_ref, w_ref, y_ref, carry_ref):
    @pl.when(pl.program_id(2) == 0)
    def _():
        carry_ref[...] = jnp.zeros_like(carry_ref)

    z = c_ref[0] * x_ref[0]
    tb = z.shape[0]
    w = w_ref[...]
    y = z * w[CONV_K - 1:CONV_K, :]
    for j in range(1, CONV_K):
        y = y + _prev_rows(z, carry_ref[...], j) * w[CONV_K - 1 - j:CONV_K - j, :]
    carry_ref[...] = z[tb - 8:tb, :]
    y_ref[0] = (b_ref[0] * y).astype(y_ref.dtype)


def _conv(proj3, conv_w, *, tb=512):
    b, t, _ = proj3.shape
    nblk = CONV_WIDTH // LANES
    off = RWKV_SHIFT_WIDTH // LANES
    spec = lambda o: pl.BlockSpec((1, tb, LANES), lambda bi, p, ti: (bi, ti, o + p))
    return pl.pallas_call(
        _conv_kernel,
        grid=(b, nblk, t // tb),
        in_specs=[spec(off), spec(off + nblk), spec(off + 2 * nblk),
                  pl.BlockSpec((CONV_K, LANES), lambda bi, p, ti: (0, p))],
        out_specs=pl.BlockSpec((1, tb, LANES), lambda bi, p, ti: (bi, ti, p)),
        out_shape=jax.ShapeDtypeStruct((b, t, CONV_WIDTH), BF16),
        scratch_shapes=[pltpu.VMEM((8, LANES), F32)],
        compiler_params=pltpu.CompilerParams(
            dimension_semantics=("parallel", "parallel", "arbitrary"), vmem_limit_bytes=VMEM_LIMIT),
        name="shortconv",
    )(proj3, proj3, proj3, conv_w)


def _hgrn_kernel(q_ref, f_ref, i_ref, og_ref, lb_ref, nw_ref, y_ref, st_ref, *, layer, tb):
    @pl.when(pl.program_id(2) == 0)
    def _():
        st_ref[...] = jnp.zeros_like(st_ref)

    lbp = lb_ref[...]
    e = jnp.exp(lbp - jnp.max(lbp, axis=0, keepdims=True))
    pr = e / jnp.sum(e, axis=0, keepdims=True)
    lb = jnp.zeros((1, LANES), F32)
    for l in range(1, layer + 1):
        lb = lb + pr[l:l + 1, :]

    qr = q_ref[0]
    q = qr * jax.nn.sigmoid(qr)
    f = f_ref[0]
    forget = lb + (1.0 - lb) * jax.nn.sigmoid(f)
    logf = jnp.log(jnp.maximum(forget, HGRN_MIN_FORGET))
    kx = (1.0 - lb) * jax.nn.sigmoid(-f)
    iv = i_ref[0]

    n = HGRN_BLOCK
    nb = tb // n
    ti = lax.broadcasted_iota(jnp.int32, (tb, tb), 0)
    si = lax.broadcasted_iota(jnp.int32, (tb, tb), 1)
    same = (ti ^ si) < n
    sums = jnp.concatenate([(same & (ti >= si)).astype(F32).astype(BF16), same.astype(F32).astype(BF16)], axis=0)
    hi, lo = _split_bf16(logf)
    gg = _dot(sums, jnp.concatenate([hi, lo], axis=1))
    g = gg[:tb, :LANES] + gg[:tb, LANES:]
    gl = gg[tb:, :LANES] + gg[tb:, LANES:]
    qd = (q * jnp.exp(g)).astype(BF16)
    kd = (kx * jnp.exp(gl - g)).astype(BF16)
    dec = jnp.exp(gl)
    trow = lax.broadcasted_iota(jnp.int32, (n, 1), 0)
    blocks = lambda x: [x[i * n:(i + 1) * n] for i in range(nb)]
    gb, qb, kb, ib, qdb, kdb = (blocks(x) for x in (g, q, kx, iv, qd, kd))

    upd = [_dot(i_.T.astype(BF16), k_) for i_, k_ in zip(ib, kdb)]
    st = st_ref[...]
    sts = []
    for i in range(nb):
        sts.append(st.astype(BF16))
        st = st * dec[i * n:i * n + 1, :] + upd[i]
    st_ref[...] = st
    outs = [_dot_nt(q_, s_) for q_, s_ in zip(qdb, sts)]
    for s in range(n):
        for i in range(nb):
            w = jnp.sum(qb[i] * kb[i][s:s + 1, :] * jnp.exp(gb[i] - gb[i][s:s + 1, :]), axis=-1, keepdims=True)
            outs[i] = outs[i] + jnp.where(trow >= s, w, 0.0) * ib[i][s:s + 1, :]
    o = jnp.concatenate(outs, axis=0)
    o = o * lax.rsqrt(jnp.mean(o * o, axis=-1, keepdims=True) + HGRN_NORM_EPS) * nw_ref[...]
    og = og_ref[0]
    y_ref[0] = (o * (og * jax.nn.sigmoid(og))).astype(y_ref.dtype)


def _hgrn(proj3, hgrn_lb, norm_w, *, layer, tb=256):
    b, t, _ = proj3.shape
    nh = HGRN_WIDTH // HGRN_HEAD_DIM
    off = (RWKV_SHIFT_WIDTH + 3 * CONV_WIDTH) // LANES
    spec = lambda o: pl.BlockSpec((1, tb, LANES), lambda bi, p, ti: (bi, ti, o + p))
    depth = hgrn_lb.shape[0]
    return pl.pallas_call(
        functools.partial(_hgrn_kernel, layer=layer, tb=tb),
        grid=(b, nh, t // tb),
        in_specs=[spec(off), spec(off + nh), spec(off + 2 * nh), spec(off + 3 * nh),
                  pl.BlockSpec((depth, LANES), lambda bi, p, ti: (0, p)),
                  pl.BlockSpec((1, LANES), lambda bi, p, ti: (0, p))],
        out_specs=pl.BlockSpec((1, tb, LANES), lambda bi, p, ti: (bi, ti, p)),
        out_shape=jax.ShapeDtypeStruct((b, t, HGRN_WIDTH), BF16),
        scratch_shapes=[pltpu.VMEM((LANES, LANES), F32)],
        compiler_params=pltpu.CompilerParams(
            dimension_semantics=("parallel", "parallel", "arbitrary"), vmem_limit_bytes=VMEM_LIMIT),
        name="hgrn2",
    )(proj3, proj3, proj3, proj3, hgrn_lb, norm_w)


def _pad_cols(w, n):
    return jnp.pad(w, ((0, 0), (0, n - w.shape[1])))


def kernel(x, ffn1_norm, ffn1_w_gu, ffn1_w_down, mix_norm, w_in, rwkv_mu, rwkv_w_up, rwkv_w0,
           rwkv_a_up, rwkv_a0, rwkv_g_up, rwkv_k_k, rwkv_k_a, rwkv_r_k, rwkv_ln_w, rwkv_ln_b,
           conv_w, hgrn_lb, hgrn_norm, w_out, ffn2_norm, ffn2_w_gu, ffn2_w_down, final_norm):
    b, t, d = x.shape
    depth = ffn1_norm.shape[0]
    d_ff = ffn1_w_down.shape[1]
    ffp = -(-d_ff // 512) * 512
    m = b * t
    h = x.reshape(m, d)
    row = lambda v: v.reshape(1, -1)
    final_w = row(final_norm)

    def ffn_weights(w_gu, w_down):
        wg = _pad_cols(w_gu[:, :d_ff], ffp).astype(BF16)
        wu = _pad_cols(w_gu[:, d_ff:], ffp).astype(BF16)
        wd = jnp.pad(w_down, ((0, ffp - d_ff), (0, 0))).astype(BF16)
        return wg, wu, wd

    zeros_low = jnp.zeros((LANES - DECAY_RANK, RWKV_WIDTH), F32)
    for l in range(depth):
        wg, wu, wd = ffn_weights(ffn1_w_gu[l], ffn1_w_down[l])
        h = _ffn(h, row(ffn1_norm[l]), wg, wu, wd, final_w, final=False)

        proj = _proj(h, row(mix_norm[l]), w_in[l].astype(BF16))
        proj3 = proj.reshape(b, t, -1)
        ww = jnp.concatenate([rwkv_w_up[l], zeros_low], axis=0).astype(BF16)
        wa = jnp.concatenate([zeros_low, rwkv_a_up[l]], axis=0).astype(BF16)
        y_r = _rwkv(proj3, row(rwkv_mu[l]), ww, wa, rwkv_g_up[l].astype(BF16),
                    row(rwkv_w0[l]), row(rwkv_a0[l]), row(rwkv_k_k[l]), row(rwkv_k_a[l]),
                    row(rwkv_r_k[l]), row(rwkv_ln_w[l]), row(rwkv_ln_b[l]))
        y_c = _conv(proj3, conv_w[l])
        y_h = _hgrn(proj3, hgrn_lb, row(hgrn_norm[l]), layer=l)
        wo = w_out[l].astype(BF16)
        h = _outproj(h, y_r.reshape(m, -1), y_c.reshape(m, -1), y_h.reshape(m, -1),
                     wo[:RWKV_WIDTH], wo[RWKV_WIDTH:RWKV_WIDTH + CONV_WIDTH], wo[RWKV_WIDTH + CONV_WIDTH:])

        wg, wu, wd = ffn_weights(ffn2_w_gu[l], ffn2_w_down[l])
        h = _ffn(h, row(ffn2_norm[l]), wg, wu, wd, final_w, final=(l == depth - 1))
    return h.reshape(b, t, d)
```

```python
import functools

import jax
import jax.numpy as jnp
from jax import lax
from jax.experimental import pallas as pl
from jax.experimental.pallas import tpu as pltpu

F32 = jnp.float32
BF16 = jnp.bfloat16

RWKV_HEAD_DIM = 64
RWKV_WIDTH = 1024
DECAY_RANK = 64
AAA_RANK = 64
GATE_RANK = 128
RWKV_LOW = DECAY_RANK + AAA_RANK + GATE_RANK
RWKV_SHIFT_WIDTH = 3 * RWKV_WIDTH + RWKV_LOW
RWKV_GN_EPS = 64e-5
CONV_WIDTH = 512
CONV_K = 3
HGRN_HEAD_DIM = 128
HGRN_WIDTH = 512
HGRN_NORM_EPS = 1e-5
HGRN_MIN_FORGET = 1e-30
FFN_RESIDUAL_SCALE = 0.5
RMS_EPS = 1e-6

LANES = 128
RWKV_CHUNK = 64
HGRN_BLOCK = 16
VMEM_LIMIT = 56 * 1024 * 1024


def _dot(a, b):
    return jnp.dot(a, b, preferred_element_type=F32)


def _dot_nt(a, b):
    return lax.dot_general(a, b, (((1,), (1,)), ((), ())), preferred_element_type=F32)


def _rms(x, w):
    return x * lax.rsqrt(jnp.mean(x * x, axis=-1, keepdims=True) + RMS_EPS) * w


def _split_bf16(x):
    hi = x.astype(BF16)
    lo = (x - hi.astype(F32)).astype(BF16)
    return hi, lo


def _ffn_kernel(x_ref, nw_ref, wg_ref, wu_ref, wd_ref, fw_ref, o_ref, xn_ref, *, final):
    j = pl.program_id(1)

    @pl.when(j == 0)
    def _():
        xn_ref[...] = _rms(x_ref[...], nw_ref[...]).astype(BF16)
        o_ref[...] = jnp.zeros_like(o_ref)

    xn = xn_ref[...]
    g = _dot(xn, wg_ref[...])
    u = _dot(xn, wu_ref[...])
    a = (g * jax.nn.sigmoid(g) * u).astype(BF16)
    o_ref[...] += _dot(a, wd_ref[...])

    @pl.when(j == pl.num_programs(1) - 1)
    def _():
        h = x_ref[...] + FFN_RESIDUAL_SCALE * o_ref[...]
        if final:
            h = _rms(h, fw_ref[...])
        o_ref[...] = h


def _ffn(h, norm_w, wg, wu, wd, final_w, *, final, tm=None, tf=512):
    m, d = h.shape
    if tm is None:
        tm = 512 if final else 1024
    ffp = wg.shape[1]
    grid = (m // tm, ffp // tf)
    return pl.pallas_call(
        functools.partial(_ffn_kernel, final=final),
        grid=grid,
        in_specs=[
            pl.BlockSpec((tm, d), lambda i, j: (i, 0)),
            pl.BlockSpec((1, d), lambda i, j: (0, 0)),
            pl.BlockSpec((d, tf), lambda i, j: (0, j)),
            pl.BlockSpec((d, tf), lambda i, j: (0, j)),
            pl.BlockSpec((tf, d), lambda i, j: (j, 0)),
            pl.BlockSpec((1, d), lambda i, j: (0, 0)),
        ],
        out_specs=pl.BlockSpec((tm, d), lambda i, j: (i, 0)),
        out_shape=jax.ShapeDtypeStruct((m, d), F32),
        scratch_shapes=[pltpu.VMEM((tm, d), BF16)],
        compiler_params=pltpu.CompilerParams(
            dimension_semantics=("parallel", "arbitrary"), vmem_limit_bytes=VMEM_LIMIT),
        name="ffn",
    )(h, norm_w, wg, wu, wd, final_w)


def _proj_kernel(x_ref, nw_ref, w_ref, o_ref, xn_ref):
    @pl.when(pl.program_id(1) == 0)
    def _():
        xn_ref[...] = _rms(x_ref[...], nw_ref[...]).astype(BF16)

    o_ref[...] = _dot(xn_ref[...], w_ref[...])


def _proj(h, norm_w, w, *, tm=1024, tn=768):
    m, d = h.shape
    n = w.shape[1]
    return pl.pallas_call(
        _proj_kernel,
        grid=(m // tm, n // tn),
        in_specs=[
            pl.BlockSpec((tm, d), lambda i, j: (i, 0)),
            pl.BlockSpec((1, d), lambda i, j: (0, 0)),
            pl.BlockSpec((d, tn), lambda i, j: (0, j)),
        ],
        out_specs=pl.BlockSpec((tm, tn), lambda i, j: (i, j)),
        out_shape=jax.ShapeDtypeStruct((m, n), F32),
        scratch_shapes=[pltpu.VMEM((tm, d), BF16)],
        compiler_params=pltpu.CompilerParams(
            dimension_semantics=("parallel", "arbitrary"), vmem_limit_bytes=VMEM_LIMIT),
        name="proj",
    )(h, norm_w, w)


def _outproj_kernel(h_ref, yr_ref, yc_ref, yh_ref, wr_ref, wc_ref, wh_ref, o_ref):
    acc = _dot(yr_ref[...], wr_ref[...])
    acc += _dot(yc_ref[...], wc_ref[...])
    acc += _dot(yh_ref[...], wh_ref[...])
    o_ref[...] = h_ref[...] + acc


def _outproj(h, yr, yc, yh, wo, *, tm=512):
    m, d = h.shape
    nr, ncv, nh = yr.shape[1], yc.shape[1], yh.shape[1]
    row = lambda i: (i, 0)
    return pl.pallas_call(
        _outproj_kernel,
        grid=(m // tm,),
        in_specs=[
            pl.BlockSpec((tm, d), row),
            pl.BlockSpec((tm, yr.shape[1]), row),
            pl.BlockSpec((tm, yc.shape[1]), row),
            pl.BlockSpec((tm, yh.shape[1]), row),
            pl.BlockSpec((nr, d), lambda i: (0, 0)),
            pl.BlockSpec((ncv, d), lambda i: (nr // ncv, 0)),
            pl.BlockSpec((nh, d), lambda i: ((nr + ncv) // nh, 0)),
        ],
        out_specs=pl.BlockSpec((tm, d), row),
        out_shape=jax.ShapeDtypeStruct((m, d), F32),
        compiler_params=pltpu.CompilerParams(
            dimension_semantics=("parallel",), vmem_limit_bytes=VMEM_LIMIT),
        name="outproj",
    )(h, yr, yc, yh, wo, wo, wo)


def _prev_rows(x, carry, k=1):
    rows = lax.broadcasted_iota(jnp.int32, (8, x.shape[1]), 0)
    prev = pltpu.roll(x, k, 0)
    head = jnp.where(rows < k, pltpu.roll(carry, k, 0), prev[:8])
    return jnp.concatenate([head, prev[8:]], axis=0)


def _seg_sum(x, ones_bd):
    return _dot(x.astype(BF16), ones_bd)


def _rwkv_prepare(lw, r, k2, v, av, bv, c, tick):
    n = 2 * RWKV_CHUNK
    m0, m1 = c["m0"], c["m1"]
    stack = lambda x: jnp.concatenate([x * m0, x * m1], axis=0)
    each = lambda f, *ls: [f(*xs) for xs in zip(*ls)]

    def cumsum(x):
        hi, lo = _split_bf16(x)
        gcat = _dot(c["tri"], jnp.concatenate([hi, lo], axis=1))
        return gcat[:, :LANES] + gcat[:, LANES:]

    g = each(cumsum, lw)
    tick()
    gl = [x[RWKV_CHUNK - 1:RWKV_CHUNK, :] for x in g]
    e_g = each(jnp.exp, g)
    e_ng = each(lambda x: jnp.exp(-x), g)
    e_d = each(lambda x, y: jnp.exp(x - y), gl, g)
    ats = each(lambda a_, g_, l_: stack(a_ * jnp.exp(g_ - l_)), av, g, lw)
    rts = each(lambda r_, e_: stack(r_ * e_), r, e_g)
    bts = each(lambda b_, e_: stack(b_ * e_), bv, e_ng)
    kts = each(lambda k_, e_: stack(k_ * e_), k2, e_ng)
    bes = each(lambda b_, e_: stack(b_ * e_).astype(BF16), bv, e_d)
    kes = each(lambda k_, e_: stack(k_ * e_).astype(BF16), k2, e_d)
    vs = each(stack, v)
    vsb = [x.astype(BF16) for x in vs]

    p = each(lambda a_, r_, b_, k_: _dot_nt(jnp.concatenate([a_, r_], axis=0).astype(BF16),
                                            jnp.concatenate([b_, k_], axis=0).astype(BF16)),
             ats, rts, bts, kts)
    a_ab = [jnp.where(c["strict"], x[:n, :n], 0.0) for x in p]
    a_kr = [jnp.concatenate([jnp.where(c["strict"], x[:n, n:], 0.0),
                             jnp.where(c["incl"], x[n:, n:], 0.0)], axis=0).astype(BF16) for x in p]
    a_rb = [jnp.where(c["incl"], x[n:, :n], 0.0).astype(BF16) for x in p]
    tick()

    h = RWKV_CHUNK
    side = lambda x: x[:h] + x[h:]
    diag = lambda x: jnp.concatenate([x, x], axis=0) * c["own"]
    a_sbs = [side(x) for x in a_ab]
    inv = [c["eye"] + x for x in a_sbs]
    q = [_dot(x.astype(BF16), y.astype(BF16)) for x, y in zip(a_sbs, a_ab)]
    tick()
    for _ in range(4):
        z = each(lambda i_, q_: _dot(jnp.concatenate([i_, q_], axis=0).astype(BF16), diag(q_).astype(BF16)), inv, q)
        inv = each(lambda i_, z_: i_ + z_[:h], inv, z)
        q = [x[h:] for x in z]
        tick()
    inv = each(lambda i_, q_: diag(i_ + _dot(i_.astype(BF16), diag(q_).astype(BF16))), inv, q)
    tick()

    akv_rkv = each(_dot, a_kr, vsb)
    tick()
    tt = each(lambda i_, a_, x_: _dot(i_.astype(BF16), jnp.concatenate([a_, x_[:n]], axis=1).astype(BF16)),
              inv, ats, akv_rkv)
    tick()
    rr = each(lambda a_, t_: _dot(a_, t_.astype(BF16)), a_rb, tt)
    rq = each(lambda r_, x_: (r_ + x_[:, :LANES]).astype(BF16), rts, rr)
    oc = each(lambda x_, y_: x_[:, LANES:] + y_[n:], rr, akv_rkv)
    bb = each(lambda t_, b_: _dot(t_.T.astype(BF16), b_), tt, bes)
    bta = [x[:LANES].astype(BF16) for x in bb]
    nct = each(lambda x_, v_, k_: x_[LANES:] + _dot(v_.T.astype(BF16), k_), bb, vs, kes)
    e_gl = each(jnp.exp, gl)
    return rq, oc, bta, nct, e_gl


def _rwkv_kernel(r_ref, k_ref, v_ref, low_ref, mur_ref, muk_ref, muv_ref, mul_ref,
                 ww_ref, wa_ref, wg_ref, w0_ref, a0_ref, kk_ref, ka_ref, rk_ref, lnw_ref, lnb_ref,
                 y_ref, cr_ref, ck_ref, cv_ref, cl_ref, st_ref, feat_ref,
                 rq_ref, oc_ref, bta_ref, nct_ref, egl_ref, bonus_ref, gate_ref, *, tb, nt):
    s = pl.program_id(0)
    nc = tb // RWKV_CHUNK

    @pl.when(s == 0)
    def _():
        for ref in (cr_ref, ck_ref, cv_ref, cl_ref, st_ref, feat_ref, rq_ref, oc_ref, bta_ref, nct_ref, egl_ref,
                    bonus_ref, gate_ref):
            ref[...] = jnp.zeros_like(ref)

    lane = lax.broadcasted_iota(jnp.int32, (1, LANES), 1)
    m0 = (lane < RWKV_HEAD_DIM).astype(F32)
    m1 = 1.0 - m0
    ri = lax.broadcasted_iota(jnp.int32, (LANES, LANES), 0)
    ci = lax.broadcasted_iota(jnp.int32, (LANES, LANES), 1)
    same = (ri < RWKV_HEAD_DIM) == (ci < RWKV_HEAD_DIM)
    ones_bd = same.astype(F32).astype(BF16)
    ti = lax.broadcasted_iota(jnp.int32, (RWKV_CHUNK, RWKV_CHUNK), 0)
    si = lax.broadcasted_iota(jnp.int32, (RWKV_CHUNK, RWKV_CHUNK), 1)
    tw = lax.broadcasted_iota(jnp.int32, (RWKV_CHUNK, LANES), 0)
    sw = lax.broadcasted_iota(jnp.int32, (RWKV_CHUNK, LANES), 1) & (RWKV_HEAD_DIM - 1)
    consts = dict(
        m0=m0, m1=m1, own=same.astype(F32),
        strict=same & (ri > ci), incl=same & (ri >= ci),
        eye=(tw == sw).astype(F32),
        tri=(ti >= si).astype(F32).astype(BF16),
    )

    rd = lax.rem(s, 2)
    rd3 = lax.rem(s + 1, 3)
    seq_start = lax.rem(jnp.maximum(s - 2, 0), nt) == 0
    state = [st_ref[...] * jnp.where(seq_start, 0.0, 1.0)]
    outs = []
    inv_n = 1.0 / RWKV_HEAD_DIM
    own = same.astype(F32)

    def recurrence_step():
        i = len(outs)
        if i == nc:
            return
        st = state[0]
        stb = st.astype(BF16)
        o_stack = _dot_nt(rq_ref[rd, i], stb) + oc_ref[rd, i]
        state[0] = st * egl_ref[rd, i][0:1, :] + _dot(stb, bta_ref[rd, i]) + nct_ref[rd, i]
        mean = jnp.sum(o_stack, axis=-1, keepdims=True) * inv_n
        cen = (o_stack - mean) * own
        var = jnp.sum(cen * cen, axis=-1, keepdims=True) * inv_n
        on = cen * lax.rsqrt(var + RWKV_GN_EPS)
        outs.append(on[:RWKV_CHUNK] + on[RWKV_CHUNK:])

    wf = lax.rem(s, 2)
    wf3 = lax.rem(s, 3)
    keep = jnp.where(lax.rem(jnp.minimum(s, pl.num_programs(0) - 3), nt) == 0, 0.0, 1.0)
    f = {}

    def lerp(x_ref, c_ref, mu_ref):
        x = x_ref[0]
        prev = _prev_rows(x, c_ref[...] * keep)
        c_ref[...] = x[tb - 8:tb, :]
        return x + (prev - x) * mu_ref[...]

    def feat_low():
        low = lerp(low_ref, cl_ref, mul_ref)
        f["wa_in"] = low[:, :LANES]
        gate_ref[wf3] = _dot(jax.nn.sigmoid(low[:, LANES:]).astype(BF16), wg_ref[...])

    def feat_decay():
        xw = w0_ref[...] + _dot(jnp.tanh(f["wa_in"]).astype(BF16), ww_ref[...])
        feat_ref[wf, 0] = -jnp.exp(jnp.float32(-0.5)) * jax.nn.sigmoid(xw)
        f["a"] = jax.nn.sigmoid(a0_ref[...] + _dot(f["wa_in"].astype(BF16), wa_ref[...]))

    def feat_k():
        k = lerp(k_ref, ck_ref, muk_ref)
        kk = k * kk_ref[...]
        kk = kk * lax.rsqrt(jnp.maximum(_seg_sum(kk * kk, ones_bd), 1e-24))
        f["k2"] = k * (1.0 + (f["a"] - 1.0) * ka_ref[...])
        feat_ref[wf, 2] = f["k2"]
        feat_ref[wf, 4] = -kk
        feat_ref[wf, 5] = kk * f["a"]

    def feat_rv():
        r = lerp(r_ref, cr_ref, mur_ref)
        v = lerp(v_ref, cv_ref, muv_ref)
        feat_ref[wf, 1] = r
        feat_ref[wf, 3] = v
        bonus_ref[wf3] = _seg_sum(r * f["k2"] * rk_ref[...], ones_bd) * v

    pieces = [feat_low, feat_decay, feat_k, feat_rv]

    def tick():
        if pieces:
            pieces.pop(0)()
        recurrence_step()

    rf = lax.rem(s + 1, 2)
    wr = lax.rem(s + 1, 2)
    chunks = lambda j: [feat_ref[rf, j, i * RWKV_CHUNK:(i + 1) * RWKV_CHUNK, :] for i in range(nc)]
    rq, oc, bta, nct, e_gl = _rwkv_prepare(chunks(0), chunks(1), chunks(2), chunks(3), chunks(4), chunks(5),
                                           consts, tick)
    while pieces or len(outs) < nc:
        tick()

    st_ref[...] = state[0]
    o = jnp.concatenate(outs, axis=0) * lnw_ref[...] + lnb_ref[...]
    y_ref[0] = ((o + bonus_ref[rd3]) * gate_ref[rd3]).astype(y_ref.dtype)

    for i in range(nc):
        rq_ref[wr, i] = rq[i]
        oc_ref[wr, i] = oc[i]
        bta_ref[wr, i] = bta[i]
        nct_ref[wr, i] = nct[i]
        egl_ref[wr, i] = jnp.broadcast_to(e_gl[i], (8, LANES))


def _rwkv(proj3, mu, ww, wa, wg, w0, a0, k_k, k_a, r_k, ln_w, ln_b, *, tb=512):
    b, t, _ = proj3.shape
    npairs = RWKV_WIDTH // LANES
    nt = t // tb
    nc = tb // RWKV_CHUNK
    nblocks = b * npairs * nt
    off_k = RWKV_WIDTH // LANES
    off_v = 2 * RWKV_WIDTH // LANES
    off_low = 3 * RWKV_WIDTH // RWKV_LOW

    def where(item):
        return item // (nt * npairs), item % nt, (item // nt) % npairs

    cur = lambda s: where(jnp.minimum(s, nblocks - 1))
    lag = lambda s: where(jnp.maximum(s - 2, 0))
    act = lambda off: pl.BlockSpec((1, tb, LANES), lambda s: (cur(s)[0], cur(s)[1], off + cur(s)[2]))
    vec = lambda off: pl.BlockSpec((1, LANES), lambda s: (0, off + cur(s)[2]))
    lagvec = pl.BlockSpec((1, LANES), lambda s: (0, lag(s)[2]))
    wspec = pl.BlockSpec((LANES, LANES), lambda s: (0, cur(s)[2]))
    return pl.pallas_call(
        functools.partial(_rwkv_kernel, tb=tb, nt=nt),
        grid=(nblocks + 2,),
        in_specs=[
            act(0), act(off_k), act(off_v),
            pl.BlockSpec((1, tb, RWKV_LOW), lambda s: (cur(s)[0], cur(s)[1], off_low)),
            vec(0), vec(off_k), vec(off_v),
            pl.BlockSpec((1, RWKV_LOW), lambda s: (0, off_low)),
            wspec, wspec, wspec,
            vec(0), vec(0), vec(0), vec(0), vec(0), lagvec, lagvec,
        ],
        out_specs=pl.BlockSpec((1, tb, LANES), lambda s: lag(s)),
        out_shape=jax.ShapeDtypeStruct((b, t, RWKV_WIDTH), BF16),
        scratch_shapes=[pltpu.VMEM((8, LANES), F32), pltpu.VMEM((8, LANES), F32), pltpu.VMEM((8, LANES), F32),
                        pltpu.VMEM((8, RWKV_LOW), F32), pltpu.VMEM((LANES, LANES), F32),
                        pltpu.VMEM((2, 6, tb, LANES), F32),
                        pltpu.VMEM((2, nc, LANES, LANES), BF16), pltpu.VMEM((2, nc, LANES, LANES), F32),
                        pltpu.VMEM((2, nc, LANES, LANES), BF16), pltpu.VMEM((2, nc, LANES, LANES), F32),
                        pltpu.VMEM((2, nc, 8, LANES), F32),
                        pltpu.VMEM((3, tb, LANES), F32), pltpu.VMEM((3, tb, LANES), F32)],
        compiler_params=pltpu.CompilerParams(
            dimension_semantics=("arbitrary",), vmem_limit_bytes=VMEM_LIMIT),
        name="rwkv7",
    )(proj3, proj3, proj3, proj3, mu, mu, mu, mu, ww, wa, wg, w0, a0, k_k, k_a, r_k, ln_w, ln_b)


def _conv_kernel(c_ref, x_ref, b_ref, w_ref, y_ref, carry_ref):
    @pl.when(pl.program_id(2) == 0)
    def _():
        carry_ref[...] = jnp.zeros_like(carry_ref)

    z = c_ref[0] * x_ref[0]
    tb = z.shape[0]
    w = w_ref[...]
    y = z * w[CONV_K - 1:CONV_K, :]
    for j in range(1, CONV_K):
        y = y + _prev_rows(z, carry_ref[...], j) * w[CONV_K - 1 - j:CONV_K - j, :]
    carry_ref[...] = z[tb - 8:tb, :]
    y_ref[0] = (b_ref[0] * y).astype(y_ref.dtype)


def _conv(proj3, conv_w, *, tb=1024, cw=2 * LANES):
    b, t, _ = proj3.shape
    tb = min(tb, t)
    nblk = CONV_WIDTH // cw
    off = RWKV_SHIFT_WIDTH // cw
    spec = lambda o: pl.BlockSpec((1, tb, cw), lambda bi, p, ti: (bi, ti, o + p))
    return pl.pallas_call(
        _conv_kernel,
        grid=(b, nblk, t // tb),
        in_specs=[spec(off), spec(off + nblk), spec(off + 2 * nblk),
                  pl.BlockSpec((CONV_K, cw), lambda bi, p, ti: (0, p))],
        out_specs=pl.BlockSpec((1, tb, cw), lambda bi, p, ti: (bi, ti, p)),
        out_shape=jax.ShapeDtypeStruct((b, t, CONV_WIDTH), BF16),
        scratch_shapes=[pltpu.VMEM((8, cw), F32)],
        compiler_params=pltpu.CompilerParams(
            dimension_semantics=("parallel", "parallel", "arbitrary"), vmem_limit_bytes=VMEM_LIMIT),
        name="shortconv",
    )(proj3, proj3, proj3, conv_w)


def _hgrn_kernel(q_ref, f_ref, i_ref, og_ref, lb_ref, nw_ref, y_ref, st_ref, *, layer, tb):
    @pl.when(pl.program_id(2) == 0)
    def _():
        st_ref[...] = jnp.zeros_like(st_ref)

    lbp = lb_ref[...]
    e = jnp.exp(lbp - jnp.max(lbp, axis=0, keepdims=True))
    pr = e / jnp.sum(e, axis=0, keepdims=True)
    lb = jnp.zeros((1, LANES), F32)
    for l in range(1, layer + 1):
        lb = lb + pr[l:l + 1, :]

    qr = q_ref[0]
    q = qr * jax.nn.sigmoid(qr)
    f = f_ref[0]
    forget = lb + (1.0 - lb) * jax.nn.sigmoid(f)
    logf = jnp.log(jnp.maximum(forget, HGRN_MIN_FORGET))
    kx = (1.0 - lb) * jax.nn.sigmoid(-f)
    iv = i_ref[0]

    n = HGRN_BLOCK
    nb = tb // n
    ti = lax.broadcasted_iota(jnp.int32, (tb, tb), 0)
    si = lax.broadcasted_iota(jnp.int32, (tb, tb), 1)
    same = (ti ^ si) < n
    sums = jnp.concatenate([(same & (ti >= si)).astype(F32).astype(BF16), same.astype(F32).astype(BF16)], axis=0)
    hi, lo = _split_bf16(logf)
    gg = _dot(sums, jnp.concatenate([hi, lo], axis=1))
    g = gg[:tb, :LANES] + gg[:tb, LANES:]
    gl = gg[tb:, :LANES] + gg[tb:, LANES:]
    qd = (q * jnp.exp(g)).astype(BF16)
    kd = (kx * jnp.exp(gl - g)).astype(BF16)
    dec = jnp.exp(gl)
    trow = lax.broadcasted_iota(jnp.int32, (n, 1), 0)
    blocks = lambda x: [x[i * n:(i + 1) * n] for i in range(nb)]
    gb, qb, kb, ib, qdb, kdb = (blocks(x) for x in (g, q, kx, iv, qd, kd))

    upd = [_dot(i_.T.astype(BF16), k_) for i_, k_ in zip(ib, kdb)]
    st = st_ref[...]
    sts = []
    for i in range(nb):
        sts.append(st.astype(BF16))
        st = st * dec[i * n:i * n + 1, :] + upd[i]
    st_ref[...] = st
    outs = [_dot_nt(q_, s_) for q_, s_ in zip(qdb, sts)]
    for s in range(n):
        for i in range(nb):
            w = jnp.sum(qb[i] * kb[i][s:s + 1, :] * jnp.exp(gb[i] - gb[i][s:s + 1, :]), axis=-1, keepdims=True)
            outs[i] = outs[i] + jnp.where(trow >= s, w, 0.0) * ib[i][s:s + 1, :]
    o = jnp.concatenate(outs, axis=0)
    o = o * lax.rsqrt(jnp.mean(o * o, axis=-1, keepdims=True) + HGRN_NORM_EPS) * nw_ref[...]
    og = og_ref[0]
    y_ref[0] = (o * (og * jax.nn.sigmoid(og))).astype(y_ref.dtype)


def _hgrn(proj3, hgrn_lb, norm_w, *, layer, tb=256):
    b, t, _ = proj3.shape
    nh = HGRN_WIDTH // HGRN_HEAD_DIM
    off = (RWKV_SHIFT_WIDTH + 3 * CONV_WIDTH) // LANES
    spec = lambda o: pl.BlockSpec((1, tb, LANES), lambda bi, p, ti: (bi, ti, o + p))
    depth = hgrn_lb.shape[0]
    return pl.pallas_call(
        functools.partial(_hgrn_kernel, layer=layer, tb=tb),
        grid=(b, nh, t // tb),
        in_specs=[spec(off), spec(off + nh), spec(off + 2 * nh), spec(off + 3 * nh),
                  pl.BlockSpec((depth, LANES), lambda bi, p, ti: (0, p)),
                  pl.BlockSpec((1, LANES), lambda bi, p, ti: (0, p))],
        out_specs=pl.BlockSpec((1, tb, LANES), lambda bi, p, ti: (bi, ti, p)),
        out_shape=jax.ShapeDtypeStruct((b, t, HGRN_WIDTH), BF16),
        scratch_shapes=[pltpu.VMEM((LANES, LANES), F32)],
        compiler_params=pltpu.CompilerParams(
            dimension_semantics=("parallel", "parallel", "arbitrary"), vmem_limit_bytes=VMEM_LIMIT),
        name="hgrn2",
    )(proj3, proj3, proj3, proj3, hgrn_lb, norm_w)


def kernel(x, ffn1_norm, ffn1_w_gu, ffn1_w_down, mix_norm, w_in, rwkv_mu, rwkv_w_up, rwkv_w0,
           rwkv_a_up, rwkv_a0, rwkv_g_up, rwkv_k_k, rwkv_k_a, rwkv_r_k, rwkv_ln_w, rwkv_ln_b,
           conv_w, hgrn_lb, hgrn_norm, w_out, ffn2_norm, ffn2_w_gu, ffn2_w_down, final_norm):
    b, t, d = x.shape
    depth = ffn1_norm.shape[0]
    d_ff = ffn1_w_down.shape[1]
    ffp = -(-d_ff // 512) * 512
    m = b * t
    h = x.reshape(m, d)
    row = lambda v: v.reshape(1, -1)
    final_w = row(final_norm)

    def ffn_weights(w_gu, w_down):
        padc = ((0, 0), (0, 0), (0, ffp - d_ff))
        wg = jnp.pad(w_gu[:, :, :d_ff].astype(BF16), padc)
        wu = jnp.pad(w_gu[:, :, d_ff:].astype(BF16), padc)
        wd = jnp.pad(w_down.astype(BF16), ((0, 0), (0, ffp - d_ff), (0, 0)))
        return wg, wu, wd

    wg1, wu1, wd1 = ffn_weights(ffn1_w_gu, ffn1_w_down)
    wg2, wu2, wd2 = ffn_weights(ffn2_w_gu, ffn2_w_down)
    w_in_b = w_in.astype(BF16)
    w_out_b = w_out.astype(BF16)
    zeros_low = jnp.zeros((depth, LANES - DECAY_RANK, RWKV_WIDTH), BF16)
    ww = jnp.concatenate([rwkv_w_up.astype(BF16), zeros_low], axis=1)
    wa = jnp.concatenate([zeros_low, rwkv_a_up.astype(BF16)], axis=1)
    wgate = rwkv_g_up.astype(BF16)

    for l in range(depth):
        h = _ffn(h, row(ffn1_norm[l]), wg1[l], wu1[l], wd1[l], final_w, final=False)

        proj = _proj(h, row(mix_norm[l]), w_in_b[l])
        proj3 = proj.reshape(b, t, -1)
        y_r = _rwkv(proj3, row(rwkv_mu[l]), ww[l], wa[l], wgate[l],
                    row(rwkv_w0[l]), row(rwkv_a0[l]), row(rwkv_k_k[l]), row(rwkv_k_a[l]),
                    row(rwkv_r_k[l]), row(rwkv_ln_w[l]), row(rwkv_ln_b[l]))
        y_c = _conv(proj3, conv_w[l])
        y_h = _hgrn(proj3, hgrn_lb, row(hgrn_norm[l]), layer=l)
        h = _outproj(h, y_r.reshape(m, -1), y_c.reshape(m, -1), y_h.reshape(m, -1), w_out_b[l])

        h = _ffn(h, row(ffn2_norm[l]), wg2[l], wu2[l], wd2[l], final_w, final=(l == depth - 1))
    return h.reshape(b, t, d)
```

```python
import functools

import jax
import jax.numpy as jnp
from jax import lax
from jax.experimental import pallas as pl
from jax.experimental.pallas import tpu as pltpu

F32 = jnp.float32
BF16 = jnp.bfloat16

RWKV_HEAD_DIM = 64
RWKV_WIDTH = 1024
DECAY_RANK = 64
AAA_RANK = 64
GATE_RANK = 128
RWKV_LOW = DECAY_RANK + AAA_RANK + GATE_RANK
RWKV_SHIFT_WIDTH = 3 * RWKV_WIDTH + RWKV_LOW
RWKV_GN_EPS = 64e-5
CONV_WIDTH = 512
CONV_K = 3
HGRN_HEAD_DIM = 128
HGRN_WIDTH = 512
HGRN_NORM_EPS = 1e-5
HGRN_MIN_FORGET = 1e-30
FFN_RESIDUAL_SCALE = 0.5
RMS_EPS = 1e-6

LANES = 128
RWKV_CHUNK = 64
HGRN_BLOCK = 16
VMEM_LIMIT = 56 * 1024 * 1024


def _dot(a, b):
    return jnp.dot(a, b, preferred_element_type=F32)


def _dot_nt(a, b):
    return lax.dot_general(a, b, (((1,), (1,)), ((), ())), preferred_element_type=F32)


def _rms(x, w):
    return x * lax.rsqrt(jnp.mean(x * x, axis=-1, keepdims=True) + RMS_EPS) * w


def _split_bf16(x):
    hi = x.astype(BF16)
    lo = (x - hi.astype(F32)).astype(BF16)
    return hi, lo


def _ffn_kernel(x_ref, nw_ref, wg_ref, wu_ref, wd_ref, fw_ref, o_ref, xn_ref, *, final):
    j = pl.program_id(1)

    @pl.when(j == 0)
    def _():
        xn_ref[...] = _rms(x_ref[...], nw_ref[...]).astype(BF16)
        o_ref[...] = jnp.zeros_like(o_ref)

    xn = xn_ref[...]
    g = _dot(xn, wg_ref[...])
    u = _dot(xn, wu_ref[...])
    a = (g * jax.nn.sigmoid(g) * u).astype(BF16)
    o_ref[...] += _dot(a, wd_ref[...])

    @pl.when(j == pl.num_programs(1) - 1)
    def _():
        h = x_ref[...] + FFN_RESIDUAL_SCALE * o_ref[...]
        if final:
            h = _rms(h, fw_ref[...])
        o_ref[...] = h


def _ffn(h, norm_w, wg, wu, wd, final_w, *, final, tm=None, tf=512):
    m, d = h.shape
    if tm is None:
        tm = 512 if final else 1024
    ffp = wg.shape[1]
    grid = (m // tm, ffp // tf)
    return pl.pallas_call(
        functools.partial(_ffn_kernel, final=final),
        grid=grid,
        in_specs=[
            pl.BlockSpec((tm, d), lambda i, j: (i, 0)),
            pl.BlockSpec((1, d), lambda i, j: (0, 0)),
            pl.BlockSpec((d, tf), lambda i, j: (0, j)),
            pl.BlockSpec((d, tf), lambda i, j: (0, j)),
            pl.BlockSpec((tf, d), lambda i, j: (j, 0)),
            pl.BlockSpec((1, d), lambda i, j: (0, 0)),
        ],
        out_specs=pl.BlockSpec((tm, d), lambda i, j: (i, 0)),
        out_shape=jax.ShapeDtypeStruct((m, d), F32),
        scratch_shapes=[pltpu.VMEM((tm, d), BF16)],
        compiler_params=pltpu.CompilerParams(
            dimension_semantics=("parallel", "arbitrary"), vmem_limit_bytes=VMEM_LIMIT),
        name="ffn",
    )(h, norm_w, wg, wu, wd, final_w)


def _proj_kernel(x_ref, nw_ref, w_ref, o_ref, xn_ref):
    @pl.when(pl.program_id(1) == 0)
    def _():
        xn_ref[...] = _rms(x_ref[...], nw_ref[...]).astype(BF16)

    o_ref[...] = _dot(xn_ref[...], w_ref[...]).astype(o_ref.dtype)


def _proj(h, norm_w, w, *, tm=1024, tn=768):
    m, d = h.shape
    n = w.shape[1]
    return pl.pallas_call(
        _proj_kernel,
        grid=(m // tm, n // tn),
        in_specs=[
            pl.BlockSpec((tm, d), lambda i, j: (i, 0)),
            pl.BlockSpec((1, d), lambda i, j: (0, 0)),
            pl.BlockSpec((d, tn), lambda i, j: (0, j)),
        ],
        out_specs=pl.BlockSpec((tm, tn), lambda i, j: (i, j)),
        out_shape=jax.ShapeDtypeStruct((m, n), BF16),
        scratch_shapes=[pltpu.VMEM((tm, d), BF16)],
        compiler_params=pltpu.CompilerParams(
            dimension_semantics=("parallel", "arbitrary"), vmem_limit_bytes=VMEM_LIMIT),
        name="proj",
    )(h, norm_w, w)


def _outproj_kernel(h_ref, yr_ref, yc_ref, yh_ref, wr_ref, wc_ref, wh_ref, o_ref):
    acc = _dot(yr_ref[...], wr_ref[...])
    acc += _dot(yc_ref[...], wc_ref[...])
    acc += _dot(yh_ref[...], wh_ref[...])
    o_ref[...] = h_ref[...] + acc


def _outproj(h, yr, yc, yh, wo, *, tm=512):
    m, d = h.shape
    nr, ncv, nh = yr.shape[1], yc.shape[1], yh.shape[1]
    row = lambda i: (i, 0)
    return pl.pallas_call(
        _outproj_kernel,
        grid=(m // tm,),
        in_specs=[
            pl.BlockSpec((tm, d), row),
            pl.BlockSpec((tm, yr.shape[1]), row),
            pl.BlockSpec((tm, yc.shape[1]), row),
            pl.BlockSpec((tm, yh.shape[1]), row),
            pl.BlockSpec((nr, d), lambda i: (0, 0)),
            pl.BlockSpec((ncv, d), lambda i: (nr // ncv, 0)),
            pl.BlockSpec((nh, d), lambda i: ((nr + ncv) // nh, 0)),
        ],
        out_specs=pl.BlockSpec((tm, d), row),
        out_shape=jax.ShapeDtypeStruct((m, d), F32),
        compiler_params=pltpu.CompilerParams(
            dimension_semantics=("parallel",), vmem_limit_bytes=VMEM_LIMIT),
        name="outproj",
    )(h, yr, yc, yh, wo, wo, wo)


def _prev_rows(x, carry, k=1):
    rows = lax.broadcasted_iota(jnp.int32, (8, x.shape[1]), 0)
    prev = pltpu.roll(x, k, 0)
    head = jnp.where(rows < k, pltpu.roll(carry, k, 0), prev[:8])
    return jnp.concatenate([head, prev[8:]], axis=0)


def _seg_sum(x, ones_bd):
    return _dot(x.astype(BF16), ones_bd)


def _rwkv_prepare(lw, r, k2, v, av, bv, c, tick):
    n = 2 * RWKV_CHUNK
    m0, m1 = c["m0"], c["m1"]
    stack = lambda x: jnp.concatenate([x * m0, x * m1], axis=0)
    each = lambda f, *ls: [f(*xs) for xs in zip(*ls)]

    def cumsum(x):
        hi, lo = _split_bf16(x)
        gcat = _dot(c["tri"], jnp.concatenate([hi, lo], axis=1))
        return gcat[:, :LANES] + gcat[:, LANES:]

    g = each(cumsum, lw)
    tick()
    gl = [x[RWKV_CHUNK - 1:RWKV_CHUNK, :] for x in g]
    e_g = each(jnp.exp, g)
    e_ng = each(lambda x: jnp.exp(-x), g)
    e_d = each(lambda x, y: jnp.exp(x - y), gl, g)
    ats = each(lambda a_, g_, l_: stack(a_ * jnp.exp(g_ - l_)), av, g, lw)
    rts = each(lambda r_, e_: stack(r_ * e_), r, e_g)
    bts = each(lambda b_, e_: stack(b_ * e_), bv, e_ng)
    kts = each(lambda k_, e_: stack(k_ * e_), k2, e_ng)
    bes = each(lambda b_, e_: stack(b_ * e_).astype(BF16), bv, e_d)
    kes = each(lambda k_, e_: stack(k_ * e_).astype(BF16), k2, e_d)
    vs = each(stack, v)
    vsb = [x.astype(BF16) for x in vs]

    p = each(lambda a_, r_, b_, k_: _dot_nt(jnp.concatenate([a_, r_], axis=0).astype(BF16),
                                            jnp.concatenate([b_, k_], axis=0).astype(BF16)),
             ats, rts, bts, kts)
    a_ab = [jnp.where(c["strict"], x[:n, :n], 0.0) for x in p]
    a_kr = [jnp.concatenate([jnp.where(c["strict"], x[:n, n:], 0.0),
                             jnp.where(c["incl"], x[n:, n:], 0.0)], axis=0).astype(BF16) for x in p]
    a_rb = [jnp.where(c["incl"], x[n:, :n], 0.0).astype(BF16) for x in p]
    tick()

    h = RWKV_CHUNK
    side = lambda x: x[:h] + x[h:]
    diag = lambda x: jnp.concatenate([x, x], axis=0) * c["own"]
    a_sbs = [side(x) for x in a_ab]
    inv = [c["eye"] + x for x in a_sbs]
    q = [_dot(x.astype(BF16), y.astype(BF16)) for x, y in zip(a_sbs, a_ab)]
    tick()
    for _ in range(4):
        z = each(lambda i_, q_: _dot(jnp.concatenate([i_, q_], axis=0).astype(BF16), diag(q_).astype(BF16)), inv, q)
        inv = each(lambda i_, z_: i_ + z_[:h], inv, z)
        q = [x[h:] for x in z]
        tick()
    inv = each(lambda i_, q_: diag(i_ + _dot(i_.astype(BF16), diag(q_).astype(BF16))), inv, q)
    tick()

    akv_rkv = each(_dot, a_kr, vsb)
    tick()
    tt = each(lambda i_, a_, x_: _dot(i_.astype(BF16), jnp.concatenate([a_, x_[:n]], axis=1).astype(BF16)),
              inv, ats, akv_rkv)
    tick()
    rr = each(lambda a_, t_: _dot(a_, t_.astype(BF16)), a_rb, tt)
    rq = each(lambda r_, x_: (r_ + x_[:, :LANES]).astype(BF16), rts, rr)
    oc = each(lambda x_, y_: x_[:, LANES:] + y_[n:], rr, akv_rkv)
    bb = each(lambda t_, b_: _dot(t_.T.astype(BF16), b_), tt, bes)
    bta = [x[:LANES].astype(BF16) for x in bb]
    nct = each(lambda x_, v_, k_: x_[LANES:] + _dot(v_.T.astype(BF16), k_), bb, vs, kes)
    e_gl = each(jnp.exp, gl)
    return rq, oc, bta, nct, e_gl


def _rwkv_kernel(r_ref, k_ref, v_ref, low_ref, mur_ref, muk_ref, muv_ref, mul_ref,
                 ww_ref, wa_ref, wg_ref, w0_ref, a0_ref, kk_ref, ka_ref, rk_ref, lnw_ref, lnb_ref,
                 y_ref, cr_ref, ck_ref, cv_ref, cl_ref, st_ref, feat_ref,
                 rq_ref, oc_ref, bta_ref, nct_ref, egl_ref, bonus_ref, gate_ref, *, tb, nt):
    s = pl.program_id(0)
    nc = tb // RWKV_CHUNK

    @pl.when(s == 0)
    def _():
        for ref in (cr_ref, ck_ref, cv_ref, cl_ref, st_ref, feat_ref, rq_ref, oc_ref, bta_ref, nct_ref, egl_ref,
                    bonus_ref, gate_ref):
            ref[...] = jnp.zeros_like(ref)

    lane = lax.broadcasted_iota(jnp.int32, (1, LANES), 1)
    m0 = (lane < RWKV_HEAD_DIM).astype(F32)
    m1 = 1.0 - m0
    ri = lax.broadcasted_iota(jnp.int32, (LANES, LANES), 0)
    ci = lax.broadcasted_iota(jnp.int32, (LANES, LANES), 1)
    same = (ri < RWKV_HEAD_DIM) == (ci < RWKV_HEAD_DIM)
    ones_bd = same.astype(F32).astype(BF16)
    ti = lax.broadcasted_iota(jnp.int32, (RWKV_CHUNK, RWKV_CHUNK), 0)
    si = lax.broadcasted_iota(jnp.int32, (RWKV_CHUNK, RWKV_CHUNK), 1)
    tw = lax.broadcasted_iota(jnp.int32, (RWKV_CHUNK, LANES), 0)
    sw = lax.broadcasted_iota(jnp.int32, (RWKV_CHUNK, LANES), 1) & (RWKV_HEAD_DIM - 1)
    consts = dict(
        m0=m0, m1=m1, own=same.astype(F32),
        strict=same & (ri > ci), incl=same & (ri >= ci),
        eye=(tw == sw).astype(F32),
        tri=(ti >= si).astype(F32).astype(BF16),
    )

    rd = lax.rem(s, 2)
    rd3 = lax.rem(s + 1, 3)
    seq_start = lax.rem(jnp.maximum(s - 2, 0), nt) == 0
    state = [st_ref[...] * jnp.where(seq_start, 0.0, 1.0)]
    outs = []
    inv_n = 1.0 / RWKV_HEAD_DIM
    own = same.astype(F32)

    def recurrence_step():
        i = len(outs)
        if i == nc:
            return
        st = state[0]
        stb = st.astype(BF16)
        o_stack = _dot_nt(rq_ref[rd, i], stb) + oc_ref[rd, i]
        state[0] = st * egl_ref[rd, i][0:1, :] + _dot(stb, bta_ref[rd, i]) + nct_ref[rd, i]
        mean = jnp.sum(o_stack, axis=-1, keepdims=True) * inv_n
        cen = (o_stack - mean) * own
        var = jnp.sum(cen * cen, axis=-1, keepdims=True) * inv_n
        on = cen * lax.rsqrt(var + RWKV_GN_EPS)
        outs.append(on[:RWKV_CHUNK] + on[RWKV_CHUNK:])

    wf = lax.rem(s, 2)
    wf3 = lax.rem(s, 3)
    keep = jnp.where(lax.rem(jnp.minimum(s, pl.num_programs(0) - 3), nt) == 0, 0.0, 1.0)
    f = {}

    def lerp(x_ref, c_ref, mu_ref):
        x = x_ref[0].astype(F32)
        prev = _prev_rows(x, c_ref[...] * keep)
        c_ref[...] = x[tb - 8:tb, :]
        return x + (prev - x) * mu_ref[...]

    def feat_low():
        low = lerp(low_ref, cl_ref, mul_ref)
        f["wa_in"] = low[:, :LANES]
        gate_ref[wf3] = _dot(jax.nn.sigmoid(low[:, LANES:]).astype(BF16), wg_ref[...])

    def feat_decay():
        xw = w0_ref[...] + _dot(jnp.tanh(f["wa_in"]).astype(BF16), ww_ref[...])
        feat_ref[wf, 0] = -jnp.exp(jnp.float32(-0.5)) * jax.nn.sigmoid(xw)
        f["a"] = jax.nn.sigmoid(a0_ref[...] + _dot(f["wa_in"].astype(BF16), wa_ref[...]))

    def feat_k():
        k = lerp(k_ref, ck_ref, muk_ref)
        kk = k * kk_ref[...]
        kk = kk * lax.rsqrt(jnp.maximum(_seg_sum(kk * kk, ones_bd), 1e-24))
        f["k2"] = k * (1.0 + (f["a"] - 1.0) * ka_ref[...])
        feat_ref[wf, 2] = f["k2"]
        feat_ref[wf, 4] = -kk
        feat_ref[wf, 5] = kk * f["a"]

    def feat_rv():
        r = lerp(r_ref, cr_ref, mur_ref)
        v = lerp(v_ref, cv_ref, muv_ref)
        feat_ref[wf, 1] = r
        feat_ref[wf, 3] = v
        bonus_ref[wf3] = _seg_sum(r * f["k2"] * rk_ref[...], ones_bd) * v

    pieces = [feat_low, feat_decay, feat_k, feat_rv]

    def tick():
        if pieces:
            pieces.pop(0)()
        recurrence_step()

    rf = lax.rem(s + 1, 2)
    wr = lax.rem(s + 1, 2)
    chunks = lambda j: [feat_ref[rf, j, i * RWKV_CHUNK:(i + 1) * RWKV_CHUNK, :] for i in range(nc)]
    rq, oc, bta, nct, e_gl = _rwkv_prepare(chunks(0), chunks(1), chunks(2), chunks(3), chunks(4), chunks(5),
                                           consts, tick)
    while pieces or len(outs) < nc:
        tick()

    st_ref[...] = state[0]
    o = jnp.concatenate(outs, axis=0) * lnw_ref[...] + lnb_ref[...]
    y_ref[0] = ((o + bonus_ref[rd3]) * gate_ref[rd3]).astype(y_ref.dtype)

    for i in range(nc):
        rq_ref[wr, i] = rq[i]
        oc_ref[wr, i] = oc[i]
        bta_ref[wr, i] = bta[i]
        nct_ref[wr, i] = nct[i]
        egl_ref[wr, i] = jnp.broadcast_to(e_gl[i], (8, LANES))


def _rwkv(proj3, mu, ww, wa, wg, w0, a0, k_k, k_a, r_k, ln_w, ln_b, *, tb=512):
    b, t, _ = proj3.shape
    npairs = RWKV_WIDTH // LANES
    nt = t // tb
    nc = tb // RWKV_CHUNK
    nblocks = b * npairs * nt
    off_k = RWKV_WIDTH // LANES
    off_v = 2 * RWKV_WIDTH // LANES
    off_low = 3 * RWKV_WIDTH // RWKV_LOW

    def where(item):
        return item // (nt * npairs), item % nt, (item // nt) % npairs

    cur = lambda s: where(jnp.minimum(s, nblocks - 1))
    lag = lambda s: where(jnp.maximum(s - 2, 0))
    act = lambda off: pl.BlockSpec((1, tb, LANES), lambda s: (cur(s)[0], cur(s)[1], off + cur(s)[2]))
    vec = lambda off: pl.BlockSpec((1, LANES), lambda s: (0, off + cur(s)[2]))
    lagvec = pl.BlockSpec((1, LANES), lambda s: (0, lag(s)[2]))
    wspec = pl.BlockSpec((LANES, LANES), lambda s: (0, cur(s)[2]))
    return pl.pallas_call(
        functools.partial(_rwkv_kernel, tb=tb, nt=nt),
        grid=(nblocks + 2,),
        in_specs=[
            act(0), act(off_k), act(off_v),
            pl.BlockSpec((1, tb, RWKV_LOW), lambda s: (cur(s)[0], cur(s)[1], off_low)),
            vec(0), vec(off_k), vec(off_v),
            pl.BlockSpec((1, RWKV_LOW), lambda s: (0, off_low)),
            wspec, wspec, wspec,
            vec(0), vec(0), vec(0), vec(0), vec(0), lagvec, lagvec,
        ],
        out_specs=pl.BlockSpec((1, tb, LANES), lambda s: lag(s)),
        out_shape=jax.ShapeDtypeStruct((b, t, RWKV_WIDTH), BF16),
        scratch_shapes=[pltpu.VMEM((8, LANES), F32), pltpu.VMEM((8, LANES), F32), pltpu.VMEM((8, LANES), F32),
                        pltpu.VMEM((8, RWKV_LOW), F32), pltpu.VMEM((LANES, LANES), F32),
                        pltpu.VMEM((2, 6, tb, LANES), F32),
                        pltpu.VMEM((2, nc, LANES, LANES), BF16), pltpu.VMEM((2, nc, LANES, LANES), F32),
                        pltpu.VMEM((2, nc, LANES, LANES), BF16), pltpu.VMEM((2, nc, LANES, LANES), F32),
                        pltpu.VMEM((2, nc, 8, LANES), F32),
                        pltpu.VMEM((3, tb, LANES), F32), pltpu.VMEM((3, tb, LANES), F32)],
        compiler_params=pltpu.CompilerParams(
            dimension_semantics=("arbitrary",), vmem_limit_bytes=VMEM_LIMIT),
        name="rwkv7",
    )(proj3, proj3, proj3, proj3, mu, mu, mu, mu, ww, wa, wg, w0, a0, k_k, k_a, r_k, ln_w, ln_b)


def _conv_kernel(c_ref, x_ref, b_ref, w_ref, y_ref, carry_ref):
    @pl.when(pl.program_id(2) == 0)
    def _():
        carry_ref[...] = jnp.zeros_like(carry_ref)

    z = c_ref[0].astype(F32) * x_ref[0].astype(F32)
    tb = z.shape[0]
    w = w_ref[...]
    y = z * w[CONV_K - 1:CONV_K, :]
    for j in range(1, CONV_K):
        y = y + _prev_rows(z, carry_ref[...], j) * w[CONV_K - 1 - j:CONV_K - j, :]
    carry_ref[...] = z[tb - 8:tb, :]
    y_ref[0] = (b_ref[0].astype(F32) * y).astype(y_ref.dtype)


def _conv(proj3, conv_w, *, tb=1024, cw=2 * LANES):
    b, t, _ = proj3.shape
    tb = min(tb, t)
    nblk = CONV_WIDTH // cw
    off = RWKV_SHIFT_WIDTH // cw
    spec = lambda o: pl.BlockSpec((1, tb, cw), lambda bi, p, ti: (bi, ti, o + p))
    return pl.pallas_call(
        _conv_kernel,
        grid=(b, nblk, t // tb),
        in_specs=[spec(off), spec(off + nblk), spec(off + 2 * nblk),
                  pl.BlockSpec((CONV_K, cw), lambda bi, p, ti: (0, p))],
        out_specs=pl.BlockSpec((1, tb, cw), lambda bi, p, ti: (bi, ti, p)),
        out_shape=jax.ShapeDtypeStruct((b, t, CONV_WIDTH), BF16),
        scratch_shapes=[pltpu.VMEM((8, cw), F32)],
        compiler_params=pltpu.CompilerParams(
            dimension_semantics=("parallel", "parallel", "arbitrary"), vmem_limit_bytes=VMEM_LIMIT),
        name="shortconv",
    )(proj3, proj3, proj3, conv_w)


def _hgrn_kernel(q_ref, f_ref, i_ref, og_ref, lb_ref, nw_ref, y_ref, st_ref, *, layer, tb):
    @pl.when(pl.program_id(2) == 0)
    def _():
        st_ref[...] = jnp.zeros_like(st_ref)

    lbp = lb_ref[...]
    e = jnp.exp(lbp - jnp.max(lbp, axis=0, keepdims=True))
    pr = e / jnp.sum(e, axis=0, keepdims=True)
    lb = jnp.zeros((1, LANES), F32)
    for l in range(1, layer + 1):
        lb = lb + pr[l:l + 1, :]

    qr = q_ref[0].astype(F32)
    q = qr * jax.nn.sigmoid(qr)
    f = f_ref[0].astype(F32)
    forget = lb + (1.0 - lb) * jax.nn.sigmoid(f)
    logf = jnp.log(jnp.maximum(forget, HGRN_MIN_FORGET))
    kx = (1.0 - lb) * jax.nn.sigmoid(-f)
    iv = i_ref[0].astype(F32)

    n = HGRN_BLOCK
    nb = tb // n
    ti = lax.broadcasted_iota(jnp.int32, (tb, tb), 0)
    si = lax.broadcasted_iota(jnp.int32, (tb, tb), 1)
    same = (ti ^ si) < n
    sums = jnp.concatenate([(same & (ti >= si)).astype(F32).astype(BF16), same.astype(F32).astype(BF16)], axis=0)
    hi, lo = _split_bf16(logf)
    gg = _dot(sums, jnp.concatenate([hi, lo], axis=1))
    g = gg[:tb, :LANES] + gg[:tb, LANES:]
    gl = gg[tb:, :LANES] + gg[tb:, LANES:]
    qd = (q * jnp.exp(g)).astype(BF16)
    kd = (kx * jnp.exp(gl - g)).astype(BF16)
    dec = jnp.exp(gl)
    trow = lax.broadcasted_iota(jnp.int32, (n, 1), 0)
    blocks = lambda x: [x[i * n:(i + 1) * n] for i in range(nb)]
    gb, qb, kb, ib, qdb, kdb = (blocks(x) for x in (g, q, kx, iv, qd, kd))

    upd = [_dot(i_.T.astype(BF16), k_) for i_, k_ in zip(ib, kdb)]
    st = st_ref[...]
    sts = []
    for i in range(nb):
        sts.append(st.astype(BF16))
        st = st * dec[i * n:i * n + 1, :] + upd[i]
    st_ref[...] = st
    outs = [_dot_nt(q_, s_) for q_, s_ in zip(qdb, sts)]
    for s in range(n):
        for i in range(nb):
            w = jnp.sum(qb[i] * kb[i][s:s + 1, :] * jnp.exp(gb[i] - gb[i][s:s + 1, :]), axis=-1, keepdims=True)
            outs[i] = outs[i] + jnp.where(trow >= s, w, 0.0) * ib[i][s:s + 1, :]
    o = jnp.concatenate(outs, axis=0)
    o = o * lax.rsqrt(jnp.mean(o * o, axis=-1, keepdims=True) + HGRN_NORM_EPS) * nw_ref[...]
    og = og_ref[0].astype(F32)
    y_ref[0] = (o * (og * jax.nn.sigmoid(og))).astype(y_ref.dtype)


def _hgrn(proj3, hgrn_lb, norm_w, *, layer, tb=256):
    b, t, _ = proj3.shape
    nh = HGRN_WIDTH // HGRN_HEAD_DIM
    off = (RWKV_SHIFT_WIDTH + 3 * CONV_WIDTH) // LANES
    spec = lambda o: pl.BlockSpec((1, tb, LANES), lambda bi, p, ti: (bi, ti, o + p))
    depth = hgrn_lb.shape[0]
    return pl.pallas_call(
        functools.partial(_hgrn_kernel, layer=layer, tb=tb),
        grid=(b, nh, t // tb),
        in_specs=[spec(off), spec(off + nh), spec(off + 2 * nh), spec(off + 3 * nh),
                  pl.BlockSpec((depth, LANES), lambda bi, p, ti: (0, p)),
                  pl.BlockSpec((1, LANES), lambda bi, p, ti: (0, p))],
        out_specs=pl.BlockSpec((1, tb, LANES), lambda bi, p, ti: (bi, ti, p)),
        out_shape=jax.ShapeDtypeStruct((b, t, HGRN_WIDTH), BF16),
        scratch_shapes=[pltpu.VMEM((LANES, LANES), F32)],
        compiler_params=pltpu.CompilerParams(
            dimension_semantics=("parallel", "parallel", "arbitrary"), vmem_limit_bytes=VMEM_LIMIT),
        name="hgrn2",
    )(proj3, proj3, proj3, proj3, hgrn_lb, norm_w)


def _cast_kernel(x_ref, o_ref, *, valid_rows, rows):
    x = x_ref[...]
    if valid_rows % rows:
        r = pl.program_id(1) * rows + lax.broadcasted_iota(jnp.int32, x.shape, 1)
        x = jnp.where(r < valid_rows, x, 0.0)
    o_ref[...] = x.astype(o_ref.dtype)


def _cast_rows(w, out_rows, *, rows=256):
    nl, r, c = w.shape
    return pl.pallas_call(
        functools.partial(_cast_kernel, valid_rows=r, rows=rows),
        grid=(nl, out_rows // rows),
        in_specs=[pl.BlockSpec((1, rows, c), lambda l, i: (l, i, 0))],
        out_specs=pl.BlockSpec((1, rows, c), lambda l, i: (l, i, 0)),
        out_shape=jax.ShapeDtypeStruct((nl, out_rows, c), BF16),
        compiler_params=pltpu.CompilerParams(
            dimension_semantics=("parallel", "parallel"), vmem_limit_bytes=VMEM_LIMIT),
        name="cast_rows",
    )(w)


def _split_gu_kernel(x_ref, g_ref, u_ref, *, d_ff):
    x = x_ref[0]
    pad = jnp.zeros((x.shape[0], g_ref.shape[2] - d_ff), g_ref.dtype)
    g_ref[0] = jnp.concatenate([x[:, :d_ff].astype(g_ref.dtype), pad], axis=1)
    u_ref[0] = jnp.concatenate([x[:, d_ff:].astype(u_ref.dtype), pad], axis=1)


def _split_gu(w_gu, d_ff, ffp, *, rows=256):
    nl, d, _ = w_gu.shape
    out = jax.ShapeDtypeStruct((nl, d, ffp), BF16)
    spec = pl.BlockSpec((1, rows, ffp), lambda l, i: (l, i, 0))
    return pl.pallas_call(
        functools.partial(_split_gu_kernel, d_ff=d_ff),
        grid=(nl, d // rows),
        in_specs=[pl.BlockSpec((1, rows, 2 * d_ff), lambda l, i: (l, i, 0))],
        out_specs=(spec, spec),
        out_shape=(out, out),
        compiler_params=pltpu.CompilerParams(
            dimension_semantics=("parallel", "parallel"), vmem_limit_bytes=VMEM_LIMIT),
        name="split_gate_up",
    )(w_gu)


def kernel(x, ffn1_norm, ffn1_w_gu, ffn1_w_down, mix_norm, w_in, rwkv_mu, rwkv_w_up, rwkv_w0,
           rwkv_a_up, rwkv_a0, rwkv_g_up, rwkv_k_k, rwkv_k_a, rwkv_r_k, rwkv_ln_w, rwkv_ln_b,
           conv_w, hgrn_lb, hgrn_norm, w_out, ffn2_norm, ffn2_w_gu, ffn2_w_down, final_norm):
    b, t, d = x.shape
    depth = ffn1_norm.shape[0]
    d_ff = ffn1_w_down.shape[1]
    ffp = -(-d_ff // 512) * 512
    m = b * t
    h = x.reshape(m, d)
    row = lambda v: v.reshape(1, -1)
    final_w = row(final_norm)

    wg1, wu1 = _split_gu(ffn1_w_gu, d_ff, ffp)
    wg2, wu2 = _split_gu(ffn2_w_gu, d_ff, ffp)
    wd1 = _cast_rows(ffn1_w_down, ffp)
    wd2 = _cast_rows(ffn2_w_down, ffp)
    w_in_b = _cast_rows(w_in, d)
    w_out_b = _cast_rows(w_out, d)
    zeros_low = jnp.zeros((depth, LANES - DECAY_RANK, RWKV_WIDTH), BF16)
    ww = jnp.concatenate([rwkv_w_up.astype(BF16), zeros_low], axis=1)
    wa = jnp.concatenate([zeros_low, rwkv_a_up.astype(BF16)], axis=1)
    wgate = rwkv_g_up.astype(BF16)

    for l in range(depth):
        h = _ffn(h, row(ffn1_norm[l]), wg1[l], wu1[l], wd1[l], final_w, final=False)

        proj = _proj(h, row(mix_norm[l]), w_in_b[l])
        proj3 = proj.reshape(b, t, -1)
        y_r = _rwkv(proj3, row(rwkv_mu[l]), ww[l], wa[l], wgate[l],
                    row(rwkv_w0[l]), row(rwkv_a0[l]), row(rwkv_k_k[l]), row(rwkv_k_a[l]),
                    row(rwkv_r_k[l]), row(rwkv_ln_w[l]), row(rwkv_ln_b[l]))
        y_c = _conv(proj3, conv_w[l])
        y_h = _hgrn(proj3, hgrn_lb, row(hgrn_norm[l]), layer=l)
        h = _outproj(h, y_r.reshape(m, -1), y_c.reshape(m, -1), y_h.reshape(m, -1), w_out_b[l])

        h = _ffn(h, row(ffn2_norm[l]), wg2[l], wu2[l], wd2[l], final_w, final=(l == depth - 1))
    return h.reshape(b, t, d)
```

```python
import functools

import jax
import jax.numpy as jnp
from jax import lax
from jax.experimental import pallas as pl
from jax.experimental.pallas import tpu as pltpu

F32 = jnp.float32
BF16 = jnp.bfloat16

RWKV_HEAD_DIM = 64
RWKV_WIDTH = 1024
DECAY_RANK = 64
AAA_RANK = 64
GATE_RANK = 128
RWKV_LOW = DECAY_RANK + AAA_RANK + GATE_RANK
RWKV_SHIFT_WIDTH = 3 * RWKV_WIDTH + RWKV_LOW
RWKV_GN_EPS = 64e-5
CONV_WIDTH = 512
CONV_K = 3
HGRN_HEAD_DIM = 128
HGRN_WIDTH = 512
HGRN_NORM_EPS = 1e-5
HGRN_MIN_FORGET = 1e-30
FFN_RESIDUAL_SCALE = 0.5
RMS_EPS = 1e-6

LANES = 128
RWKV_CHUNK = 64
HGRN_BLOCK = 16
VMEM_LIMIT = 56 * 1024 * 1024


def _dot(a, b):
    return jnp.dot(a, b, preferred_element_type=F32)


def _dot_nt(a, b):
    return lax.dot_general(a, b, (((1,), (1,)), ((), ())), preferred_element_type=F32)


def _rms(x, w):
    return x * lax.rsqrt(jnp.mean(x * x, axis=-1, keepdims=True) + RMS_EPS) * w


def _split_bf16(x):
    hi = x.astype(BF16)
    lo = (x - hi.astype(F32)).astype(BF16)
    return hi, lo


def _ffn_kernel(x_ref, nw_ref, wg_ref, wu_ref, wd_ref, fw_ref, o_ref, xn_ref, *, final):
    j = pl.program_id(1)

    @pl.when(j == 0)
    def _():
        xn_ref[...] = _rms(x_ref[...], nw_ref[...]).astype(BF16)
        o_ref[...] = jnp.zeros_like(o_ref)

    xn = xn_ref[...]
    g = _dot(xn, wg_ref[...])
    u = _dot(xn, wu_ref[...])
    a = (g * jax.nn.sigmoid(g) * u).astype(BF16)
    o_ref[...] += _dot(a, wd_ref[...])

    @pl.when(j == pl.num_programs(1) - 1)
    def _():
        rows = 256

        def chunk(c, carry):
            sl = pl.ds(pl.multiple_of(c * rows, rows), rows)
            h = x_ref[sl, :] + FFN_RESIDUAL_SCALE * o_ref[sl, :]
            if final:
                h = _rms(h, fw_ref[...])
            o_ref[sl, :] = h
            return carry

        lax.fori_loop(0, o_ref.shape[0] // rows, chunk, 0)


def _ffn(h, norm_w, wg, wu, wd, final_w, *, layer, final, tm=1024, tf=512):
    m, d = h.shape
    ffp = wg.shape[2]
    grid = (m // tm, ffp // tf)
    return pl.pallas_call(
        functools.partial(_ffn_kernel, final=final),
        grid=grid,
        in_specs=[
            pl.BlockSpec((tm, d), lambda i, j: (i, 0)),
            pl.BlockSpec((1, d), lambda i, j: (0, 0)),
            pl.BlockSpec((None, d, tf), lambda i, j: (layer, 0, j)),
            pl.BlockSpec((None, d, tf), lambda i, j: (layer, 0, j)),
            pl.BlockSpec((None, tf, d), lambda i, j: (layer, j, 0)),
            pl.BlockSpec((1, d), lambda i, j: (0, 0)),
        ],
        out_specs=pl.BlockSpec((tm, d), lambda i, j: (i, 0)),
        out_shape=jax.ShapeDtypeStruct((m, d), F32),
        scratch_shapes=[pltpu.VMEM((tm, d), BF16)],
        compiler_params=pltpu.CompilerParams(
            dimension_semantics=("parallel", "arbitrary"), vmem_limit_bytes=VMEM_LIMIT),
        name="ffn",
    )(h, norm_w, wg, wu, wd, final_w)


def _proj_kernel(x_ref, nw_ref, w_ref, o_ref, xn_ref):
    @pl.when(pl.program_id(1) == 0)
    def _():
        xn_ref[...] = _rms(x_ref[...], nw_ref[...]).astype(BF16)

    o_ref[...] = _dot(xn_ref[...], w_ref[...]).astype(o_ref.dtype)


def _proj(h, norm_w, w, *, layer, tm=1024, tn=2304):
    m, d = h.shape
    n = w.shape[2]
    return pl.pallas_call(
        _proj_kernel,
        grid=(m // tm, n // tn),
        in_specs=[
            pl.BlockSpec((tm, d), lambda i, j: (i, 0)),
            pl.BlockSpec((1, d), lambda i, j: (0, 0)),
            pl.BlockSpec((None, d, tn), lambda i, j: (layer, 0, j)),
        ],
        out_specs=pl.BlockSpec((tm, tn), lambda i, j: (i, j)),
        out_shape=jax.ShapeDtypeStruct((m, n), BF16),
        scratch_shapes=[pltpu.VMEM((tm, d), BF16)],
        compiler_params=pltpu.CompilerParams(
            dimension_semantics=("parallel", "arbitrary"), vmem_limit_bytes=VMEM_LIMIT),
        name="proj",
    )(h, norm_w, w)


def _outproj_kernel(h_ref, yr_ref, yc_ref, yh_ref, wr_ref, wc_ref, wh_ref, o_ref):
    acc = _dot(yr_ref[...], wr_ref[...])
    acc += _dot(yc_ref[...], wc_ref[...])
    acc += _dot(yh_ref[...], wh_ref[...])
    o_ref[...] = h_ref[...] + acc


def _outproj(h, yr, yc, yh, wo, *, layer, tm=512):
    m, d = h.shape
    nr, ncv, nh = yr.shape[1], yc.shape[1], yh.shape[1]
    row = lambda i: (i, 0)
    return pl.pallas_call(
        _outproj_kernel,
        grid=(m // tm,),
        in_specs=[
            pl.BlockSpec((tm, d), row),
            pl.BlockSpec((tm, yr.shape[1]), row),
            pl.BlockSpec((tm, yc.shape[1]), row),
            pl.BlockSpec((tm, yh.shape[1]), row),
            pl.BlockSpec((None, nr, d), lambda i: (layer, 0, 0)),
            pl.BlockSpec((None, ncv, d), lambda i: (layer, nr // ncv, 0)),
            pl.BlockSpec((None, nh, d), lambda i: (layer, (nr + ncv) // nh, 0)),
        ],
        out_specs=pl.BlockSpec((tm, d), row),
        out_shape=jax.ShapeDtypeStruct((m, d), F32),
        compiler_params=pltpu.CompilerParams(
            dimension_semantics=("parallel",), vmem_limit_bytes=VMEM_LIMIT),
        name="outproj",
    )(h, yr, yc, yh, wo, wo, wo)


def _prev_rows(x, carry, k=1):
    rows = lax.broadcasted_iota(jnp.int32, (8, x.shape[1]), 0)
    prev = pltpu.roll(x, k, 0)
    head = jnp.where(rows < k, pltpu.roll(carry, k, 0), prev[:8])
    return jnp.concatenate([head, prev[8:]], axis=0)


def _seg_sum(x, ones_bd):
    return _dot(x.astype(BF16), ones_bd)


def _rwkv_prepare(lw, r, k2, v, av, bv, c, tick):
    n = 2 * RWKV_CHUNK
    m0, m1 = c["m0"], c["m1"]
    stack = lambda x: jnp.concatenate([x * m0, x * m1], axis=0)
    each = lambda f, *ls: [f(*xs) for xs in zip(*ls)]

    def cumsum(x):
        hi, lo = _split_bf16(x)
        gcat = _dot(c["tri"], jnp.concatenate([hi, lo], axis=1))
        return gcat[:, :LANES] + gcat[:, LANES:]

    g = each(cumsum, lw)
    tick()
    gl = [x[RWKV_CHUNK - 1:RWKV_CHUNK, :] for x in g]
    e_g = each(jnp.exp, g)
    e_ng = each(lambda x: jnp.exp(-x), g)
    e_d = each(lambda x, y: jnp.exp(x - y), gl, g)
    ats = each(lambda a_, g_, l_: stack(a_ * jnp.exp(g_ - l_)), av, g, lw)
    rts = each(lambda r_, e_: stack(r_ * e_), r, e_g)
    bts = each(lambda b_, e_: stack(b_ * e_), bv, e_ng)
    kts = each(lambda k_, e_: stack(k_ * e_), k2, e_ng)
    bes = each(lambda b_, e_: stack(b_ * e_).astype(BF16), bv, e_d)
    kes = each(lambda k_, e_: stack(k_ * e_).astype(BF16), k2, e_d)
    vs = each(stack, v)
    vsb = [x.astype(BF16) for x in vs]

    p = each(lambda a_, r_, b_, k_: _dot_nt(jnp.concatenate([a_, r_], axis=0).astype(BF16),
                                            jnp.concatenate([b_, k_], axis=0).astype(BF16)),
             ats, rts, bts, kts)
    a_ab = [jnp.where(c["strict"], x[:n, :n], 0.0) for x in p]
    a_kr = [jnp.concatenate([jnp.where(c["strict"], x[:n, n:], 0.0),
                             jnp.where(c["incl"], x[n:, n:], 0.0)], axis=0).astype(BF16) for x in p]
    a_rb = [jnp.where(c["incl"], x[n:, :n], 0.0).astype(BF16) for x in p]
    tick()

    h = RWKV_CHUNK
    side = lambda x: x[:h] + x[h:]
    diag = lambda x: jnp.concatenate([x, x], axis=0) * c["own"]
    a_sbs = [side(x) for x in a_ab]
    inv = [c["eye"] + x for x in a_sbs]
    q = [_dot(x.astype(BF16), y.astype(BF16)) for x, y in zip(a_sbs, a_ab)]
    tick()
    for _ in range(4):
        z = each(lambda i_, q_: _dot(jnp.concatenate([i_, q_], axis=0).astype(BF16), diag(q_).astype(BF16)), inv, q)
        inv = each(lambda i_, z_: i_ + z_[:h], inv, z)
        q = [x[h:] for x in z]
        tick()
    inv = each(lambda i_, q_: diag(i_ + _dot(i_.astype(BF16), diag(q_).astype(BF16))), inv, q)
    tick()

    akv_rkv = each(_dot, a_kr, vsb)
    tick()
    tt = each(lambda i_, a_, x_: _dot(i_.astype(BF16), jnp.concatenate([a_, x_[:n]], axis=1).astype(BF16)),
              inv, ats, akv_rkv)
    tick()
    rr = each(lambda a_, t_: _dot(a_, t_.astype(BF16)), a_rb, tt)
    rq = each(lambda r_, x_: (r_ + x_[:, :LANES]).astype(BF16), rts, rr)
    oc = each(lambda x_, y_: x_[:, LANES:] + y_[n:], rr, akv_rkv)
    bb = each(lambda t_, b_: _dot(t_.T.astype(BF16), b_), tt, bes)
    bta = [x[:LANES].astype(BF16) for x in bb]
    nct = each(lambda x_, v_, k_: x_[LANES:] + _dot(v_.T.astype(BF16), k_), bb, vs, kes)
    e_gl = each(jnp.exp, gl)
    return rq, oc, bta, nct, e_gl


def _rwkv_kernel(r_ref, k_ref, v_ref, low_ref, mur_ref, muk_ref, muv_ref, mul_ref,
                 ww_ref, wa_ref, wg_ref, w0_ref, a0_ref, kk_ref, ka_ref, rk_ref, lnw_ref, lnb_ref,
                 y_ref, cr_ref, ck_ref, cv_ref, cl_ref, st_ref, feat_ref,
                 rq_ref, oc_ref, bta_ref, nct_ref, egl_ref, bonus_ref, gate_ref, *, tb, nt):
    s = pl.program_id(0)
    nc = tb // RWKV_CHUNK

    @pl.when(s == 0)
    def _():
        for ref in (cr_ref, ck_ref, cv_ref, cl_ref, st_ref, feat_ref, rq_ref, oc_ref, bta_ref, nct_ref, egl_ref,
                    bonus_ref, gate_ref):
            ref[...] = jnp.zeros_like(ref)

    lane = lax.broadcasted_iota(jnp.int32, (1, LANES), 1)
    m0 = (lane < RWKV_HEAD_DIM).astype(F32)
    m1 = 1.0 - m0
    ri = lax.broadcasted_iota(jnp.int32, (LANES, LANES), 0)
    ci = lax.broadcasted_iota(jnp.int32, (LANES, LANES), 1)
    same = (ri < RWKV_HEAD_DIM) == (ci < RWKV_HEAD_DIM)
    ones_bd = same.astype(F32).astype(BF16)
    ti = lax.broadcasted_iota(jnp.int32, (RWKV_CHUNK, RWKV_CHUNK), 0)
    si = lax.broadcasted_iota(jnp.int32, (RWKV_CHUNK, RWKV_CHUNK), 1)
    tw = lax.broadcasted_iota(jnp.int32, (RWKV_CHUNK, LANES), 0)
    sw = lax.broadcasted_iota(jnp.int32, (RWKV_CHUNK, LANES), 1) & (RWKV_HEAD_DIM - 1)
    consts = dict(
        m0=m0, m1=m1, own=same.astype(F32),
        strict=same & (ri > ci), incl=same & (ri >= ci),
        eye=(tw == sw).astype(F32),
        tri=(ti >= si).astype(F32).astype(BF16),
    )

    rd = lax.rem(s, 2)
    rd3 = lax.rem(s + 1, 3)
    seq_start = lax.rem(jnp.maximum(s - 2, 0), nt) == 0
    state = [st_ref[...] * jnp.where(seq_start, 0.0, 1.0)]
    outs = []
    inv_n = 1.0 / RWKV_HEAD_DIM
    own = same.astype(F32)

    def recurrence_step():
        i = len(outs)
        if i == nc:
            return
        st = state[0]
        stb = st.astype(BF16)
        o_stack = _dot_nt(rq_ref[rd, i], stb) + oc_ref[rd, i]
        state[0] = st * egl_ref[rd, i][0:1, :] + _dot(stb, bta_ref[rd, i]) + nct_ref[rd, i]
        mean = jnp.sum(o_stack, axis=-1, keepdims=True) * inv_n
        cen = (o_stack - mean) * own
        var = jnp.sum(cen * cen, axis=-1, keepdims=True) * inv_n
        on = cen * lax.rsqrt(var + RWKV_GN_EPS)
        outs.append(on[:RWKV_CHUNK] + on[RWKV_CHUNK:])

    wf = lax.rem(s, 2)
    wf3 = lax.rem(s, 3)
    keep = jnp.where(lax.rem(jnp.minimum(s, pl.num_programs(0) - 3), nt) == 0, 0.0, 1.0)
    f = {}

    def lerp(x_ref, c_ref, mu_ref):
        x = x_ref[0].astype(F32)
        prev = _prev_rows(x, c_ref[...] * keep)
        c_ref[...] = x[tb - 8:tb, :]
        return x + (prev - x) * mu_ref[...]

    def feat_low():
        low = lerp(low_ref, cl_ref, mul_ref)
        f["wa_in"] = low[:, :LANES]
        gate_ref[wf3] = _dot(jax.nn.sigmoid(low[:, LANES:]).astype(BF16), wg_ref[...])

    def feat_decay():
        xw = w0_ref[...] + _dot(jnp.tanh(f["wa_in"]).astype(BF16), ww_ref[...])
        feat_ref[wf, 0] = -jnp.exp(jnp.float32(-0.5)) * jax.nn.sigmoid(xw)
        f["a"] = jax.nn.sigmoid(a0_ref[...] + _dot(f["wa_in"].astype(BF16), wa_ref[...]))

    def feat_k():
        k = lerp(k_ref, ck_ref, muk_ref)
        kk = k * kk_ref[...]
        kk = kk * lax.rsqrt(jnp.maximum(_seg_sum(kk * kk, ones_bd), 1e-24))
        f["k2"] = k * (1.0 + (f["a"] - 1.0) * ka_ref[...])
        feat_ref[wf, 2] = f["k2"]
        feat_ref[wf, 4] = -kk
        feat_ref[wf, 5] = kk * f["a"]

    def feat_rv():
        r = lerp(r_ref, cr_ref, mur_ref)
        v = lerp(v_ref, cv_ref, muv_ref)
        feat_ref[wf, 1] = r
        feat_ref[wf, 3] = v
        bonus_ref[wf3] = _seg_sum(r * f["k2"] * rk_ref[...], ones_bd) * v

    pieces = [feat_low, feat_decay, feat_k, feat_rv]

    def tick():
        if pieces:
            pieces.pop(0)()
        recurrence_step()

    rf = lax.rem(s + 1, 2)
    wr = lax.rem(s + 1, 2)
    chunks = lambda j: [feat_ref[rf, j, i * RWKV_CHUNK:(i + 1) * RWKV_CHUNK, :] for i in range(nc)]
    rq, oc, bta, nct, e_gl = _rwkv_prepare(chunks(0), chunks(1), chunks(2), chunks(3), chunks(4), chunks(5),
                                           consts, tick)
    while pieces or len(outs) < nc:
        tick()

    st_ref[...] = state[0]
    o = jnp.concatenate(outs, axis=0) * lnw_ref[...] + lnb_ref[...]
    y_ref[0] = ((o + bonus_ref[rd3]) * gate_ref[rd3]).astype(y_ref.dtype)

    for i in range(nc):
        rq_ref[wr, i] = rq[i]
        oc_ref[wr, i] = oc[i]
        bta_ref[wr, i] = bta[i]
        nct_ref[wr, i] = nct[i]
        egl_ref[wr, i] = jnp.broadcast_to(e_gl[i], (8, LANES))


def _rwkv(proj3, mu, ww, wa, wg, w0, a0, k_k, k_a, r_k, ln_w, ln_b, *, tb=512):
    b, t, _ = proj3.shape
    npairs = RWKV_WIDTH // LANES
    nt = t // tb
    nc = tb // RWKV_CHUNK
    nblocks = b * npairs * nt
    off_k = RWKV_WIDTH // LANES
    off_v = 2 * RWKV_WIDTH // LANES
    off_low = 3 * RWKV_WIDTH // RWKV_LOW

    def where(item):
        return item // (nt * npairs), item % nt, (item // nt) % npairs

    cur = lambda s: where(jnp.minimum(s, nblocks - 1))
    lag = lambda s: where(jnp.maximum(s - 2, 0))
    act = lambda off: pl.BlockSpec((1, tb, LANES), lambda s: (cur(s)[0], cur(s)[1], off + cur(s)[2]))
    vec = lambda off: pl.BlockSpec((1, LANES), lambda s: (0, off + cur(s)[2]))
    lagvec = pl.BlockSpec((1, LANES), lambda s: (0, lag(s)[2]))
    wspec = pl.BlockSpec((LANES, LANES), lambda s: (0, cur(s)[2]))
    return pl.pallas_call(
        functools.partial(_rwkv_kernel, tb=tb, nt=nt),
        grid=(nblocks + 2,),
        in_specs=[
            act(0), act(off_k), act(off_v),
            pl.BlockSpec((1, tb, RWKV_LOW), lambda s: (cur(s)[0], cur(s)[1], off_low)),
            vec(0), vec(off_k), vec(off_v),
            pl.BlockSpec((1, RWKV_LOW), lambda s: (0, off_low)),
            wspec, wspec, wspec,
            vec(0), vec(0), vec(0), vec(0), vec(0), lagvec, lagvec,
        ],
        out_specs=pl.BlockSpec((1, tb, LANES), lambda s: lag(s)),
        out_shape=jax.ShapeDtypeStruct((b, t, RWKV_WIDTH), BF16),
        scratch_shapes=[pltpu.VMEM((8, LANES), F32), pltpu.VMEM((8, LANES), F32), pltpu.VMEM((8, LANES), F32),
                        pltpu.VMEM((8, RWKV_LOW), F32), pltpu.VMEM((LANES, LANES), F32),
                        pltpu.VMEM((2, 6, tb, LANES), F32),
                        pltpu.VMEM((2, nc, LANES, LANES), BF16), pltpu.VMEM((2, nc, LANES, LANES), F32),
                        pltpu.VMEM((2, nc, LANES, LANES), BF16), pltpu.VMEM((2, nc, LANES, LANES), F32),
                        pltpu.VMEM((2, nc, 8, LANES), F32),
                        pltpu.VMEM((3, tb, LANES), F32), pltpu.VMEM((3, tb, LANES), F32)],
        compiler_params=pltpu.CompilerParams(
            dimension_semantics=("arbitrary",), vmem_limit_bytes=VMEM_LIMIT),
        name="rwkv7",
    )(proj3, proj3, proj3, proj3, mu, mu, mu, mu, ww, wa, wg, w0, a0, k_k, k_a, r_k, ln_w, ln_b)


def _conv_kernel(c_ref, x_ref, b_ref, w_ref, y_ref, carry_ref):
    @pl.when(pl.program_id(2) == 0)
    def _():
        carry_ref[...] = jnp.zeros_like(carry_ref)

    z = c_ref[0].astype(F32) * x_ref[0].astype(F32)
    tb = z.shape[0]
    w = w_ref[...]
    y = z * w[CONV_K - 1:CONV_K, :]
    for j in range(1, CONV_K):
        y = y + _prev_rows(z, carry_ref[...], j) * w[CONV_K - 1 - j:CONV_K - j, :]
    carry_ref[...] = z[tb - 8:tb, :]
    y_ref[0] = (b_ref[0].astype(F32) * y).astype(y_ref.dtype)


def _conv(proj3, conv_w, *, tb=1024, cw=2 * LANES):
    b, t, _ = proj3.shape
    tb = min(tb, t)
    nblk = CONV_WIDTH // cw
    off = RWKV_SHIFT_WIDTH // cw
    spec = lambda o: pl.BlockSpec((1, tb, cw), lambda bi, p, ti: (bi, ti, o + p))
    return pl.pallas_call(
        _conv_kernel,
        grid=(b, nblk, t // tb),
        in_specs=[spec(off), spec(off + nblk), spec(off + 2 * nblk),
                  pl.BlockSpec((CONV_K, cw), lambda bi, p, ti: (0, p))],
        out_specs=pl.BlockSpec((1, tb, cw), lambda bi, p, ti: (bi, ti, p)),
        out_shape=jax.ShapeDtypeStruct((b, t, CONV_WIDTH), BF16),
        scratch_shapes=[pltpu.VMEM((8, cw), F32)],
        compiler_params=pltpu.CompilerParams(
            dimension_semantics=("parallel", "parallel", "arbitrary"), vmem_limit_bytes=VMEM_LIMIT),
        name="shortconv",
    )(proj3, proj3, proj3, conv_w)


def _hgrn_kernel(q_ref, f_ref, i_ref, og_ref, lb_ref, nw_ref, y_ref, st_ref, *, layer, tb):
    @pl.when(pl.program_id(2) == 0)
    def _():
        st_ref[...] = jnp.zeros_like(st_ref)

    lbp = lb_ref[...]
    e = jnp.exp(lbp - jnp.max(lbp, axis=0, keepdims=True))
    pr = e / jnp.sum(e, axis=0, keepdims=True)
    lb = jnp.zeros((1, LANES), F32)
    for l in range(1, layer + 1):
        lb = lb + pr[l:l + 1, :]

    qr = q_ref[0].astype(F32)
    q = qr * jax.nn.sigmoid(qr)
    f = f_ref[0].astype(F32)
    forget = lb + (1.0 - lb) * jax.nn.sigmoid(f)
    logf = jnp.log(jnp.maximum(forget, HGRN_MIN_FORGET))
    kx = (1.0 - lb) * jax.nn.sigmoid(-f)
    iv = i_ref[0].astype(F32)

    n = HGRN_BLOCK
    nb = tb // n
    ti = lax.broadcasted_iota(jnp.int32, (tb, tb), 0)
    si = lax.broadcasted_iota(jnp.int32, (tb, tb), 1)
    same = (ti ^ si) < n
    sums = jnp.concatenate([(same & (ti >= si)).astype(F32).astype(BF16), same.astype(F32).astype(BF16)], axis=0)
    hi, lo = _split_bf16(logf)
    gg = _dot(sums, jnp.concatenate([hi, lo], axis=1))
    g = gg[:tb, :LANES] + gg[:tb, LANES:]
    gl = gg[tb:, :LANES] + gg[tb:, LANES:]
    qd = (q * jnp.exp(g)).astype(BF16)
    kd = (kx * jnp.exp(gl - g)).astype(BF16)
    dec = jnp.exp(gl)
    trow = lax.broadcasted_iota(jnp.int32, (n, 1), 0)
    blocks = lambda x: [x[i * n:(i + 1) * n] for i in range(nb)]
    gb, qb, kb, ib, qdb, kdb = (blocks(x) for x in (g, q, kx, iv, qd, kd))

    upd = [_dot(i_.T.astype(BF16), k_) for i_, k_ in zip(ib, kdb)]
    st = st_ref[...]
    sts = []
    for i in range(nb):
        sts.append(st.astype(BF16))
        st = st * dec[i * n:i * n + 1, :] + upd[i]
    st_ref[...] = st
    outs = [_dot_nt(q_, s_) for q_, s_ in zip(qdb, sts)]
    for s in range(n):
        for i in range(nb):
            w = jnp.sum(qb[i] * kb[i][s:s + 1, :] * jnp.exp(gb[i] - gb[i][s:s + 1, :]), axis=-1, keepdims=True)
            outs[i] = outs[i] + jnp.where(trow >= s, w, 0.0) * ib[i][s:s + 1, :]
    o = jnp.concatenate(outs, axis=0)
    o = o * lax.rsqrt(jnp.mean(o * o, axis=-1, keepdims=True) + HGRN_NORM_EPS) * nw_ref[...]
    og = og_ref[0].astype(F32)
    y_ref[0] = (o * (og * jax.nn.sigmoid(og))).astype(y_ref.dtype)


def _hgrn(proj3, hgrn_lb, norm_w, *, layer, tb=512):
    b, t, _ = proj3.shape
    tb = min(tb, t)
    nh = HGRN_WIDTH // HGRN_HEAD_DIM
    off = (RWKV_SHIFT_WIDTH + 3 * CONV_WIDTH) // LANES
    spec = lambda o: pl.BlockSpec((1, tb, LANES), lambda bi, p, ti: (bi, ti, o + p))
    depth = hgrn_lb.shape[0]
    return pl.pallas_call(
        functools.partial(_hgrn_kernel, layer=layer, tb=tb),
        grid=(b, nh, t // tb),
        in_specs=[spec(off), spec(off + nh), spec(off + 2 * nh), spec(off + 3 * nh),
                  pl.BlockSpec((depth, LANES), lambda bi, p, ti: (0, p)),
                  pl.BlockSpec((1, LANES), lambda bi, p, ti: (0, p))],
        out_specs=pl.BlockSpec((1, tb, LANES), lambda bi, p, ti: (bi, ti, p)),
        out_shape=jax.ShapeDtypeStruct((b, t, HGRN_WIDTH), BF16),
        scratch_shapes=[pltpu.VMEM((LANES, LANES), F32)],
        compiler_params=pltpu.CompilerParams(
            dimension_semantics=("parallel", "parallel", "arbitrary"), vmem_limit_bytes=VMEM_LIMIT),
        name="hgrn2",
    )(proj3, proj3, proj3, proj3, hgrn_lb, norm_w)


def _cast_kernel(x_ref, o_ref, *, valid_rows, rows):
    x = x_ref[...]
    if valid_rows % rows:
        r = pl.program_id(1) * rows + lax.broadcasted_iota(jnp.int32, x.shape, 1)
        x = jnp.where(r < valid_rows, x, 0.0)
    o_ref[...] = x.astype(o_ref.dtype)


def _cast_rows(w, out_rows, *, rows=256):
    nl, r, c = w.shape
    return pl.pallas_call(
        functools.partial(_cast_kernel, valid_rows=r, rows=rows),
        grid=(nl, out_rows // rows),
        in_specs=[pl.BlockSpec((1, rows, c), lambda l, i: (l, i, 0))],
        out_specs=pl.BlockSpec((1, rows, c), lambda l, i: (l, i, 0)),
        out_shape=jax.ShapeDtypeStruct((nl, out_rows, c), BF16),
        compiler_params=pltpu.CompilerParams(
            dimension_semantics=("parallel", "parallel"), vmem_limit_bytes=VMEM_LIMIT),
        name="cast_rows",
    )(w)


def _split_gu_kernel(x_ref, g_ref, u_ref, *, d_ff):
    x = x_ref[0]
    pad = jnp.zeros((x.shape[0], g_ref.shape[2] - d_ff), g_ref.dtype)
    g_ref[0] = jnp.concatenate([x[:, :d_ff].astype(g_ref.dtype), pad], axis=1)
    u_ref[0] = jnp.concatenate([x[:, d_ff:].astype(u_ref.dtype), pad], axis=1)


def _split_gu(w_gu, d_ff, ffp, *, rows=256):
    nl, d, _ = w_gu.shape
    out = jax.ShapeDtypeStruct((nl, d, ffp), BF16)
    spec = pl.BlockSpec((1, rows, ffp), lambda l, i: (l, i, 0))
    return pl.pallas_call(
        functools.partial(_split_gu_kernel, d_ff=d_ff),
        grid=(nl, d // rows),
        in_specs=[pl.BlockSpec((1, rows, 2 * d_ff), lambda l, i: (l, i, 0))],
        out_specs=(spec, spec),
        out_shape=(out, out),
        compiler_params=pltpu.CompilerParams(
            dimension_semantics=("parallel", "parallel"), vmem_limit_bytes=VMEM_LIMIT),
        name="split_gate_up",
    )(w_gu)


def kernel(x, ffn1_norm, ffn1_w_gu, ffn1_w_down, mix_norm, w_in, rwkv_mu, rwkv_w_up, rwkv_w0,
           rwkv_a_up, rwkv_a0, rwkv_g_up, rwkv_k_k, rwkv_k_a, rwkv_r_k, rwkv_ln_w, rwkv_ln_b,
           conv_w, hgrn_lb, hgrn_norm, w_out, ffn2_norm, ffn2_w_gu, ffn2_w_down, final_norm):
    b, t, d = x.shape
    depth = ffn1_norm.shape[0]
    d_ff = ffn1_w_down.shape[1]
    ffp = -(-d_ff // 512) * 512
    m = b * t
    h = x.reshape(m, d)
    row = lambda v: v.reshape(1, -1)
    final_w = row(final_norm)

    wg1, wu1 = _split_gu(ffn1_w_gu, d_ff, ffp)
    wg2, wu2 = _split_gu(ffn2_w_gu, d_ff, ffp)
    wd1 = _cast_rows(ffn1_w_down, ffp)
    wd2 = _cast_rows(ffn2_w_down, ffp)
    w_in_b = _cast_rows(w_in, d)
    w_out_b = _cast_rows(w_out, d)
    zeros_low = jnp.zeros((depth, LANES - DECAY_RANK, RWKV_WIDTH), BF16)
    ww = jnp.concatenate([rwkv_w_up.astype(BF16), zeros_low], axis=1)
    wa = jnp.concatenate([zeros_low, rwkv_a_up.astype(BF16)], axis=1)
    wgate = rwkv_g_up.astype(BF16)

    for l in range(depth):
        h = _ffn(h, row(ffn1_norm[l]), wg1, wu1, wd1, final_w, layer=l, final=False)

        proj = _proj(h, row(mix_norm[l]), w_in_b, layer=l)
        proj3 = proj.reshape(b, t, -1)
        y_r = _rwkv(proj3, row(rwkv_mu[l]), ww[l], wa[l], wgate[l],
                    row(rwkv_w0[l]), row(rwkv_a0[l]), row(rwkv_k_k[l]), row(rwkv_k_a[l]),
                    row(rwkv_r_k[l]), row(rwkv_ln_w[l]), row(rwkv_ln_b[l]))
        y_c = _conv(proj3, conv_w[l])
        y_h = _hgrn(proj3, hgrn_lb, row(hgrn_norm[l]), layer=l)
        h = _outproj(h, y_r.reshape(m, -1), y_c.reshape(m, -1), y_h.reshape(m, -1), w_out_b, layer=l)

        h = _ffn(h, row(ffn2_norm[l]), wg2, wu2, wd2, final_w, layer=l, final=(l == depth - 1))
    return h.reshape(b, t, d)
```

```python
import functools

import jax
import jax.numpy as jnp
from jax import lax
from jax.experimental import pallas as pl
from jax.experimental.pallas import tpu as pltpu

F32 = jnp.float32
BF16 = jnp.bfloat16

RWKV_HEAD_DIM = 64
RWKV_WIDTH = 1024
DECAY_RANK = 64
AAA_RANK = 64
GATE_RANK = 128
RWKV_LOW = DECAY_RANK + AAA_RANK + GATE_RANK
RWKV_SHIFT_WIDTH = 3 * RWKV_WIDTH + RWKV_LOW
RWKV_GN_EPS = 64e-5
CONV_WIDTH = 512
CONV_K = 3
HGRN_HEAD_DIM = 128
HGRN_WIDTH = 512
HGRN_NORM_EPS = 1e-5
HGRN_MIN_FORGET = 1e-30
FFN_RESIDUAL_SCALE = 0.5
RMS_EPS = 1e-6

LANES = 128
RWKV_CHUNK = 64
HGRN_BLOCK = 16
VMEM_LIMIT = 56 * 1024 * 1024


def _dot(a, b):
    return jnp.dot(a, b, preferred_element_type=F32)


def _dot_nt(a, b):
    return lax.dot_general(a, b, (((1,), (1,)), ((), ())), preferred_element_type=F32)


def _rms(x, w):
    return x * lax.rsqrt(jnp.mean(x * x, axis=-1, keepdims=True) + RMS_EPS) * w


def _split_bf16(x):
    hi = x.astype(BF16)
    lo = (x - hi.astype(F32)).astype(BF16)
    return hi, lo


def _ffn_kernel(x_ref, nw_ref, wg_ref, wu_ref, wd_ref, fw_ref, o_ref, xn_ref, *, final):
    j = pl.program_id(1)

    def partial_sum():
        xn = xn_ref[...]
        g = _dot(xn, wg_ref[...])
        u = _dot(xn, wu_ref[...])
        a = (g * jax.nn.sigmoid(g) * u).astype(BF16)
        return _dot(a, wd_ref[...])

    @pl.when(j == 0)
    def _():
        xn_ref[...] = _rms(x_ref[...], nw_ref[...]).astype(BF16)
        o_ref[...] = partial_sum()

    last = pl.num_programs(1) - 1

    @pl.when((j > 0) & (j < last))
    def _():
        o_ref[...] += partial_sum()

    @pl.when(j == last)
    def _():
        h = x_ref[...] + FFN_RESIDUAL_SCALE * (o_ref[...] + partial_sum())
        if final:
            h = _rms(h, fw_ref[...])
        o_ref[...] = h


def _ffn(h, norm_w, wg, wu, wd, final_w, *, layer, final, tm=1024, tf=512):
    m, d = h.shape
    ffp = wg.shape[2]
    grid = (m // tm, ffp // tf)
    assert grid[1] >= 2
    return pl.pallas_call(
        functools.partial(_ffn_kernel, final=final),
        grid=grid,
        in_specs=[
            pl.BlockSpec((tm, d), lambda i, j: (i, 0)),
            pl.BlockSpec((1, d), lambda i, j: (0, 0)),
            pl.BlockSpec((None, d, tf), lambda i, j: (layer, 0, j)),
            pl.BlockSpec((None, d, tf), lambda i, j: (layer, 0, j)),
            pl.BlockSpec((None, tf, d), lambda i, j: (layer, j, 0)),
            pl.BlockSpec((1, d), lambda i, j: (0, 0)),
        ],
        out_specs=pl.BlockSpec((tm, d), lambda i, j: (i, 0)),
        out_shape=jax.ShapeDtypeStruct((m, d), F32),
        scratch_shapes=[pltpu.VMEM((tm, d), BF16)],
        compiler_params=pltpu.CompilerParams(
            dimension_semantics=("parallel", "arbitrary"), vmem_limit_bytes=VMEM_LIMIT),
        name="ffn",
    )(h, norm_w, wg, wu, wd, final_w)


def _proj_kernel(x_ref, nw_ref, w_ref, o_ref, xn_ref):
    @pl.when(pl.program_id(1) == 0)
    def _():
        xn_ref[...] = _rms(x_ref[...], nw_ref[...]).astype(BF16)
        o_ref[...] = _dot(xn_ref[...], w_ref[...]).astype(o_ref.dtype)

    @pl.when(pl.program_id(1) > 0)
    def _():
        o_ref[...] = _dot(xn_ref[...], w_ref[...]).astype(o_ref.dtype)


def _proj(h, norm_w, w, *, layer, tm=1024, tn=2304):
    m, d = h.shape
    n = w.shape[2]
    return pl.pallas_call(
        _proj_kernel,
        grid=(m // tm, n // tn),
        in_specs=[
            pl.BlockSpec((tm, d), lambda i, j: (i, 0)),
            pl.BlockSpec((1, d), lambda i, j: (0, 0)),
            pl.BlockSpec((None, d, tn), lambda i, j: (layer, 0, j)),
        ],
        out_specs=pl.BlockSpec((tm, tn), lambda i, j: (i, j)),
        out_shape=jax.ShapeDtypeStruct((m, n), BF16),
        scratch_shapes=[pltpu.VMEM((tm, d), BF16)],
        compiler_params=pltpu.CompilerParams(
            dimension_semantics=("parallel", "arbitrary"), vmem_limit_bytes=VMEM_LIMIT),
        name="proj",
    )(h, norm_w, w)


def _outproj_kernel(h_ref, yr_ref, yc_ref, yh_ref, wr_ref, wc_ref, wh_ref, o_ref):
    acc = _dot(yr_ref[...], wr_ref[...])
    acc += _dot(yc_ref[...], wc_ref[...])
    acc += _dot(yh_ref[...], wh_ref[...])
    o_ref[...] = h_ref[...] + acc


def _outproj(h, yr, yc, yh, wo, *, layer, tm=512):
    m, d = h.shape
    nr, ncv, nh = yr.shape[1], yc.shape[1], yh.shape[1]
    row = lambda i: (i, 0)
    return pl.pallas_call(
        _outproj_kernel,
        grid=(m // tm,),
        in_specs=[
            pl.BlockSpec((tm, d), row),
            pl.BlockSpec((tm, yr.shape[1]), row),
            pl.BlockSpec((tm, yc.shape[1]), row),
            pl.BlockSpec((tm, yh.shape[1]), row),
            pl.BlockSpec((None, nr, d), lambda i: (layer, 0, 0)),
            pl.BlockSpec((None, ncv, d), lambda i: (layer, nr // ncv, 0)),
            pl.BlockSpec((None, nh, d), lambda i: (layer, (nr + ncv) // nh, 0)),
        ],
        out_specs=pl.BlockSpec((tm, d), row),
        out_shape=jax.ShapeDtypeStruct((m, d), F32),
        compiler_params=pltpu.CompilerParams(
            dimension_semantics=("parallel",), vmem_limit_bytes=VMEM_LIMIT),
        name="outproj",
    )(h, yr, yc, yh, wo, wo, wo)


def _prev_rows(x, carry, k=1):
    rows = lax.broadcasted_iota(jnp.int32, (8, x.shape[1]), 0)
    prev = pltpu.roll(x, k, 0)
    head = jnp.where(rows < k, pltpu.roll(carry, k, 0), prev[:8])
    return jnp.concatenate([head, prev[8:]], axis=0)


def _seg_sum(x, ones_bd):
    return _dot(x.astype(BF16), ones_bd)


def _rwkv_prepare(lw, r, k2, v, av, bv, c, tick):
    n = 2 * RWKV_CHUNK
    m0, m1 = c["m0"], c["m1"]
    stack = lambda x: jnp.concatenate([x * m0, x * m1], axis=0)
    each = lambda f, *ls: [f(*xs) for xs in zip(*ls)]

    def cumsum(x):
        hi, lo = _split_bf16(x)
        gcat = _dot(c["tri"], jnp.concatenate([hi, lo], axis=1))
        return gcat[:, :LANES] + gcat[:, LANES:]

    g = each(cumsum, lw)
    tick()
    gl = [x[RWKV_CHUNK - 1:RWKV_CHUNK, :] for x in g]
    e_g = each(jnp.exp, g)
    e_ng = each(lambda x: jnp.exp(-x), g)
    e_d = each(lambda x, y: jnp.exp(x - y), gl, g)
    ats = each(lambda a_, g_, l_: stack(a_ * jnp.exp(g_ - l_)), av, g, lw)
    rts = each(lambda r_, e_: stack(r_ * e_), r, e_g)
    bts = each(lambda b_, e_: stack(b_ * e_), bv, e_ng)
    kts = each(lambda k_, e_: stack(k_ * e_), k2, e_ng)
    bes = each(lambda b_, e_: stack(b_ * e_).astype(BF16), bv, e_d)
    kes = each(lambda k_, e_: stack(k_ * e_).astype(BF16), k2, e_d)
    vs = each(stack, v)
    vsb = [x.astype(BF16) for x in vs]

    p = each(lambda a_, r_, b_, k_: _dot_nt(jnp.concatenate([a_, r_], axis=0).astype(BF16),
                                            jnp.concatenate([b_, k_], axis=0).astype(BF16)),
             ats, rts, bts, kts)
    a_ab = [jnp.where(c["strict"], x[:n, :n], 0.0) for x in p]
    a_kr = [jnp.concatenate([jnp.where(c["strict"], x[:n, n:], 0.0),
                             jnp.where(c["incl"], x[n:, n:], 0.0)], axis=0).astype(BF16) for x in p]
    a_rb = [jnp.where(c["incl"], x[n:, :n], 0.0).astype(BF16) for x in p]
    tick()

    h = RWKV_CHUNK
    side = lambda x: x[:h] + x[h:]
    diag = lambda x: jnp.concatenate([x, x], axis=0) * c["own"]
    a_sbs = [side(x) for x in a_ab]
    inv = [c["eye"] + x for x in a_sbs]
    q = [_dot(x.astype(BF16), y.astype(BF16)) for x, y in zip(a_sbs, a_ab)]
    tick()
    for _ in range(4):
        z = each(lambda i_, q_: _dot(jnp.concatenate([i_, q_], axis=0).astype(BF16), diag(q_).astype(BF16)), inv, q)
        inv = each(lambda i_, z_: i_ + z_[:h], inv, z)
        q = [x[h:] for x in z]
        tick()
    inv = each(lambda i_, q_: diag(i_ + _dot(i_.astype(BF16), diag(q_).astype(BF16))), inv, q)
    tick()

    akv_rkv = each(_dot, a_kr, vsb)
    tick()
    tt = each(lambda i_, a_, x_: _dot(i_.astype(BF16), jnp.concatenate([a_, x_[:n]], axis=1).astype(BF16)),
              inv, ats, akv_rkv)
    tick()
    rr = each(lambda a_, t_: _dot(a_, t_.astype(BF16)), a_rb, tt)
    rq = each(lambda r_, x_: (r_ + x_[:, :LANES]).astype(BF16), rts, rr)
    oc = each(lambda x_, y_: x_[:, LANES:] + y_[n:], rr, akv_rkv)
    bb = each(lambda t_, b_: _dot(t_.T.astype(BF16), b_), tt, bes)
    bta = [x[:LANES].astype(BF16) for x in bb]
    nct = each(lambda x_, v_, k_: x_[LANES:] + _dot(v_.T.astype(BF16), k_), bb, vs, kes)
    e_gl = each(jnp.exp, gl)
    return rq, oc, bta, nct, e_gl


def _rwkv_kernel(r_ref, k_ref, v_ref, low_ref, mur_ref, muk_ref, muv_ref, mul_ref,
                 ww_ref, wa_ref, wg_ref, w0_ref, a0_ref, kk_ref, ka_ref, rk_ref, lnw_ref, lnb_ref,
                 y_ref, cr_ref, ck_ref, cv_ref, cl_ref, st_ref, feat_ref,
                 rq_ref, oc_ref, bta_ref, nct_ref, egl_ref, bonus_ref, gate_ref, *, tb, nt):
    s = pl.program_id(0)
    nc = tb // RWKV_CHUNK

    @pl.when(s == 0)
    def _():
        for ref in (cr_ref, ck_ref, cv_ref, cl_ref, st_ref, feat_ref, rq_ref, oc_ref, bta_ref, nct_ref, egl_ref,
                    bonus_ref, gate_ref):
            ref[...] = jnp.zeros_like(ref)

    lane = lax.broadcasted_iota(jnp.int32, (1, LANES), 1)
    m0 = (lane < RWKV_HEAD_DIM).astype(F32)
    m1 = 1.0 - m0
    ri = lax.broadcasted_iota(jnp.int32, (LANES, LANES), 0)
    ci = lax.broadcasted_iota(jnp.int32, (LANES, LANES), 1)
    same = (ri < RWKV_HEAD_DIM) == (ci < RWKV_HEAD_DIM)
    ones_bd = same.astype(F32).astype(BF16)
    ti = lax.broadcasted_iota(jnp.int32, (RWKV_CHUNK, RWKV_CHUNK), 0)
    si = lax.broadcasted_iota(jnp.int32, (RWKV_CHUNK, RWKV_CHUNK), 1)
    tw = lax.broadcasted_iota(jnp.int32, (RWKV_CHUNK, LANES), 0)
    sw = lax.broadcasted_iota(jnp.int32, (RWKV_CHUNK, LANES), 1) & (RWKV_HEAD_DIM - 1)
    consts = dict(
        m0=m0, m1=m1, own=same.astype(F32),
        strict=same & (ri > ci), incl=same & (ri >= ci),
        eye=(tw == sw).astype(F32),
        tri=(ti >= si).astype(F32).astype(BF16),
    )

    rd = lax.rem(s, 2)
    rd3 = lax.rem(s + 1, 3)
    seq_start = lax.rem(jnp.maximum(s - 2, 0), nt) == 0
    state = [st_ref[...] * jnp.where(seq_start, 0.0, 1.0)]
    outs = []
    inv_n = 1.0 / RWKV_HEAD_DIM
    own = same.astype(F32)

    def recurrence_step():
        i = len(outs)
        if i == nc:
            return
        st = state[0]
        stb = st.astype(BF16)
        o_stack = _dot_nt(rq_ref[rd, i], stb) + oc_ref[rd, i]
        state[0] = st * egl_ref[rd, i][0:1, :] + _dot(stb, bta_ref[rd, i]) + nct_ref[rd, i]
        mean = jnp.sum(o_stack, axis=-1, keepdims=True) * inv_n
        cen = (o_stack - mean) * own
        var = jnp.sum(cen * cen, axis=-1, keepdims=True) * inv_n
        on = cen * lax.rsqrt(var + RWKV_GN_EPS)
        outs.append(on[:RWKV_CHUNK] + on[RWKV_CHUNK:])

    wf = lax.rem(s, 2)
    wf3 = lax.rem(s, 3)
    keep = jnp.where(lax.rem(jnp.minimum(s, pl.num_programs(0) - 3), nt) == 0, 0.0, 1.0)
    f = {}

    def lerp(x_ref, c_ref, mu_ref):
        x = x_ref[0].astype(F32)
        prev = _prev_rows(x, c_ref[...] * keep)
        c_ref[...] = x[tb - 8:tb, :]
        return x + (prev - x) * mu_ref[...]

    def feat_low():
        low = lerp(low_ref, cl_ref, mul_ref)
        f["wa_in"] = low[:, :LANES]
        gate_ref[wf3] = _dot(jax.nn.sigmoid(low[:, LANES:]).astype(BF16), wg_ref[...])

    def feat_decay():
        xw = w0_ref[...] + _dot(jnp.tanh(f["wa_in"]).astype(BF16), ww_ref[...])
        feat_ref[wf, 0] = -jnp.exp(jnp.float32(-0.5)) * jax.nn.sigmoid(xw)
        f["a"] = jax.nn.sigmoid(a0_ref[...] + _dot(f["wa_in"].astype(BF16), wa_ref[...]))

    def feat_k():
        k = lerp(k_ref, ck_ref, muk_ref)
        kk = k * kk_ref[...]
        kk = kk * lax.rsqrt(jnp.maximum(_seg_sum(kk * kk, ones_bd), 1e-24))
        f["k2"] = k * (1.0 + (f["a"] - 1.0) * ka_ref[...])
        feat_ref[wf, 2] = f["k2"]
        feat_ref[wf, 4] = -kk
        feat_ref[wf, 5] = kk * f["a"]

    def feat_rv():
        r = lerp(r_ref, cr_ref, mur_ref)
        v = lerp(v_ref, cv_ref, muv_ref)
        feat_ref[wf, 1] = r
        feat_ref[wf, 3] = v
        bonus_ref[wf3] = _seg_sum(r * f["k2"] * rk_ref[...], ones_bd) * v

    pieces = [feat_low, feat_decay, feat_k, feat_rv]

    def tick():
        if pieces:
            pieces.pop(0)()
        recurrence_step()

    rf = lax.rem(s + 1, 2)
    wr = lax.rem(s + 1, 2)
    chunks = lambda j: [feat_ref[rf, j, i * RWKV_CHUNK:(i + 1) * RWKV_CHUNK, :] for i in range(nc)]
    rq, oc, bta, nct, e_gl = _rwkv_prepare(chunks(0), chunks(1), chunks(2), chunks(3), chunks(4), chunks(5),
                                           consts, tick)
    while pieces or len(outs) < nc:
        tick()

    st_ref[...] = state[0]
    o = jnp.concatenate(outs, axis=0) * lnw_ref[...] + lnb_ref[...]
    y_ref[0] = ((o + bonus_ref[rd3]) * gate_ref[rd3]).astype(y_ref.dtype)

    for i in range(nc):
        rq_ref[wr, i] = rq[i]
        oc_ref[wr, i] = oc[i]
        bta_ref[wr, i] = bta[i]
        nct_ref[wr, i] = nct[i]
        egl_ref[wr, i] = jnp.broadcast_to(e_gl[i], (8, LANES))


def _rwkv(proj3, mu, ww, wa, wg, w0, a0, k_k, k_a, r_k, ln_w, ln_b, *, tb=512):
    b, t, _ = proj3.shape
    npairs = RWKV_WIDTH // LANES
    nt = t // tb
    nc = tb // RWKV_CHUNK
    nblocks = b * npairs * nt
    off_k = RWKV_WIDTH // LANES
    off_v = 2 * RWKV_WIDTH // LANES
    off_low = 3 * RWKV_WIDTH // RWKV_LOW

    def where(item):
        return item // (nt * npairs), item % nt, (item // nt) % npairs

    cur = lambda s: where(jnp.minimum(s, nblocks - 1))
    lag = lambda s: where(jnp.maximum(s - 2, 0))
    act = lambda off: pl.BlockSpec((1, tb, LANES), lambda s: (cur(s)[0], cur(s)[1], off + cur(s)[2]))
    vec = lambda off: pl.BlockSpec((1, LANES), lambda s: (0, off + cur(s)[2]))
    lagvec = pl.BlockSpec((1, LANES), lambda s: (0, lag(s)[2]))
    wspec = pl.BlockSpec((LANES, LANES), lambda s: (0, cur(s)[2]))
    return pl.pallas_call(
        functools.partial(_rwkv_kernel, tb=tb, nt=nt),
        grid=(nblocks + 2,),
        in_specs=[
            act(0), act(off_k), act(off_v),
            pl.BlockSpec((1, tb, RWKV_LOW), lambda s: (cur(s)[0], cur(s)[1], off_low)),
            vec(0), vec(off_k), vec(off_v),
            pl.BlockSpec((1, RWKV_LOW), lambda s: (0, off_low)),
            wspec, wspec, wspec,
            vec(0), vec(0), vec(0), vec(0), vec(0), lagvec, lagvec,
        ],
        out_specs=pl.BlockSpec((1, tb, LANES), lambda s: lag(s)),
        out_shape=jax.ShapeDtypeStruct((b, t, RWKV_WIDTH), BF16),
        scratch_shapes=[pltpu.VMEM((8, LANES), F32), pltpu.VMEM((8, LANES), F32), pltpu.VMEM((8, LANES), F32),
                        pltpu.VMEM((8, RWKV_LOW), F32), pltpu.VMEM((LANES, LANES), F32),
                        pltpu.VMEM((2, 6, tb, LANES), F32),
                        pltpu.VMEM((2, nc, LANES, LANES), BF16), pltpu.VMEM((2, nc, LANES, LANES), F32),
                        pltpu.VMEM((2, nc, LANES, LANES), BF16), pltpu.VMEM((2, nc, LANES, LANES), F32),
                        pltpu.VMEM((2, nc, 8, LANES), F32),
                        pltpu.VMEM((3, tb, LANES), F32), pltpu.VMEM((3, tb, LANES), F32)],
        compiler_params=pltpu.CompilerParams(
            dimension_semantics=("arbitrary",), vmem_limit_bytes=VMEM_LIMIT),
        name="rwkv7",
    )(proj3, proj3, proj3, proj3, mu, mu, mu, mu, ww, wa, wg, w0, a0, k_k, k_a, r_k, ln_w, ln_b)


def _conv_kernel(c_ref, x_ref, b_ref, w_ref, y_ref, carry_ref):
    @pl.when(pl.program_id(2) == 0)
    def _():
        carry_ref[...] = jnp.zeros_like(carry_ref)

    z = c_ref[0].astype(F32) * x_ref[0].astype(F32)
    tb = z.shape[0]
    w = w_ref[...]
    y = z * w[CONV_K - 1:CONV_K, :]
    for j in range(1, CONV_K):
        y = y + _prev_rows(z, carry_ref[...], j) * w[CONV_K - 1 - j:CONV_K - j, :]
    carry_ref[...] = z[tb - 8:tb, :]
    y_ref[0] = (b_ref[0].astype(F32) * y).astype(y_ref.dtype)


def _conv(proj3, conv_w, *, tb=1024, cw=2 * LANES):
    b, t, _ = proj3.shape
    tb = min(tb, t)
    nblk = CONV_WIDTH // cw
    off = RWKV_SHIFT_WIDTH // cw
    spec = lambda o: pl.BlockSpec((1, tb, cw), lambda bi, p, ti: (bi, ti, o + p))
    return pl.pallas_call(
        _conv_kernel,
        grid=(b, nblk, t // tb),
        in_specs=[spec(off), spec(off + nblk), spec(off + 2 * nblk),
                  pl.BlockSpec((CONV_K, cw), lambda bi, p, ti: (0, p))],
        out_specs=pl.BlockSpec((1, tb, cw), lambda bi, p, ti: (bi, ti, p)),
        out_shape=jax.ShapeDtypeStruct((b, t, CONV_WIDTH), BF16),
        scratch_shapes=[pltpu.VMEM((8, cw), F32)],
        compiler_params=pltpu.CompilerParams(
            dimension_semantics=("parallel", "parallel", "arbitrary"), vmem_limit_bytes=VMEM_LIMIT),
        name="shortconv",
    )(proj3, proj3, proj3, conv_w)


def _hgrn_kernel(q_ref, f_ref, i_ref, og_ref, lb_ref, nw_ref, y_ref, st_ref, *, layer, tb):
    @pl.when(pl.program_id(2) == 0)
    def _():
        st_ref[...] = jnp.zeros_like(st_ref)

    lbp = lb_ref[...]
    e = jnp.exp(lbp - jnp.max(lbp, axis=0, keepdims=True))
    pr = e / jnp.sum(e, axis=0, keepdims=True)
    lb = jnp.zeros((1, LANES), F32)
    for l in range(1, layer + 1):
        lb = lb + pr[l:l + 1, :]

    qr = q_ref[0].astype(F32)
    q = qr * jax.nn.sigmoid(qr)
    f = f_ref[0].astype(F32)
    forget = lb + (1.0 - lb) * jax.nn.sigmoid(f)
    logf = jnp.log(jnp.maximum(forget, HGRN_MIN_FORGET))
    kx = (1.0 - lb) * jax.nn.sigmoid(-f)
    iv = i_ref[0].astype(F32)

    n = HGRN_BLOCK
    nb = tb // n
    ti = lax.broadcasted_iota(jnp.int32, (tb, tb), 0)
    si = lax.broadcasted_iota(jnp.int32, (tb, tb), 1)
    same = (ti ^ si) < n
    sums = jnp.concatenate([(same & (ti >= si)).astype(F32).astype(BF16), same.astype(F32).astype(BF16)], axis=0)
    hi, lo = _split_bf16(logf)
    gg = _dot(sums, jnp.concatenate([hi, lo], axis=1))
    g = gg[:tb, :LANES] + gg[:tb, LANES:]
    gl = gg[tb:, :LANES] + gg[tb:, LANES:]
    qd = (q * jnp.exp(g)).astype(BF16)
    kd = (kx * jnp.exp(gl - g)).astype(BF16)
    dec = jnp.exp(gl)
    trow = lax.broadcasted_iota(jnp.int32, (n, 1), 0)
    blocks = lambda x: [x[i * n:(i + 1) * n] for i in range(nb)]
    gb, qb, kb, ib, qdb, kdb = (blocks(x) for x in (g, q, kx, iv, qd, kd))

    upd = [_dot(i_.T.astype(BF16), k_) for i_, k_ in zip(ib, kdb)]
    st = st_ref[...]
    sts = []
    for i in range(nb):
        sts.append(st.astype(BF16))
        st = st * dec[i * n:i * n + 1, :] + upd[i]
    st_ref[...] = st
    outs = [_dot_nt(q_, s_) for q_, s_ in zip(qdb, sts)]
    for s in range(n):
        for i in range(nb):
            w = jnp.sum(qb[i] * kb[i][s:s + 1, :] * jnp.exp(gb[i] - gb[i][s:s + 1, :]), axis=-1, keepdims=True)
            outs[i] = outs[i] + jnp.where(trow >= s, w, 0.0) * ib[i][s:s + 1, :]
    o = jnp.concatenate(outs, axis=0)
    o = o * lax.rsqrt(jnp.mean(o * o, axis=-1, keepdims=True) + HGRN_NORM_EPS) * nw_ref[...]
    og = og_ref[0].astype(F32)
    y_ref[0] = (o * (og * jax.nn.sigmoid(og))).astype(y_ref.dtype)


def _hgrn(proj3, hgrn_lb, norm_w, *, layer, tb=512):
    b, t, _ = proj3.shape
    tb = min(tb, t)
    nh = HGRN_WIDTH // HGRN_HEAD_DIM
    off = (RWKV_SHIFT_WIDTH + 3 * CONV_WIDTH) // LANES
    spec = lambda o: pl.BlockSpec((1, tb, LANES), lambda bi, p, ti: (bi, ti, o + p))
    depth = hgrn_lb.shape[0]
    return pl.pallas_call(
        functools.partial(_hgrn_kernel, layer=layer, tb=tb),
        grid=(b, nh, t // tb),
        in_specs=[spec(off), spec(off + nh), spec(off + 2 * nh), spec(off + 3 * nh),
                  pl.BlockSpec((depth, LANES), lambda bi, p, ti: (0, p)),
                  pl.BlockSpec((1, LANES), lambda bi, p, ti: (0, p))],
        out_specs=pl.BlockSpec((1, tb, LANES), lambda bi, p, ti: (bi, ti, p)),
        out_shape=jax.ShapeDtypeStruct((b, t, HGRN_WIDTH), BF16),
        scratch_shapes=[pltpu.VMEM((LANES, LANES), F32)],
        compiler_params=pltpu.CompilerParams(
            dimension_semantics=("parallel", "parallel", "arbitrary"), vmem_limit_bytes=VMEM_LIMIT),
        name="hgrn2",
    )(proj3, proj3, proj3, proj3, hgrn_lb, norm_w)


def _cast_kernel(x_ref, o_ref, *, valid_rows, rows):
    x = x_ref[...]
    if valid_rows % rows:
        r = pl.program_id(1) * rows + lax.broadcasted_iota(jnp.int32, x.shape, 1)
        x = jnp.where(r < valid_rows, x, 0.0)
    o_ref[...] = x.astype(o_ref.dtype)


def _cast_rows(w, out_rows, *, rows=256):
    nl, r, c = w.shape
    return pl.pallas_call(
        functools.partial(_cast_kernel, valid_rows=r, rows=rows),
        grid=(nl, out_rows // rows),
        in_specs=[pl.BlockSpec((1, rows, c), lambda l, i: (l, i, 0))],
        out_specs=pl.BlockSpec((1, rows, c), lambda l, i: (l, i, 0)),
        out_shape=jax.ShapeDtypeStruct((nl, out_rows, c), BF16),
        compiler_params=pltpu.CompilerParams(
            dimension_semantics=("parallel", "parallel"), vmem_limit_bytes=VMEM_LIMIT),
        name="cast_rows",
    )(w)


def _split_gu_kernel(x_ref, g_ref, u_ref, *, d_ff):
    x = x_ref[0]
    pad = jnp.zeros((x.shape[0], g_ref.shape[2] - d_ff), g_ref.dtype)
    g_ref[0] = jnp.concatenate([x[:, :d_ff].astype(g_ref.dtype), pad], axis=1)
    u_ref[0] = jnp.concatenate([x[:, d_ff:].astype(u_ref.dtype), pad], axis=1)


def _split_gu(w_gu, d_ff, ffp, *, rows=256):
    nl, d, _ = w_gu.shape
    out = jax.ShapeDtypeStruct((nl, d, ffp), BF16)
    spec = pl.BlockSpec((1, rows, ffp), lambda l, i: (l, i, 0))
    return pl.pallas_call(
        functools.partial(_split_gu_kernel, d_ff=d_ff),
        grid=(nl, d // rows),
        in_specs=[pl.BlockSpec((1, rows, 2 * d_ff), lambda l, i: (l, i, 0))],
        out_specs=(spec, spec),
        out_shape=(out, out),
        compiler_params=pltpu.CompilerParams(
            dimension_semantics=("parallel", "parallel"), vmem_limit_bytes=VMEM_LIMIT),
        name="split_gate_up",
    )(w_gu)


def kernel(x, ffn1_norm, ffn1_w_gu, ffn1_w_down, mix_norm, w_in, rwkv_mu, rwkv_w_up, rwkv_w0,
           rwkv_a_up, rwkv_a0, rwkv_g_up, rwkv_k_k, rwkv_k_a, rwkv_r_k, rwkv_ln_w, rwkv_ln_b,
           conv_w, hgrn_lb, hgrn_norm, w_out, ffn2_norm, ffn2_w_gu, ffn2_w_down, final_norm):
    b, t, d = x.shape
    depth = ffn1_norm.shape[0]
    d_ff = ffn1_w_down.shape[1]
    ffp = -(-d_ff // 512) * 512
    m = b * t
    h = x.reshape(m, d)
    row = lambda v: v.reshape(1, -1)
    final_w = row(final_norm)

    wg1, wu1 = _split_gu(ffn1_w_gu, d_ff, ffp)
    wg2, wu2 = _split_gu(ffn2_w_gu, d_ff, ffp)
    wd1 = _cast_rows(ffn1_w_down, ffp)
    wd2 = _cast_rows(ffn2_w_down, ffp)
    w_in_b = _cast_rows(w_in, d)
    w_out_b = _cast_rows(w_out, d)
    zeros_low = jnp.zeros((depth, LANES - DECAY_RANK, RWKV_WIDTH), BF16)
    ww = jnp.concatenate([rwkv_w_up.astype(BF16), zeros_low], axis=1)
    wa = jnp.concatenate([zeros_low, rwkv_a_up.astype(BF16)], axis=1)
    wgate = rwkv_g_up.astype(BF16)

    for l in range(depth):
        h = _ffn(h, row(ffn1_norm[l]), wg1, wu1, wd1, final_w, layer=l, final=False)

        proj = _proj(h, row(mix_norm[l]), w_in_b, layer=l)
        proj3 = proj.reshape(b, t, -1)
        y_r = _rwkv(proj3, row(rwkv_mu[l]), ww[l], wa[l], wgate[l],
                    row(rwkv_w0[l]), row(rwkv_a0[l]), row(rwkv_k_k[l]), row(rwkv_k_a[l]),
                    row(rwkv_r_k[l]), row(rwkv_ln_w[l]), row(rwkv_ln_b[l]))
        y_c = _conv(proj3, conv_w[l])
        y_h = _hgrn(proj3, hgrn_lb, row(hgrn_norm[l]), layer=l)
        h = _outproj(h, y_r.reshape(m, -1), y_c.reshape(m, -1), y_h.reshape(m, -1), w_out_b, layer=l)

        h = _ffn(h, row(ffn2_norm[l]), wg2, wu2, wd2, final_w, layer=l, final=(l == depth - 1))
    return h.reshape(b, t, d)
```

```python
import functools

import jax
import jax.numpy as jnp
from jax import lax
from jax.experimental import pallas as pl
from jax.experimental.pallas import tpu as pltpu

F32 = jnp.float32
BF16 = jnp.bfloat16

RWKV_HEAD_DIM = 64
RWKV_WIDTH = 1024
DECAY_RANK = 64
AAA_RANK = 64
GATE_RANK = 128
RWKV_LOW = DECAY_RANK + AAA_RANK + GATE_RANK
RWKV_SHIFT_WIDTH = 3 * RWKV_WIDTH + RWKV_LOW
RWKV_GN_EPS = 64e-5
CONV_WIDTH = 512
CONV_K = 3
HGRN_HEAD_DIM = 128
HGRN_WIDTH = 512
HGRN_NORM_EPS = 1e-5
HGRN_MIN_FORGET = 1e-30
FFN_RESIDUAL_SCALE = 0.5
RMS_EPS = 1e-6

LANES = 128
SUBLANES = 8
RWKV_CHUNK = 64
HGRN_BLOCK = 16
VMEM_LIMIT = 56 * 1024 * 1024


def _dot(a, b):
    return jnp.dot(a, b, preferred_element_type=F32)


def _dot_nt(a, b):
    return lax.dot_general(a, b, (((1,), (1,)), ((), ())), preferred_element_type=F32)


def _rms(x, w):
    return x * lax.rsqrt(jnp.mean(x * x, axis=-1, keepdims=True) + RMS_EPS) * w


def _split_bf16(x):
    hi = x.astype(BF16)
    lo = (x - hi.astype(F32)).astype(BF16)
    return hi, lo


def _ffn_kernel(x_ref, nw_ref, wg_ref, wu_ref, wd_ref, fw_ref, o_ref, xn_ref, *, final):
    j = pl.program_id(1)

    def partial_sum():
        xn = xn_ref[...]
        g = _dot(xn, wg_ref[...])
        u = _dot(xn, wu_ref[...])
        a = (g * jax.nn.sigmoid(g) * u).astype(BF16)
        return _dot(a, wd_ref[...])

    @pl.when(j == 0)
    def _():
        xn_ref[...] = _rms(x_ref[...], nw_ref[...]).astype(BF16)
        o_ref[...] = partial_sum()

    last = pl.num_programs(1) - 1

    @pl.when((j > 0) & (j < last))
    def _():
        o_ref[...] += partial_sum()

    @pl.when(j == last)
    def _():
        h = x_ref[...] + FFN_RESIDUAL_SCALE * (o_ref[...] + partial_sum())
        if final:
            h = _rms(h, fw_ref[...])
        o_ref[...] = h


def _ffn(h, norm_w, wg, wu, wd, final_w, *, layer, final, tm=1024, tf=512):
    m, d = h.shape
    ffp = wg.shape[2]
    grid = (m // tm, ffp // tf)
    assert grid[1] >= 2
    return pl.pallas_call(
        functools.partial(_ffn_kernel, final=final),
        grid=grid,
        in_specs=[
            pl.BlockSpec((tm, d), lambda i, j: (i, 0)),
            pl.BlockSpec((1, d), lambda i, j: (0, 0)),
            pl.BlockSpec((None, d, tf), lambda i, j: (layer, 0, j)),
            pl.BlockSpec((None, d, tf), lambda i, j: (layer, 0, j)),
            pl.BlockSpec((None, tf, d), lambda i, j: (layer, j, 0)),
            pl.BlockSpec((1, d), lambda i, j: (0, 0)),
        ],
        out_specs=pl.BlockSpec((tm, d), lambda i, j: (i, 0)),
        out_shape=jax.ShapeDtypeStruct((m, d), F32),
        scratch_shapes=[pltpu.VMEM((tm, d), BF16)],
        compiler_params=pltpu.CompilerParams(
            dimension_semantics=("parallel", "arbitrary"), vmem_limit_bytes=VMEM_LIMIT),
        name="ffn",
    )(h, norm_w, wg, wu, wd, final_w)


def _proj_kernel(x_ref, nw_ref, w_ref, o_ref, xn_ref):
    @pl.when(pl.program_id(1) == 0)
    def _():
        xn_ref[...] = _rms(x_ref[...], nw_ref[...]).astype(BF16)
        o_ref[...] = _dot(xn_ref[...], w_ref[...]).astype(o_ref.dtype)

    @pl.when(pl.program_id(1) > 0)
    def _():
        o_ref[...] = _dot(xn_ref[...], w_ref[...]).astype(o_ref.dtype)


def _proj(h, norm_w, w, *, layer, tm=1024, tn=2304):
    m, d = h.shape
    n = w.shape[2]
    return pl.pallas_call(
        _proj_kernel,
        grid=(m // tm, n // tn),
        in_specs=[
            pl.BlockSpec((tm, d), lambda i, j: (i, 0)),
            pl.BlockSpec((1, d), lambda i, j: (0, 0)),
            pl.BlockSpec((None, d, tn), lambda i, j: (layer, 0, j)),
        ],
        out_specs=pl.BlockSpec((tm, tn), lambda i, j: (i, j)),
        out_shape=jax.ShapeDtypeStruct((m, n), BF16),
        scratch_shapes=[pltpu.VMEM((tm, d), BF16)],
        compiler_params=pltpu.CompilerParams(
            dimension_semantics=("parallel", "arbitrary"), vmem_limit_bytes=VMEM_LIMIT),
        name="proj",
    )(h, norm_w, w)


def _outproj_kernel(h_ref, yr_ref, yc_ref, yh_ref, wr_ref, wc_ref, wh_ref, o_ref):
    acc = _dot(yr_ref[...], wr_ref[...])
    acc += _dot(yc_ref[...], wc_ref[...])
    acc += _dot(yh_ref[...], wh_ref[...])
    o_ref[...] = h_ref[...] + acc


def _outproj(h, yr, yc, yh, wo, *, layer, tm=512):
    m, d = h.shape
    nr, ncv, nh = yr.shape[1], yc.shape[1], yh.shape[1]
    row = lambda i: (i, 0)
    return pl.pallas_call(
        _outproj_kernel,
        grid=(m // tm,),
        in_specs=[
            pl.BlockSpec((tm, d), row),
            pl.BlockSpec((tm, yr.shape[1]), row),
            pl.BlockSpec((tm, yc.shape[1]), row),
            pl.BlockSpec((tm, yh.shape[1]), row),
            pl.BlockSpec((None, nr, d), lambda i: (layer, 0, 0)),
            pl.BlockSpec((None, ncv, d), lambda i: (layer, nr // ncv, 0)),
            pl.BlockSpec((None, nh, d), lambda i: (layer, (nr + ncv) // nh, 0)),
        ],
        out_specs=pl.BlockSpec((tm, d), row),
        out_shape=jax.ShapeDtypeStruct((m, d), F32),
        compiler_params=pltpu.CompilerParams(
            dimension_semantics=("parallel",), vmem_limit_bytes=VMEM_LIMIT),
        name="outproj",
    )(h, yr, yc, yh, wo, wo, wo)


def _prev_rows(x, carry, k=1):
    rows = lax.broadcasted_iota(jnp.int32, (SUBLANES, x.shape[1]), 0)
    prev = pltpu.roll(x, k, 0)
    head = jnp.where(rows < k, pltpu.roll(carry, k, 0), prev[:SUBLANES])
    return jnp.concatenate([head, prev[SUBLANES:]], axis=0)


def _seg_sum(x, ones_bd):
    return _dot(x.astype(BF16), ones_bd)


def _rwkv_prepare(lw, r, k2, v, av, bv, c, tick):
    n = 2 * RWKV_CHUNK
    m0, m1 = c["m0"], c["m1"]
    stack = lambda x: jnp.concatenate([x * m0, x * m1], axis=0)
    each = lambda f, *ls: [f(*xs) for xs in zip(*ls)]

    def cumsum(x):
        hi, lo = _split_bf16(x)
        gcat = _dot(c["tri"], jnp.concatenate([hi, lo], axis=1))
        return gcat[:, :LANES] + gcat[:, LANES:]

    g = each(cumsum, lw)
    tick()
    gl = [x[RWKV_CHUNK - 1:RWKV_CHUNK, :] for x in g]
    e_g = each(jnp.exp, g)
    e_ng = each(lambda x: jnp.exp(-x), g)
    e_d = each(lambda x, y: jnp.exp(x - y), gl, g)
    ats = each(lambda a_, g_, l_: stack(a_ * jnp.exp(g_ - l_)), av, g, lw)
    rts = each(lambda r_, e_: stack(r_ * e_), r, e_g)
    bts = each(lambda b_, e_: stack(b_ * e_), bv, e_ng)
    kts = each(lambda k_, e_: stack(k_ * e_), k2, e_ng)
    bes = each(lambda b_, e_: stack(b_ * e_).astype(BF16), bv, e_d)
    kes = each(lambda k_, e_: stack(k_ * e_).astype(BF16), k2, e_d)
    vs = each(stack, v)
    vsb = [x.astype(BF16) for x in vs]

    p = each(lambda a_, r_, b_, k_: _dot_nt(jnp.concatenate([a_, r_], axis=0).astype(BF16),
                                            jnp.concatenate([b_, k_], axis=0).astype(BF16)),
             ats, rts, bts, kts)
    a_ab = [jnp.where(c["strict"], x[:n, :n], 0.0) for x in p]
    a_kr = [jnp.concatenate([jnp.where(c["strict"], x[:n, n:], 0.0),
                             jnp.where(c["incl"], x[n:, n:], 0.0)], axis=0).astype(BF16) for x in p]
    a_rb = [jnp.where(c["incl"], x[n:, :n], 0.0).astype(BF16) for x in p]
    tick()

    h = RWKV_CHUNK
    side = lambda x: x[:h] + x[h:]
    diag = lambda x: jnp.concatenate([x, x], axis=0) * c["own"]
    a_sbs = [side(x) for x in a_ab]
    inv = [c["eye"] + x for x in a_sbs]
    q = [_dot(x.astype(BF16), y.astype(BF16)) for x, y in zip(a_sbs, a_ab)]
    tick()
    for _ in range(4):
        z = each(lambda i_, q_: _dot(jnp.concatenate([i_, q_], axis=0).astype(BF16), diag(q_).astype(BF16)), inv, q)
        inv = each(lambda i_, z_: i_ + z_[:h], inv, z)
        q = [x[h:] for x in z]
        tick()
    inv = each(lambda i_, q_: diag(i_ + _dot(i_.astype(BF16), diag(q_).astype(BF16))), inv, q)
    tick()

    akv_rkv = each(_dot, a_kr, vsb)
    tick()
    tt = each(lambda i_, a_, x_: _dot(i_.astype(BF16), jnp.concatenate([a_, x_[:n]], axis=1).astype(BF16)),
              inv, ats, akv_rkv)
    tick()
    rr = each(lambda a_, t_: _dot(a_, t_.astype(BF16)), a_rb, tt)
    rq = each(lambda r_, x_: (r_ + x_[:, :LANES]).astype(BF16), rts, rr)
    oc = each(lambda x_, y_: x_[:, LANES:] + y_[n:], rr, akv_rkv)
    bb = each(lambda t_, b_: _dot(t_.T.astype(BF16), b_), tt, bes)
    bta = [x[:LANES].astype(BF16) for x in bb]
    nct = each(lambda x_, v_, k_: x_[LANES:] + _dot(v_.T.astype(BF16), k_), bb, vs, kes)
    e_gl = each(jnp.exp, gl)
    return rq, oc, bta, nct, e_gl


def _rwkv_kernel(r_ref, k_ref, v_ref, low_ref, mur_ref, muk_ref, muv_ref, mul_ref,
                 ww_ref, wa_ref, wg_ref, w0_ref, a0_ref, kk_ref, ka_ref, rk_ref, lnw_ref, lnb_ref,
                 y_ref, cr_ref, ck_ref, cv_ref, cl_ref, st_ref, feat_ref,
                 rq_ref, oc_ref, bta_ref, nct_ref, egl_ref, bonus_ref, gate_ref, *, tb, nt):
    s = pl.program_id(0)
    nc = tb // RWKV_CHUNK

    @pl.when(s == 0)
    def _():
        for ref in (cr_ref, ck_ref, cv_ref, cl_ref, st_ref, feat_ref, rq_ref, oc_ref, bta_ref, nct_ref, egl_ref,
                    bonus_ref, gate_ref):
            ref[...] = jnp.zeros_like(ref)

    lane = lax.broadcasted_iota(jnp.int32, (1, LANES), 1)
    m0 = (lane < RWKV_HEAD_DIM).astype(F32)
    m1 = 1.0 - m0
    ri = lax.broadcasted_iota(jnp.int32, (LANES, LANES), 0)
    ci = lax.broadcasted_iota(jnp.int32, (LANES, LANES), 1)
    same = (ri < RWKV_HEAD_DIM) == (ci < RWKV_HEAD_DIM)
    ones_bd = same.astype(F32).astype(BF16)
    ti = lax.broadcasted_iota(jnp.int32, (RWKV_CHUNK, RWKV_CHUNK), 0)
    si = lax.broadcasted_iota(jnp.int32, (RWKV_CHUNK, RWKV_CHUNK), 1)
    tw = lax.broadcasted_iota(jnp.int32, (RWKV_CHUNK, LANES), 0)
    sw = lax.broadcasted_iota(jnp.int32, (RWKV_CHUNK, LANES), 1) & (RWKV_HEAD_DIM - 1)
    consts = dict(
        m0=m0, m1=m1, own=same.astype(F32),
        strict=same & (ri > ci), incl=same & (ri >= ci),
        eye=(tw == sw).astype(F32),
        tri=(ti >= si).astype(F32).astype(BF16),
    )

    rd = lax.rem(s, 2)
    rd3 = lax.rem(s + 1, 3)
    seq_start = lax.rem(jnp.maximum(s - 2, 0), nt) == 0
    state = [st_ref[...] * jnp.where(seq_start, 0.0, 1.0)]
    outs = []
    inv_n = 1.0 / RWKV_HEAD_DIM
    own = same.astype(F32)

    def recurrence_step():
        i = len(outs)
        if i == nc:
            return
        st = state[0]
        stb = st.astype(BF16)
        o_stack = _dot_nt(rq_ref[rd, i], stb) + oc_ref[rd, i]
        state[0] = st * egl_ref[rd, i][0:1, :] + _dot(stb, bta_ref[rd, i]) + nct_ref[rd, i]
        mean = jnp.sum(o_stack, axis=-1, keepdims=True) * inv_n
        cen = (o_stack - mean) * own
        var = jnp.sum(cen * cen, axis=-1, keepdims=True) * inv_n
        on = cen * lax.rsqrt(var + RWKV_GN_EPS)
        outs.append(on[:RWKV_CHUNK] + on[RWKV_CHUNK:])

    wf = lax.rem(s, 2)
    wf3 = lax.rem(s, 3)
    keep = jnp.where(lax.rem(jnp.minimum(s, pl.num_programs(0) - 3), nt) == 0, 0.0, 1.0)
    f = {}

    def lerp(x_ref, c_ref, mu_ref):
        x = x_ref[0].astype(F32)
        prev = _prev_rows(x, c_ref[...] * keep)
        c_ref[...] = x[tb - SUBLANES:tb, :]
        return x + (prev - x) * mu_ref[...]

    def feat_low():
        low = lerp(low_ref, cl_ref, mul_ref)
        f["wa_in"] = low[:, :LANES]
        gate_ref[wf3] = _dot(jax.nn.sigmoid(low[:, LANES:]).astype(BF16), wg_ref[...])

    def feat_decay():
        xw = w0_ref[...] + _dot(jnp.tanh(f["wa_in"]).astype(BF16), ww_ref[...])
        feat_ref[wf, 0] = -jnp.exp(jnp.float32(-0.5)) * jax.nn.sigmoid(xw)
        f["a"] = jax.nn.sigmoid(a0_ref[...] + _dot(f["wa_in"].astype(BF16), wa_ref[...]))

    def feat_k():
        k = lerp(k_ref, ck_ref, muk_ref)
        kk = k * kk_ref[...]
        kk = kk * lax.rsqrt(jnp.maximum(_seg_sum(kk * kk, ones_bd), 1e-24))
        f["k2"] = k * (1.0 + (f["a"] - 1.0) * ka_ref[...])
        feat_ref[wf, 2] = f["k2"]
        feat_ref[wf, 4] = -kk
        feat_ref[wf, 5] = kk * f["a"]

    def feat_rv():
        r = lerp(r_ref, cr_ref, mur_ref)
        v = lerp(v_ref, cv_ref, muv_ref)
        feat_ref[wf, 1] = r
        feat_ref[wf, 3] = v
        bonus_ref[wf3] = _seg_sum(r * f["k2"] * rk_ref[...], ones_bd) * v

    pieces = [feat_low, feat_decay, feat_k, feat_rv]

    def tick():
        if pieces:
            pieces.pop(0)()
        recurrence_step()

    rf = lax.rem(s + 1, 2)
    wr = lax.rem(s + 1, 2)
    chunks = lambda j: [feat_ref[rf, j, i * RWKV_CHUNK:(i + 1) * RWKV_CHUNK, :] for i in range(nc)]
    rq, oc, bta, nct, e_gl = _rwkv_prepare(chunks(0), chunks(1), chunks(2), chunks(3), chunks(4), chunks(5),
                                           consts, tick)
    while pieces or len(outs) < nc:
        tick()

    st_ref[...] = state[0]
    o = jnp.concatenate(outs, axis=0) * lnw_ref[...] + lnb_ref[...]
    y_ref[0] = ((o + bonus_ref[rd3]) * gate_ref[rd3]).astype(y_ref.dtype)

    for i in range(nc):
        rq_ref[wr, i] = rq[i]
        oc_ref[wr, i] = oc[i]
        bta_ref[wr, i] = bta[i]
        nct_ref[wr, i] = nct[i]
        egl_ref[wr, i] = jnp.broadcast_to(e_gl[i], (SUBLANES, LANES))


def _rwkv(proj3, mu, ww, wa, wg, w0, a0, k_k, k_a, r_k, ln_w, ln_b, *, tb=512):
    b, t, _ = proj3.shape
    npairs = RWKV_WIDTH // LANES
    nt = t // tb
    nc = tb // RWKV_CHUNK
    nblocks = b * npairs * nt
    off_k = RWKV_WIDTH // LANES
    off_v = 2 * RWKV_WIDTH // LANES
    off_low = 3 * RWKV_WIDTH // RWKV_LOW

    def where(item):
        return item // (nt * npairs), item % nt, (item // nt) % npairs

    cur = lambda s: where(jnp.minimum(s, nblocks - 1))
    lag = lambda s: where(jnp.maximum(s - 2, 0))
    act = lambda off: pl.BlockSpec((1, tb, LANES), lambda s: (cur(s)[0], cur(s)[1], off + cur(s)[2]))
    vec = lambda off: pl.BlockSpec((1, LANES), lambda s: (0, off + cur(s)[2]))
    lagvec = pl.BlockSpec((1, LANES), lambda s: (0, lag(s)[2]))
    wspec = pl.BlockSpec((LANES, LANES), lambda s: (0, cur(s)[2]))
    return pl.pallas_call(
        functools.partial(_rwkv_kernel, tb=tb, nt=nt),
        grid=(nblocks + 2,),
        in_specs=[
            act(0), act(off_k), act(off_v),
            pl.BlockSpec((1, tb, RWKV_LOW), lambda s: (cur(s)[0], cur(s)[1], off_low)),
            vec(0), vec(off_k), vec(off_v),
            pl.BlockSpec((1, RWKV_LOW), lambda s: (0, off_low)),
            wspec, wspec, wspec,
            vec(0), vec(0), vec(0), vec(0), vec(0), lagvec, lagvec,
        ],
        out_specs=pl.BlockSpec((1, tb, LANES), lambda s: lag(s)),
        out_shape=jax.ShapeDtypeStruct((b, t, RWKV_WIDTH), BF16),
        scratch_shapes=[pltpu.VMEM((SUBLANES, LANES), F32), pltpu.VMEM((SUBLANES, LANES), F32),
                        pltpu.VMEM((SUBLANES, LANES), F32), pltpu.VMEM((SUBLANES, RWKV_LOW), F32),
                        pltpu.VMEM((LANES, LANES), F32),
                        pltpu.VMEM((2, 6, tb, LANES), F32),
                        pltpu.VMEM((2, nc, LANES, LANES), BF16), pltpu.VMEM((2, nc, LANES, LANES), F32),
                        pltpu.VMEM((2, nc, LANES, LANES), BF16), pltpu.VMEM((2, nc, LANES, LANES), F32),
                        pltpu.VMEM((2, nc, SUBLANES, LANES), F32),
                        pltpu.VMEM((3, tb, LANES), F32), pltpu.VMEM((3, tb, LANES), F32)],
        compiler_params=pltpu.CompilerParams(
            dimension_semantics=("arbitrary",), vmem_limit_bytes=VMEM_LIMIT),
        name="rwkv7",
    )(proj3, proj3, proj3, proj3, mu, mu, mu, mu, ww, wa, wg, w0, a0, k_k, k_a, r_k, ln_w, ln_b)


def _conv_kernel(c_ref, x_ref, b_ref, w_ref, y_ref, carry_ref):
    @pl.when(pl.program_id(2) == 0)
    def _():
        carry_ref[...] = jnp.zeros_like(carry_ref)

    z = c_ref[0].astype(F32) * x_ref[0].astype(F32)
    tb = z.shape[0]
    w = w_ref[...]
    y = z * w[CONV_K - 1:CONV_K, :]
    for j in range(1, CONV_K):
        y = y + _prev_rows(z, carry_ref[...], j) * w[CONV_K - 1 - j:CONV_K - j, :]
    carry_ref[...] = z[tb - SUBLANES:tb, :]
    y_ref[0] = (b_ref[0].astype(F32) * y).astype(y_ref.dtype)


def _conv(proj3, conv_w, *, tb=1024, cw=2 * LANES):
    b, t, _ = proj3.shape
    tb = min(tb, t)
    nblk = CONV_WIDTH // cw
    off = RWKV_SHIFT_WIDTH // cw
    spec = lambda o: pl.BlockSpec((1, tb, cw), lambda bi, p, ti: (bi, ti, o + p))
    return pl.pallas_call(
        _conv_kernel,
        grid=(b, nblk, t // tb),
        in_specs=[spec(off), spec(off + nblk), spec(off + 2 * nblk),
                  pl.BlockSpec((CONV_K, cw), lambda bi, p, ti: (0, p))],
        out_specs=pl.BlockSpec((1, tb, cw), lambda bi, p, ti: (bi, ti, p)),
        out_shape=jax.ShapeDtypeStruct((b, t, CONV_WIDTH), BF16),
        scratch_shapes=[pltpu.VMEM((SUBLANES, cw), F32)],
        compiler_params=pltpu.CompilerParams(
            dimension_semantics=("parallel", "parallel", "arbitrary"), vmem_limit_bytes=VMEM_LIMIT),
        name="shortconv",
    )(proj3, proj3, proj3, conv_w)


def _hgrn_kernel(q_ref, f_ref, i_ref, og_ref, lb_ref, nw_ref, y_ref, st_ref, *, layer, tb):
    @pl.when(pl.program_id(2) == 0)
    def _():
        st_ref[...] = jnp.zeros_like(st_ref)

    lbp = lb_ref[...]
    e = jnp.exp(lbp - jnp.max(lbp, axis=0, keepdims=True))
    pr = e / jnp.sum(e, axis=0, keepdims=True)
    lb = jnp.zeros((1, LANES), F32)
    for l in range(1, layer + 1):
        lb = lb + pr[l:l + 1, :]

    qr = q_ref[0].astype(F32)
    q = qr * jax.nn.sigmoid(qr)
    f = f_ref[0].astype(F32)
    forget = lb + (1.0 - lb) * jax.nn.sigmoid(f)
    logf = jnp.log(jnp.maximum(forget, HGRN_MIN_FORGET))
    kx = (1.0 - lb) * jax.nn.sigmoid(-f)
    iv = i_ref[0].astype(F32)

    n = HGRN_BLOCK
    nb = tb // n
    ti = lax.broadcasted_iota(jnp.int32, (tb, tb), 0)
    si = lax.broadcasted_iota(jnp.int32, (tb, tb), 1)
    same = (ti ^ si) < n
    sums = jnp.concatenate([(same & (ti >= si)).astype(F32).astype(BF16), same.astype(F32).astype(BF16)], axis=0)
    hi, lo = _split_bf16(logf)
    gg = _dot(sums, jnp.concatenate([hi, lo], axis=1))
    g = gg[:tb, :LANES] + gg[:tb, LANES:]
    gl = gg[tb:, :LANES] + gg[tb:, LANES:]
    qd = (q * jnp.exp(g)).astype(BF16)
    kd = (kx * jnp.exp(gl - g)).astype(BF16)
    dec = jnp.exp(gl)
    trow = lax.broadcasted_iota(jnp.int32, (n, 1), 0)
    blocks = lambda x: [x[i * n:(i + 1) * n] for i in range(nb)]
    gb, qb, kb, ib, qdb, kdb = (blocks(x) for x in (g, q, kx, iv, qd, kd))

    upd = [_dot(i_.T.astype(BF16), k_) for i_, k_ in zip(ib, kdb)]
    st = st_ref[...]
    sts = []
    for i in range(nb):
        sts.append(st.astype(BF16))
        st = st * dec[i * n:i * n + 1, :] + upd[i]
    st_ref[...] = st
    outs = [_dot_nt(q_, s_) for q_, s_ in zip(qdb, sts)]
    for s in range(n):
        for i in range(nb):
            w = jnp.sum(qb[i] * kb[i][s:s + 1, :] * jnp.exp(gb[i] - gb[i][s:s + 1, :]), axis=-1, keepdims=True)
            outs[i] = outs[i] + jnp.where(trow >= s, w, 0.0) * ib[i][s:s + 1, :]
    o = jnp.concatenate(outs, axis=0)
    o = o * lax.rsqrt(jnp.mean(o * o, axis=-1, keepdims=True) + HGRN_NORM_EPS) * nw_ref[...]
    og = og_ref[0].astype(F32)
    y_ref[0] = (o * (og * jax.nn.sigmoid(og))).astype(y_ref.dtype)


def _hgrn(proj3, hgrn_lb, norm_w, *, layer, tb=512):
    b, t, _ = proj3.shape
    tb = min(tb, t)
    nh = HGRN_WIDTH // HGRN_HEAD_DIM
    off = (RWKV_SHIFT_WIDTH + 3 * CONV_WIDTH) // LANES
    spec = lambda o: pl.BlockSpec((1, tb, LANES), lambda bi, p, ti: (bi, ti, o + p))
    depth = hgrn_lb.shape[0]
    return pl.pallas_call(
        functools.partial(_hgrn_kernel, layer=layer, tb=tb),
        grid=(b, nh, t // tb),
        in_specs=[spec(off), spec(off + nh), spec(off + 2 * nh), spec(off + 3 * nh),
                  pl.BlockSpec((depth, LANES), lambda bi, p, ti: (0, p)),
                  pl.BlockSpec((1, LANES), lambda bi, p, ti: (0, p))],
        out_specs=pl.BlockSpec((1, tb, LANES), lambda bi, p, ti: (bi, ti, p)),
        out_shape=jax.ShapeDtypeStruct((b, t, HGRN_WIDTH), BF16),
        scratch_shapes=[pltpu.VMEM((LANES, LANES), F32)],
        compiler_params=pltpu.CompilerParams(
            dimension_semantics=("parallel", "parallel", "arbitrary"), vmem_limit_bytes=VMEM_LIMIT),
        name="hgrn2",
    )(proj3, proj3, proj3, proj3, hgrn_lb, norm_w)


def _cast_kernel(x_ref, o_ref, *, valid_rows, rows):
    x = x_ref[...]
    if valid_rows % rows:
        r = pl.program_id(1) * rows + lax.broadcasted_iota(jnp.int32, x.shape, 1)
        x = jnp.where(r < valid_rows, x, 0.0)
    o_ref[...] = x.astype(o_ref.dtype)


def _cast_rows(w, out_rows, *, rows=256):
    nl, r, c = w.shape
    return pl.pallas_call(
        functools.partial(_cast_kernel, valid_rows=r, rows=rows),
        grid=(nl, out_rows // rows),
        in_specs=[pl.BlockSpec((1, rows, c), lambda l, i: (l, i, 0))],
        out_specs=pl.BlockSpec((1, rows, c), lambda l, i: (l, i, 0)),
        out_shape=jax.ShapeDtypeStruct((nl, out_rows, c), BF16),
        compiler_params=pltpu.CompilerParams(
            dimension_semantics=("parallel", "parallel"), vmem_limit_bytes=VMEM_LIMIT),
        name="cast_rows",
    )(w)


def _split_gu_kernel(x_ref, g_ref, u_ref, *, d_ff):
    x = x_ref[0]
    pad = jnp.zeros((x.shape[0], g_ref.shape[2] - d_ff), g_ref.dtype)
    g_ref[0] = jnp.concatenate([x[:, :d_ff].astype(g_ref.dtype), pad], axis=1)
    u_ref[0] = jnp.concatenate([x[:, d_ff:].astype(u_ref.dtype), pad], axis=1)


def _split_gu(w_gu, d_ff, ffp, *, rows=256):
    nl, d, _ = w_gu.shape
    out = jax.ShapeDtypeStruct((nl, d, ffp), BF16)
    spec = pl.BlockSpec((1, rows, ffp), lambda l, i: (l, i, 0))
    return pl.pallas_call(
        functools.partial(_split_gu_kernel, d_ff=d_ff),
        grid=(nl, d // rows),
        in_specs=[pl.BlockSpec((1, rows, 2 * d_ff), lambda l, i: (l, i, 0))],
        out_specs=(spec, spec),
        out_shape=(out, out),
        compiler_params=pltpu.CompilerParams(
            dimension_semantics=("parallel", "parallel"), vmem_limit_bytes=VMEM_LIMIT),
        name="split_gate_up",
    )(w_gu)


def kernel(x, ffn1_norm, ffn1_w_gu, ffn1_w_down, mix_norm, w_in, rwkv_mu, rwkv_w_up, rwkv_w0,
           rwkv_a_up, rwkv_a0, rwkv_g_up, rwkv_k_k, rwkv_k_a, rwkv_r_k, rwkv_ln_w, rwkv_ln_b,
           conv_w, hgrn_lb, hgrn_norm, w_out, ffn2_norm, ffn2_w_gu, ffn2_w_down, final_norm):
    b, t, d = x.shape
    depth = ffn1_norm.shape[0]
    d_ff = ffn1_w_down.shape[1]
    ffp = -(-d_ff // 512) * 512
    m = b * t
    h = x.reshape(m, d)
    row = lambda v: v.reshape(1, -1)
    final_w = row(final_norm)

    wg1, wu1 = _split_gu(ffn1_w_gu, d_ff, ffp)
    wg2, wu2 = _split_gu(ffn2_w_gu, d_ff, ffp)
    wd1 = _cast_rows(ffn1_w_down, ffp)
    wd2 = _cast_rows(ffn2_w_down, ffp)
    w_in_b = _cast_rows(w_in, d)
    w_out_b = _cast_rows(w_out, d)
    zeros_low = jnp.zeros((depth, LANES - DECAY_RANK, RWKV_WIDTH), BF16)
    ww = jnp.concatenate([rwkv_w_up.astype(BF16), zeros_low], axis=1)
    wa = jnp.concatenate([zeros_low, rwkv_a_up.astype(BF16)], axis=1)
    wgate = rwkv_g_up.astype(BF16)

    for l in range(depth):
        h = _ffn(h, row(ffn1_norm[l]), wg1, wu1, wd1, final_w, layer=l, final=False)

        proj = _proj(h, row(mix_norm[l]), w_in_b, layer=l)
        proj3 = proj.reshape(b, t, -1)
        y_r = _rwkv(proj3, row(rwkv_mu[l]), ww[l], wa[l], wgate[l],
                    row(rwkv_w0[l]), row(rwkv_a0[l]), row(rwkv_k_k[l]), row(rwkv_k_a[l]),
                    row(rwkv_r_k[l]), row(rwkv_ln_w[l]), row(rwkv_ln_b[l]))
        y_c = _conv(proj3, conv_w[l])
        y_h = _hgrn(proj3, hgrn_lb, row(hgrn_norm[l]), layer=l)
        h = _outproj(h, y_r.reshape(m, -1), y_c.reshape(m, -1), y_h.reshape(m, -1), w_out_b, layer=l)

        h = _ffn(h, row(ffn2_norm[l]), wg2, wu2, wd2, final_w, layer=l, final=(l == depth - 1))
    return h.reshape(b, t, d)
```

```python
import functools

import jax
import jax.numpy as jnp
from jax import lax
from jax.experimental import pallas as pl
from jax.experimental.pallas import tpu as pltpu

F32 = jnp.float32
BF16 = jnp.bfloat16

RWKV_HEAD_DIM = 64
RWKV_WIDTH = 1024
DECAY_RANK = 64
AAA_RANK = 64
GATE_RANK = 128
RWKV_LOW = DECAY_RANK + AAA_RANK + GATE_RANK
RWKV_SHIFT_WIDTH = 3 * RWKV_WIDTH + RWKV_LOW
RWKV_GN_EPS = 64e-5
CONV_WIDTH = 512
CONV_K = 3
HGRN_HEAD_DIM = 128
HGRN_WIDTH = 512
HGRN_NORM_EPS = 1e-5
HGRN_MIN_FORGET = 1e-30
FFN_RESIDUAL_SCALE = 0.5
RMS_EPS = 1e-6

LANES = 128
SUBLANES = 8
RWKV_CHUNK = 64
HGRN_BLOCK = 16
VMEM_LIMIT = 56 * 1024 * 1024


def _dot(a, b):
    return jnp.dot(a, b, preferred_element_type=F32)


def _dot_nt(a, b):
    return lax.dot_general(a, b, (((1,), (1,)), ((), ())), preferred_element_type=F32)


def _rms(x, w):
    return x * lax.rsqrt(jnp.mean(x * x, axis=-1, keepdims=True) + RMS_EPS) * w


def _split_bf16(x):
    hi = x.astype(BF16)
    lo = (x - hi.astype(F32)).astype(BF16)
    return hi, lo


def _ffn_kernel(x_ref, nw_ref, wg_ref, wu_ref, wd_ref, fw_ref, o_ref, xn_ref, *, final):
    j = pl.program_id(1)

    def partial_sum():
        xn = xn_ref[...]
        g = _dot(xn, wg_ref[...])
        u = _dot(xn, wu_ref[...])
        a = (g * jax.nn.sigmoid(g) * u).astype(BF16)
        return _dot(a, wd_ref[...])

    @pl.when(j == 0)
    def _():
        xn_ref[...] = _rms(x_ref[...], nw_ref[...]).astype(BF16)
        o_ref[...] = partial_sum()

    last = pl.num_programs(1) - 1

    @pl.when((j > 0) & (j < last))
    def _():
        o_ref[...] += partial_sum()

    @pl.when(j == last)
    def _():
        h = x_ref[...] + FFN_RESIDUAL_SCALE * (o_ref[...] + partial_sum())
        if final:
            h = _rms(h, fw_ref[...])
        o_ref[...] = h


def _ffn(h, norm_w, wg, wu, wd, final_w, *, layer, final, tm=1024, tf=512):
    m, d = h.shape
    ffp = wg.shape[2]
    grid = (m // tm, ffp // tf)
    assert grid[1] >= 2
    return pl.pallas_call(
        functools.partial(_ffn_kernel, final=final),
        grid=grid,
        in_specs=[
            pl.BlockSpec((tm, d), lambda i, j: (i, 0)),
            pl.BlockSpec((1, d), lambda i, j: (0, 0)),
            pl.BlockSpec((None, d, tf), lambda i, j: (layer, 0, j)),
            pl.BlockSpec((None, d, tf), lambda i, j: (layer, 0, j)),
            pl.BlockSpec((None, tf, d), lambda i, j: (layer, j, 0)),
            pl.BlockSpec((1, d), lambda i, j: (0, 0)),
        ],
        out_specs=pl.BlockSpec((tm, d), lambda i, j: (i, 0)),
        out_shape=jax.ShapeDtypeStruct((m, d), F32),
        scratch_shapes=[pltpu.VMEM((tm, d), BF16)],
        compiler_params=pltpu.CompilerParams(
            dimension_semantics=("parallel", "arbitrary"), vmem_limit_bytes=VMEM_LIMIT),
        name="ffn",
    )(h, norm_w, wg, wu, wd, final_w)


def _proj_kernel(x_ref, nw_ref, w_ref, o_ref, xn_ref):
    @pl.when(pl.program_id(1) == 0)
    def _():
        xn_ref[...] = _rms(x_ref[...], nw_ref[...]).astype(BF16)
        o_ref[...] = _dot(xn_ref[...], w_ref[...]).astype(o_ref.dtype)

    @pl.when(pl.program_id(1) > 0)
    def _():
        o_ref[...] = _dot(xn_ref[...], w_ref[...]).astype(o_ref.dtype)


def _proj(h, norm_w, w, *, layer, tm=1024, tn=2304):
    m, d = h.shape
    n = w.shape[2]
    return pl.pallas_call(
        _proj_kernel,
        grid=(m // tm, n // tn),
        in_specs=[
            pl.BlockSpec((tm, d), lambda i, j: (i, 0)),
            pl.BlockSpec((1, d), lambda i, j: (0, 0)),
            pl.BlockSpec((None, d, tn), lambda i, j: (layer, 0, j)),
        ],
        out_specs=pl.BlockSpec((tm, tn), lambda i, j: (i, j)),
        out_shape=jax.ShapeDtypeStruct((m, n), BF16),
        scratch_shapes=[pltpu.VMEM((tm, d), BF16)],
        compiler_params=pltpu.CompilerParams(
            dimension_semantics=("parallel", "arbitrary"), vmem_limit_bytes=VMEM_LIMIT),
        name="proj",
    )(h, norm_w, w)


def _outproj_kernel(h_ref, yr_ref, yc_ref, yh_ref, wr_ref, wc_ref, wh_ref, o_ref):
    acc = _dot(yr_ref[...], wr_ref[...])
    acc += _dot(yc_ref[...], wc_ref[...])
    acc += _dot(yh_ref[...], wh_ref[...])
    o_ref[...] = h_ref[...] + acc


def _outproj(h, yr, yc, yh, wo, *, layer, tm=512):
    m, d = h.shape
    nr, ncv, nh = yr.shape[1], yc.shape[1], yh.shape[1]
    row = lambda i: (i, 0)
    return pl.pallas_call(
        _outproj_kernel,
        grid=(m // tm,),
        in_specs=[
            pl.BlockSpec((tm, d), row),
            pl.BlockSpec((tm, yr.shape[1]), row),
            pl.BlockSpec((tm, yc.shape[1]), row),
            pl.BlockSpec((tm, yh.shape[1]), row),
            pl.BlockSpec((None, nr, d), lambda i: (layer, 0, 0)),
            pl.BlockSpec((None, ncv, d), lambda i: (layer, nr // ncv, 0)),
            pl.BlockSpec((None, nh, d), lambda i: (layer, (nr + ncv) // nh, 0)),
        ],
        out_specs=pl.BlockSpec((tm, d), row),
        out_shape=jax.ShapeDtypeStruct((m, d), F32),
        compiler_params=pltpu.CompilerParams(
            dimension_semantics=("parallel",), vmem_limit_bytes=VMEM_LIMIT),
        name="outproj",
    )(h, yr, yc, yh, wo, wo, wo)


def _prev_rows(x, carry, k=1):
    rows = lax.broadcasted_iota(jnp.int32, (SUBLANES, x.shape[1]), 0)
    prev = pltpu.roll(x, k, 0)
    head = jnp.where(rows < k, pltpu.roll(carry, k, 0), prev[:SUBLANES])
    return jnp.concatenate([head, prev[SUBLANES:]], axis=0)


def _seg_sum(x, ones_bd):
    return _dot(x.astype(BF16), ones_bd)


def _rwkv_prepare(lw, r, k2, v, av, bv, c, tick):
    n = 2 * RWKV_CHUNK
    m0, m1 = c["m0"], c["m1"]
    stack = lambda x: jnp.concatenate([x * m0, x * m1], axis=0)
    each = lambda f, *ls: [f(*xs) for xs in zip(*ls)]

    def cumsum(x):
        hi, lo = _split_bf16(x)
        gcat = _dot(c["tri"], jnp.concatenate([hi, lo], axis=1))
        return gcat[:, :LANES] + gcat[:, LANES:]

    g = each(cumsum, lw)
    tick()
    gl = [x[RWKV_CHUNK - 1:RWKV_CHUNK, :] for x in g]
    e_g = each(jnp.exp, g)
    e_ng = each(lambda x: jnp.exp(-x), g)
    e_d = each(lambda x, y: jnp.exp(x - y), gl, g)
    ats = each(lambda a_, g_, l_: stack(a_ * jnp.exp(g_ - l_)), av, g, lw)
    rts = each(lambda r_, e_: stack(r_ * e_), r, e_g)
    bts = each(lambda b_, e_: stack(b_ * e_), bv, e_ng)
    kts = each(lambda k_, e_: stack(k_ * e_), k2, e_ng)
    bes = each(lambda b_, e_: stack(b_ * e_).astype(BF16), bv, e_d)
    kes = each(lambda k_, e_: stack(k_ * e_).astype(BF16), k2, e_d)
    vs = each(stack, v)
    vsb = [x.astype(BF16) for x in vs]

    p = each(lambda a_, r_, b_, k_: _dot_nt(jnp.concatenate([a_, r_], axis=0).astype(BF16),
                                            jnp.concatenate([b_, k_], axis=0).astype(BF16)),
             ats, rts, bts, kts)
    a_ab = [jnp.where(c["strict"], x[:n, :n], 0.0) for x in p]
    a_kr = [jnp.concatenate([jnp.where(c["strict"], x[:n, n:], 0.0),
                             jnp.where(c["incl"], x[n:, n:], 0.0)], axis=0).astype(BF16) for x in p]
    a_rb = [jnp.where(c["incl"], x[n:, :n], 0.0).astype(BF16) for x in p]
    tick()

    h = RWKV_CHUNK
    side = lambda x: x[:h] + x[h:]
    diag = lambda x: jnp.concatenate([x, x], axis=0) * c["own"]
    a_sbs = [side(x) for x in a_ab]
    inv = [c["eye"] + x for x in a_sbs]
    q = [_dot(x.astype(BF16), y.astype(BF16)) for x, y in zip(a_sbs, a_ab)]
    tick()
    for _ in range(4):
        z = each(lambda i_, q_: _dot(jnp.concatenate([i_, q_], axis=0).astype(BF16), diag(q_).astype(BF16)), inv, q)
        inv = each(lambda i_, z_: i_ + z_[:h], inv, z)
        q = [x[h:] for x in z]
        tick()
    inv = each(lambda i_, q_: diag(i_ + _dot(i_.astype(BF16), diag(q_).astype(BF16))), inv, q)
    tick()

    akv_rkv = each(_dot, a_kr, vsb)
    tick()
    tt = each(lambda i_, a_, x_: _dot(i_.astype(BF16), jnp.concatenate([a_, x_[:n]], axis=1).astype(BF16)),
              inv, ats, akv_rkv)
    tick()
    rr = each(lambda a_, t_: _dot(a_, t_.astype(BF16)), a_rb, tt)
    rq = each(lambda r_, x_: (r_ + x_[:, :LANES]).astype(BF16), rts, rr)
    oc = each(lambda x_, y_: x_[:, LANES:] + y_[n:], rr, akv_rkv)
    bb = each(lambda t_, b_: _dot(t_.T.astype(BF16), b_), tt, bes)
    bta = [x[:LANES].astype(BF16) for x in bb]
    nct = each(lambda x_, v_, k_: x_[LANES:] + _dot(v_.T.astype(BF16), k_), bb, vs, kes)
    e_gl = each(jnp.exp, gl)
    return rq, oc, bta, nct, e_gl


def _rwkv_kernel(r_ref, k_ref, v_ref, low_ref, mur_ref, muk_ref, muv_ref, mul_ref,
                 ww_ref, wa_ref, wg_ref, w0_ref, a0_ref, kk_ref, ka_ref, rk_ref, lnw_ref, lnb_ref,
                 y_ref, cr_ref, ck_ref, cv_ref, cl_ref, st_ref, feat_ref,
                 rq_ref, oc_ref, bta_ref, nct_ref, egl_ref, bonus_ref, gate_ref, *, tb, nt):
    s = pl.program_id(0)
    nc = tb // RWKV_CHUNK

    @pl.when(s == 0)
    def _():
        for ref in (cr_ref, ck_ref, cv_ref, cl_ref, st_ref, feat_ref, rq_ref, oc_ref, bta_ref, nct_ref, egl_ref,
                    bonus_ref, gate_ref):
            ref[...] = jnp.zeros_like(ref)

    lane = lax.broadcasted_iota(jnp.int32, (1, LANES), 1)
    m0 = (lane < RWKV_HEAD_DIM).astype(F32)
    m1 = 1.0 - m0
    ri = lax.broadcasted_iota(jnp.int32, (LANES, LANES), 0)
    ci = lax.broadcasted_iota(jnp.int32, (LANES, LANES), 1)
    same = (ri < RWKV_HEAD_DIM) == (ci < RWKV_HEAD_DIM)
    ones_bd = same.astype(F32).astype(BF16)
    ti = lax.broadcasted_iota(jnp.int32, (RWKV_CHUNK, RWKV_CHUNK), 0)
    si = lax.broadcasted_iota(jnp.int32, (RWKV_CHUNK, RWKV_CHUNK), 1)
    tw = lax.broadcasted_iota(jnp.int32, (RWKV_CHUNK, LANES), 0)
    sw = lax.broadcasted_iota(jnp.int32, (RWKV_CHUNK, LANES), 1) & (RWKV_HEAD_DIM - 1)
    consts = dict(
        m0=m0, m1=m1, own=same.astype(F32),
        strict=same & (ri > ci), incl=same & (ri >= ci),
        eye=(tw == sw).astype(F32),
        tri=(ti >= si).astype(F32).astype(BF16),
    )

    rd = lax.rem(s, 2)
    rd3 = lax.rem(s + 1, 3)
    seq_start = lax.rem(jnp.maximum(s - 2, 0), nt) == 0
    state = [st_ref[...] * jnp.where(seq_start, 0.0, 1.0)]
    outs = []
    inv_n = 1.0 / RWKV_HEAD_DIM
    own = same.astype(F32)

    def recurrence_step():
        i = len(outs)
        if i == nc:
            return
        st = state[0]
        stb = st.astype(BF16)
        o_stack = _dot_nt(rq_ref[rd, i], stb) + oc_ref[rd, i]
        state[0] = st * egl_ref[rd, i][0:1, :] + _dot(stb, bta_ref[rd, i]) + nct_ref[rd, i]
        mean = jnp.sum(o_stack, axis=-1, keepdims=True) * inv_n
        cen = (o_stack - mean) * own
        var = jnp.sum(cen * cen, axis=-1, keepdims=True) * inv_n
        on = cen * lax.rsqrt(var + RWKV_GN_EPS)
        outs.append(on[:RWKV_CHUNK] + on[RWKV_CHUNK:])

    wf = lax.rem(s, 2)
    wf3 = lax.rem(s, 3)
    keep = jnp.where(lax.rem(jnp.minimum(s, pl.num_programs(0) - 3), nt) == 0, 0.0, 1.0)
    f = {}

    def lerp(x_ref, c_ref, mu_ref):
        x = x_ref[0].astype(F32)
        prev = _prev_rows(x, c_ref[...] * keep)
        c_ref[...] = x[tb - SUBLANES:tb, :]
        return x + (prev - x) * mu_ref[...]

    def feat_low():
        low = lerp(low_ref, cl_ref, mul_ref)
        f["wa_in"] = low[:, :LANES]
        gate_ref[wf3] = _dot(jax.nn.sigmoid(low[:, LANES:]).astype(BF16), wg_ref[...])

    def feat_decay():
        xw = w0_ref[...] + _dot(jnp.tanh(f["wa_in"]).astype(BF16), ww_ref[...])
        feat_ref[wf, 0] = -jnp.exp(jnp.float32(-0.5)) * jax.nn.sigmoid(xw)
        f["a"] = jax.nn.sigmoid(a0_ref[...] + _dot(f["wa_in"].astype(BF16), wa_ref[...]))

    def feat_k():
        k = lerp(k_ref, ck_ref, muk_ref)
        kk = k * kk_ref[...]
        kk = kk * lax.rsqrt(jnp.maximum(_seg_sum(kk * kk, ones_bd), 1e-24))
        f["k2"] = k * (1.0 + (f["a"] - 1.0) * ka_ref[...])
        feat_ref[wf, 2] = f["k2"]
        feat_ref[wf, 4] = -kk
        feat_ref[wf, 5] = kk * f["a"]

    def feat_rv():
        r = lerp(r_ref, cr_ref, mur_ref)
        v = lerp(v_ref, cv_ref, muv_ref)
        feat_ref[wf, 1] = r
        feat_ref[wf, 3] = v
        bonus_ref[wf3] = _seg_sum(r * f["k2"] * rk_ref[...], ones_bd) * v

    pieces = [feat_low, feat_decay, feat_k, feat_rv]

    def tick():
        if pieces:
            pieces.pop(0)()
        recurrence_step()

    rf = lax.rem(s + 1, 2)
    wr = lax.rem(s + 1, 2)
    chunks = lambda j: [feat_ref[rf, j, i * RWKV_CHUNK:(i + 1) * RWKV_CHUNK, :] for i in range(nc)]
    rq, oc, bta, nct, e_gl = _rwkv_prepare(chunks(0), chunks(1), chunks(2), chunks(3), chunks(4), chunks(5),
                                           consts, tick)
    while pieces or len(outs) < nc:
        tick()

    st_ref[...] = state[0]
    o = jnp.concatenate(outs, axis=0) * lnw_ref[...] + lnb_ref[...]
    y_ref[0] = ((o + bonus_ref[rd3]) * gate_ref[rd3]).astype(y_ref.dtype)

    for i in range(nc):
        rq_ref[wr, i] = rq[i]
        oc_ref[wr, i] = oc[i]
        bta_ref[wr, i] = bta[i]
        nct_ref[wr, i] = nct[i]
        egl_ref[wr, i] = jnp.broadcast_to(e_gl[i], (SUBLANES, LANES))


def _rwkv(proj3, mu, ww, wa, wg, w0, a0, k_k, k_a, r_k, ln_w, ln_b, *, tb=512):
    b, t, _ = proj3.shape
    npairs = RWKV_WIDTH // LANES
    nt = t // tb
    nc = tb // RWKV_CHUNK
    nblocks = b * npairs * nt
    off_k = RWKV_WIDTH // LANES
    off_v = 2 * RWKV_WIDTH // LANES
    off_low = 3 * RWKV_WIDTH // RWKV_LOW

    def where(item):
        return item // (nt * npairs), item % nt, (item // nt) % npairs

    cur = lambda s: where(jnp.minimum(s, nblocks - 1))
    lag = lambda s: where(jnp.maximum(s - 2, 0))
    act = lambda off: pl.BlockSpec((1, tb, LANES), lambda s: (cur(s)[0], cur(s)[1], off + cur(s)[2]))
    vec = lambda off: pl.BlockSpec((1, LANES), lambda s: (0, off + cur(s)[2]))
    lagvec = pl.BlockSpec((1, LANES), lambda s: (0, lag(s)[2]))
    wspec = pl.BlockSpec((LANES, LANES), lambda s: (0, cur(s)[2]))
    return pl.pallas_call(
        functools.partial(_rwkv_kernel, tb=tb, nt=nt),
        grid=(nblocks + 2,),
        in_specs=[
            act(0), act(off_k), act(off_v),
            pl.BlockSpec((1, tb, RWKV_LOW), lambda s: (cur(s)[0], cur(s)[1], off_low)),
            vec(0), vec(off_k), vec(off_v),
            pl.BlockSpec((1, RWKV_LOW), lambda s: (0, off_low)),
            wspec, wspec, wspec,
            vec(0), vec(0), vec(0), vec(0), vec(0), lagvec, lagvec,
        ],
        out_specs=pl.BlockSpec((1, tb, LANES), lambda s: lag(s)),
        out_shape=jax.ShapeDtypeStruct((b, t, RWKV_WIDTH), BF16),
        scratch_shapes=[pltpu.VMEM((SUBLANES, LANES), F32), pltpu.VMEM((SUBLANES, LANES), F32),
                        pltpu.VMEM((SUBLANES, LANES), F32), pltpu.VMEM((SUBLANES, RWKV_LOW), F32),
                        pltpu.VMEM((LANES, LANES), F32),
                        pltpu.VMEM((2, 6, tb, LANES), F32),
                        pltpu.VMEM((2, nc, LANES, LANES), BF16), pltpu.VMEM((2, nc, LANES, LANES), F32),
                        pltpu.VMEM((2, nc, LANES, LANES), BF16), pltpu.VMEM((2, nc, LANES, LANES), F32),
                        pltpu.VMEM((2, nc, SUBLANES, LANES), F32),
                        pltpu.VMEM((3, tb, LANES), F32), pltpu.VMEM((3, tb, LANES), F32)],
        compiler_params=pltpu.CompilerParams(
            dimension_semantics=("arbitrary",), vmem_limit_bytes=VMEM_LIMIT),
        name="rwkv7",
    )(proj3, proj3, proj3, proj3, mu, mu, mu, mu, ww, wa, wg, w0, a0, k_k, k_a, r_k, ln_w, ln_b)


def _conv_kernel(c_ref, x_ref, b_ref, w_ref, y_ref, carry_ref):
    @pl.when(pl.program_id(2) == 0)
    def _():
        carry_ref[...] = jnp.zeros_like(carry_ref)

    z = c_ref[0].astype(F32) * x_ref[0].astype(F32)
    tb = z.shape[0]
    w = w_ref[...]
    y = z * w[CONV_K - 1:CONV_K, :]
    for j in range(1, CONV_K):
        y = y + _prev_rows(z, carry_ref[...], j) * w[CONV_K - 1 - j:CONV_K - j, :]
    carry_ref[...] = z[tb - SUBLANES:tb, :]
    y_ref[0] = (b_ref[0].astype(F32) * y).astype(y_ref.dtype)


def _conv(proj3, conv_w, *, tb=1024, cw=2 * LANES):
    b, t, _ = proj3.shape
    tb = min(tb, t)
    nblk = CONV_WIDTH // cw
    off = RWKV_SHIFT_WIDTH // cw
    spec = lambda o: pl.BlockSpec((1, tb, cw), lambda bi, p, ti: (bi, ti, o + p))
    return pl.pallas_call(
        _conv_kernel,
        grid=(b, nblk, t // tb),
        in_specs=[spec(off), spec(off + nblk), spec(off + 2 * nblk),
                  pl.BlockSpec((CONV_K, cw), lambda bi, p, ti: (0, p))],
        out_specs=pl.BlockSpec((1, tb, cw), lambda bi, p, ti: (bi, ti, p)),
        out_shape=jax.ShapeDtypeStruct((b, t, CONV_WIDTH), BF16),
        scratch_shapes=[pltpu.VMEM((SUBLANES, cw), F32)],
        compiler_params=pltpu.CompilerParams(
            dimension_semantics=("parallel", "parallel", "arbitrary"), vmem_limit_bytes=VMEM_LIMIT),
        name="shortconv",
    )(proj3, proj3, proj3, conv_w)


def _hgrn_kernel(q_ref, f_ref, i_ref, og_ref, lb_ref, nw_ref, y_ref, st_ref, *, layer, tb):
    @pl.when(pl.program_id(2) == 0)
    def _():
        st_ref[...] = jnp.zeros_like(st_ref)

    lbp = lb_ref[...]
    e = jnp.exp(lbp - jnp.max(lbp, axis=0, keepdims=True))
    pr = e / jnp.sum(e, axis=0, keepdims=True)
    lb = jnp.zeros((1, LANES), F32)
    for l in range(1, layer + 1):
        lb = lb + pr[l:l + 1, :]

    qr = q_ref[0].astype(F32)
    q = qr * jax.nn.sigmoid(qr)
    f = f_ref[0].astype(F32)
    forget = lb + (1.0 - lb) * jax.nn.sigmoid(f)
    logf = jnp.log(jnp.maximum(forget, HGRN_MIN_FORGET))
    kx = (1.0 - lb) * jax.nn.sigmoid(-f)
    iv = i_ref[0].astype(F32)

    n = HGRN_BLOCK
    nb = tb // n
    ti = lax.broadcasted_iota(jnp.int32, (tb, tb), 0)
    si = lax.broadcasted_iota(jnp.int32, (tb, tb), 1)
    same = (ti ^ si) < n
    sums = jnp.concatenate([(same & (ti >= si)).astype(F32).astype(BF16), same.astype(F32).astype(BF16)], axis=0)
    hi, lo = _split_bf16(logf)
    gg = _dot(sums, jnp.concatenate([hi, lo], axis=1))
    g = gg[:tb, :LANES] + gg[:tb, LANES:]
    gl = gg[tb:, :LANES] + gg[tb:, LANES:]
    qd = (q * jnp.exp(g)).astype(BF16)
    kd = (kx * jnp.exp(gl - g)).astype(BF16)
    dec = jnp.exp(gl)
    trow = lax.broadcasted_iota(jnp.int32, (n, 1), 0)
    blocks = lambda x: [x[i * n:(i + 1) * n] for i in range(nb)]
    gb, qb, kb, ib, qdb, kdb = (blocks(x) for x in (g * 1.4426950408889634, q, kx, iv, qd, kd))

    upd = [_dot(i_.T.astype(BF16), k_) for i_, k_ in zip(ib, kdb)]
    st = st_ref[...]
    sts = []
    for i in range(nb):
        sts.append(st.astype(BF16))
        st = st * dec[i * n:i * n + 1, :] + upd[i]
    st_ref[...] = st
    outs = [_dot_nt(q_, s_) for q_, s_ in zip(qdb, sts)]
    lane = lax.broadcasted_iota(jnp.int32, (n, LANES), 1)
    scores = [jnp.zeros((n, LANES), F32) for _ in range(nb)]
    for s in range(n):
        for i in range(nb):
            w = jnp.sum(qb[i] * kb[i][s:s + 1, :] * jnp.exp2(gb[i] - gb[i][s:s + 1, :]), axis=-1, keepdims=True)
            scores[i] = jnp.where(lane == s, w, scores[i])
    causal = lane <= trow
    for i in range(nb):
        sc = jnp.where(causal, scores[i], 0.0)[:, :n].astype(BF16)
        outs[i] = outs[i] + _dot(sc, ib[i].astype(BF16))
    o = jnp.concatenate(outs, axis=0)
    o = o * lax.rsqrt(jnp.mean(o * o, axis=-1, keepdims=True) + HGRN_NORM_EPS) * nw_ref[...]
    og = og_ref[0].astype(F32)
    y_ref[0] = (o * (og * jax.nn.sigmoid(og))).astype(y_ref.dtype)


def _hgrn(proj3, hgrn_lb, norm_w, *, layer, tb=512):
    b, t, _ = proj3.shape
    tb = min(tb, t)
    nh = HGRN_WIDTH // HGRN_HEAD_DIM
    off = (RWKV_SHIFT_WIDTH + 3 * CONV_WIDTH) // LANES
    spec = lambda o: pl.BlockSpec((1, tb, LANES), lambda bi, p, ti: (bi, ti, o + p))
    depth = hgrn_lb.shape[0]
    return pl.pallas_call(
        functools.partial(_hgrn_kernel, layer=layer, tb=tb),
        grid=(b, nh, t // tb),
        in_specs=[spec(off), spec(off + nh), spec(off + 2 * nh), spec(off + 3 * nh),
                  pl.BlockSpec((depth, LANES), lambda bi, p, ti: (0, p)),
                  pl.BlockSpec((1, LANES), lambda bi, p, ti: (0, p))],
        out_specs=pl.BlockSpec((1, tb, LANES), lambda bi, p, ti: (bi, ti, p)),
        out_shape=jax.ShapeDtypeStruct((b, t, HGRN_WIDTH), BF16),
        scratch_shapes=[pltpu.VMEM((LANES, LANES), F32)],
        compiler_params=pltpu.CompilerParams(
            dimension_semantics=("parallel", "parallel", "arbitrary"), vmem_limit_bytes=VMEM_LIMIT),
        name="hgrn2",
    )(proj3, proj3, proj3, proj3, hgrn_lb, norm_w)


def _cast_kernel(x_ref, o_ref, *, valid_rows, rows):
    x = x_ref[...]
    if valid_rows % rows:
        r = pl.program_id(1) * rows + lax.broadcasted_iota(jnp.int32, x.shape, 1)
        x = jnp.where(r < valid_rows, x, 0.0)
    o_ref[...] = x.astype(o_ref.dtype)


def _cast_rows(w, out_rows, *, rows=256):
    nl, r, c = w.shape
    return pl.pallas_call(
        functools.partial(_cast_kernel, valid_rows=r, rows=rows),
        grid=(nl, out_rows // rows),
        in_specs=[pl.BlockSpec((1, rows, c), lambda l, i: (l, i, 0))],
        out_specs=pl.BlockSpec((1, rows, c), lambda l, i: (l, i, 0)),
        out_shape=jax.ShapeDtypeStruct((nl, out_rows, c), BF16),
        compiler_params=pltpu.CompilerParams(
            dimension_semantics=("parallel", "parallel"), vmem_limit_bytes=VMEM_LIMIT),
        name="cast_rows",
    )(w)


def _split_gu_kernel(x_ref, g_ref, u_ref, *, d_ff):
    x = x_ref[0]
    pad = jnp.zeros((x.shape[0], g_ref.shape[2] - d_ff), g_ref.dtype)
    g_ref[0] = jnp.concatenate([x[:, :d_ff].astype(g_ref.dtype), pad], axis=1)
    u_ref[0] = jnp.concatenate([x[:, d_ff:].astype(u_ref.dtype), pad], axis=1)


def _split_gu(w_gu, d_ff, ffp, *, rows=256):
    nl, d, _ = w_gu.shape
    out = jax.ShapeDtypeStruct((nl, d, ffp), BF16)
    spec = pl.BlockSpec((1, rows, ffp), lambda l, i: (l, i, 0))
    return pl.pallas_call(
        functools.partial(_split_gu_kernel, d_ff=d_ff),
        grid=(nl, d // rows),
        in_specs=[pl.BlockSpec((1, rows, 2 * d_ff), lambda l, i: (l, i, 0))],
        out_specs=(spec, spec),
        out_shape=(out, out),
        compiler_params=pltpu.CompilerParams(
            dimension_semantics=("parallel", "parallel"), vmem_limit_bytes=VMEM_LIMIT),
        name="split_gate_up",
    )(w_gu)


def kernel(x, ffn1_norm, ffn1_w_gu, ffn1_w_down, mix_norm, w_in, rwkv_mu, rwkv_w_up, rwkv_w0,
           rwkv_a_up, rwkv_a0, rwkv_g_up, rwkv_k_k, rwkv_k_a, rwkv_r_k, rwkv_ln_w, rwkv_ln_b,
           conv_w, hgrn_lb, hgrn_norm, w_out, ffn2_norm, ffn2_w_gu, ffn2_w_down, final_norm):
    b, t, d = x.shape
    depth = ffn1_norm.shape[0]
    d_ff = ffn1_w_down.shape[1]
    ffp = -(-d_ff // 512) * 512
    m = b * t
    h = x.reshape(m, d)
    row = lambda v: v.reshape(1, -1)
    final_w = row(final_norm)

    wg1, wu1 = _split_gu(ffn1_w_gu, d_ff, ffp)
    wg2, wu2 = _split_gu(ffn2_w_gu, d_ff, ffp)
    wd1 = _cast_rows(ffn1_w_down, ffp)
    wd2 = _cast_rows(ffn2_w_down, ffp)
    w_in_b = _cast_rows(w_in, d)
    w_out_b = _cast_rows(w_out, d)
    zeros_low = jnp.zeros((depth, LANES - DECAY_RANK, RWKV_WIDTH), BF16)
    ww = jnp.concatenate([rwkv_w_up.astype(BF16), zeros_low], axis=1)
    wa = jnp.concatenate([zeros_low, rwkv_a_up.astype(BF16)], axis=1)
    wgate = rwkv_g_up.astype(BF16)

    for l in range(depth):
        h = _ffn(h, row(ffn1_norm[l]), wg1, wu1, wd1, final_w, layer=l, final=False)

        proj = _proj(h, row(mix_norm[l]), w_in_b, layer=l)
        proj3 = proj.reshape(b, t, -1)
        y_r = _rwkv(proj3, row(rwkv_mu[l]), ww[l], wa[l], wgate[l],
                    row(rwkv_w0[l]), row(rwkv_a0[l]), row(rwkv_k_k[l]), row(rwkv_k_a[l]),
                    row(rwkv_r_k[l]), row(rwkv_ln_w[l]), row(rwkv_ln_b[l]))
        y_c = _conv(proj3, conv_w[l])
        y_h = _hgrn(proj3, hgrn_lb, row(hgrn_norm[l]), layer=l)
        h = _outproj(h, y_r.reshape(m, -1), y_c.reshape(m, -1), y_h.reshape(m, -1), w_out_b, layer=l)

        h = _ffn(h, row(ffn2_norm[l]), wg2, wu2, wd2, final_w, layer=l, final=(l == depth - 1))
    return h.reshape(b, t, d)
```

```python
import functools

import jax
import jax.numpy as jnp
from jax import lax
from jax.experimental import pallas as pl
from jax.experimental.pallas import tpu as pltpu

F32 = jnp.float32
BF16 = jnp.bfloat16

RWKV_HEAD_DIM = 64
RWKV_WIDTH = 1024
DECAY_RANK = 64
AAA_RANK = 64
GATE_RANK = 128
RWKV_LOW = DECAY_RANK + AAA_RANK + GATE_RANK
RWKV_SHIFT_WIDTH = 3 * RWKV_WIDTH + RWKV_LOW
RWKV_GN_EPS = 64e-5
CONV_WIDTH = 512
CONV_K = 3
HGRN_HEAD_DIM = 128
HGRN_WIDTH = 512
HGRN_NORM_EPS = 1e-5
HGRN_MIN_FORGET = 1e-30
FFN_RESIDUAL_SCALE = 0.5
RMS_EPS = 1e-6

LANES = 128
SUBLANES = 8
RWKV_CHUNK = 64
HGRN_BLOCK = 16
VMEM_LIMIT = 56 * 1024 * 1024


def _dot(a, b):
    return jnp.dot(a, b, preferred_element_type=F32)


def _dot_nt(a, b):
    return lax.dot_general(a, b, (((1,), (1,)), ((), ())), preferred_element_type=F32)


def _rms(x, w):
    return x * lax.rsqrt(jnp.mean(x * x, axis=-1, keepdims=True) + RMS_EPS) * w


def _split_bf16(x):
    hi = x.astype(BF16)
    lo = (x - hi.astype(F32)).astype(BF16)
    return hi, lo


def _ffn_kernel(x_ref, nw_ref, wg_ref, wu_ref, wd_ref, fw_ref, o_ref, xn_ref, *, final):
    j = pl.program_id(1)

    def partial_sum():
        xn = xn_ref[...]
        g = _dot(xn, wg_ref[...])
        u = _dot(xn, wu_ref[...])
        a = (g * jax.nn.sigmoid(g) * u).astype(BF16)
        return _dot(a, wd_ref[...])

    @pl.when(j == 0)
    def _():
        xn_ref[...] = _rms(x_ref[...], nw_ref[...]).astype(BF16)
        o_ref[...] = partial_sum()

    last = pl.num_programs(1) - 1

    @pl.when((j > 0) & (j < last))
    def _():
        o_ref[...] += partial_sum()

    @pl.when(j == last)
    def _():
        h = x_ref[...] + FFN_RESIDUAL_SCALE * (o_ref[...] + partial_sum())
        if final:
            h = _rms(h, fw_ref[...])
        o_ref[...] = h


def _ffn(h, norm_w, wg, wu, wd, final_w, *, layer, final, tm=1024, tf=512):
    m, d = h.shape
    ffp = wg.shape[2]
    grid = (m // tm, ffp // tf)
    assert grid[1] >= 2
    return pl.pallas_call(
        functools.partial(_ffn_kernel, final=final),
        grid=grid,
        in_specs=[
            pl.BlockSpec((tm, d), lambda i, j: (i, 0)),
            pl.BlockSpec((1, d), lambda i, j: (0, 0)),
            pl.BlockSpec((None, d, tf), lambda i, j: (layer, 0, j)),
            pl.BlockSpec((None, d, tf), lambda i, j: (layer, 0, j)),
            pl.BlockSpec((None, tf, d), lambda i, j: (layer, j, 0)),
            pl.BlockSpec((1, d), lambda i, j: (0, 0)),
        ],
        out_specs=pl.BlockSpec((tm, d), lambda i, j: (i, 0)),
        out_shape=jax.ShapeDtypeStruct((m, d), F32),
        scratch_shapes=[pltpu.VMEM((tm, d), BF16)],
        compiler_params=pltpu.CompilerParams(
            dimension_semantics=("parallel", "arbitrary"), vmem_limit_bytes=VMEM_LIMIT),
        name="ffn",
    )(h, norm_w, wg, wu, wd, final_w)


def _proj_kernel(x_ref, nw_ref, w_ref, o_ref, xn_ref):
    @pl.when(pl.program_id(1) == 0)
    def _():
        xn_ref[...] = _rms(x_ref[...], nw_ref[...]).astype(BF16)
        o_ref[...] = _dot(xn_ref[...], w_ref[...]).astype(o_ref.dtype)

    @pl.when(pl.program_id(1) > 0)
    def _():
        o_ref[...] = _dot(xn_ref[...], w_ref[...]).astype(o_ref.dtype)


def _proj(h, norm_w, w, *, layer, tm=1024, tn=2304):
    m, d = h.shape
    n = w.shape[2]
    return pl.pallas_call(
        _proj_kernel,
        grid=(m // tm, n // tn),
        in_specs=[
            pl.BlockSpec((tm, d), lambda i, j: (i, 0)),
            pl.BlockSpec((1, d), lambda i, j: (0, 0)),
            pl.BlockSpec((None, d, tn), lambda i, j: (layer, 0, j)),
        ],
        out_specs=pl.BlockSpec((tm, tn), lambda i, j: (i, j)),
        out_shape=jax.ShapeDtypeStruct((m, n), BF16),
        scratch_shapes=[pltpu.VMEM((tm, d), BF16)],
        compiler_params=pltpu.CompilerParams(
            dimension_semantics=("parallel", "arbitrary"), vmem_limit_bytes=VMEM_LIMIT),
        name="proj",
    )(h, norm_w, w)


def _outproj_kernel(h_ref, yr_ref, yc_ref, yh_ref, wr_ref, wc_ref, wh_ref, o_ref):
    acc = _dot(yr_ref[...], wr_ref[...])
    acc += _dot(yc_ref[...], wc_ref[...])
    acc += _dot(yh_ref[...], wh_ref[...])
    o_ref[...] = h_ref[...] + acc


def _outproj(h, yr, yc, yh, wo, *, layer, tm=512):
    m, d = h.shape
    nr, ncv, nh = yr.shape[1], yc.shape[1], yh.shape[1]
    row = lambda i: (i, 0)
    return pl.pallas_call(
        _outproj_kernel,
        grid=(m // tm,),
        in_specs=[
            pl.BlockSpec((tm, d), row),
            pl.BlockSpec((tm, yr.shape[1]), row),
            pl.BlockSpec((tm, yc.shape[1]), row),
            pl.BlockSpec((tm, yh.shape[1]), row),
            pl.BlockSpec((None, nr, d), lambda i: (layer, 0, 0)),
            pl.BlockSpec((None, ncv, d), lambda i: (layer, nr // ncv, 0)),
            pl.BlockSpec((None, nh, d), lambda i: (layer, (nr + ncv) // nh, 0)),
        ],
        out_specs=pl.BlockSpec((tm, d), row),
        out_shape=jax.ShapeDtypeStruct((m, d), F32),
        compiler_params=pltpu.CompilerParams(
            dimension_semantics=("parallel",), vmem_limit_bytes=VMEM_LIMIT),
        name="outproj",
    )(h, yr, yc, yh, wo, wo, wo)


def _prev_rows(x, carry, k=1):
    rows = lax.broadcasted_iota(jnp.int32, (SUBLANES, x.shape[1]), 0)
    prev = pltpu.roll(x, k, 0)
    head = jnp.where(rows < k, pltpu.roll(carry, k, 0), prev[:SUBLANES])
    return jnp.concatenate([head, prev[SUBLANES:]], axis=0)


def _seg_sum(x, ones_bd):
    return _dot(x.astype(BF16), ones_bd)


def _rwkv_prepare(lw, r, k2, v, av, bv, c, tick):
    n = 2 * RWKV_CHUNK
    m0, m1 = c["m0"], c["m1"]
    stack = lambda x: jnp.concatenate([x * m0, x * m1], axis=0)
    each = lambda f, *ls: [f(*xs) for xs in zip(*ls)]

    def cumsum(x):
        hi, lo = _split_bf16(x)
        gcat = _dot(c["tri"], jnp.concatenate([hi, lo], axis=1))
        return gcat[:, :LANES] + gcat[:, LANES:]

    g = each(cumsum, lw)
    tick()
    gl = [x[RWKV_CHUNK - 1:RWKV_CHUNK, :] for x in g]
    e_g = each(jnp.exp, g)
    e_ng = each(lambda x: jnp.exp(-x), g)
    e_d = each(lambda x, y: jnp.exp(x - y), gl, g)
    ats = each(lambda a_, g_, l_: stack(a_ * jnp.exp(g_ - l_)), av, g, lw)
    rts = each(lambda r_, e_: stack(r_ * e_), r, e_g)
    bts = each(lambda b_, e_: stack(b_ * e_), bv, e_ng)
    kts = each(lambda k_, e_: stack(k_ * e_), k2, e_ng)
    bes = each(lambda b_, e_: stack(b_ * e_).astype(BF16), bv, e_d)
    kes = each(lambda k_, e_: stack(k_ * e_).astype(BF16), k2, e_d)
    vs = each(stack, v)
    vsb = [x.astype(BF16) for x in vs]

    p = each(lambda a_, r_, b_, k_: _dot_nt(jnp.concatenate([a_, r_], axis=0).astype(BF16),
                                            jnp.concatenate([b_, k_], axis=0).astype(BF16)),
             ats, rts, bts, kts)
    a_ab = [jnp.where(c["strict"], x[:n, :n], 0.0) for x in p]
    a_kr = [jnp.concatenate([jnp.where(c["strict"], x[:n, n:], 0.0),
                             jnp.where(c["incl"], x[n:, n:], 0.0)], axis=0).astype(BF16) for x in p]
    a_rb = [jnp.where(c["incl"], x[n:, :n], 0.0).astype(BF16) for x in p]
    tick()

    h = RWKV_CHUNK
    side = lambda x: x[:h] + x[h:]
    diag = lambda x: jnp.concatenate([x, x], axis=0) * c["own"]
    a_sbs = [side(x) for x in a_ab]
    inv = [c["eye"] + x for x in a_sbs]
    q = [_dot(x.astype(BF16), y.astype(BF16)) for x, y in zip(a_sbs, a_ab)]
    tick()
    for _ in range(4):
        z = each(lambda i_, q_: _dot(jnp.concatenate([i_, q_], axis=0).astype(BF16), diag(q_).astype(BF16)), inv, q)
        inv = each(lambda i_, z_: i_ + z_[:h], inv, z)
        q = [x[h:] for x in z]
        tick()
    inv = each(lambda i_, q_: diag(i_ + _dot(i_.astype(BF16), diag(q_).astype(BF16))), inv, q)
    tick()

    akv_rkv = each(_dot, a_kr, vsb)
    tick()
    tt = each(lambda i_, a_, x_: _dot(i_.astype(BF16), jnp.concatenate([a_, x_[:n]], axis=1).astype(BF16)),
              inv, ats, akv_rkv)
    tick()
    rr = each(lambda a_, t_: _dot(a_, t_.astype(BF16)), a_rb, tt)
    rq = each(lambda r_, x_: (r_ + x_[:, :LANES]).astype(BF16), rts, rr)
    oc = each(lambda x_, y_: x_[:, LANES:] + y_[n:], rr, akv_rkv)
    bb = each(lambda t_, b_: _dot(t_.T.astype(BF16), b_), tt, bes)
    bta = [x[:LANES].astype(BF16) for x in bb]
    nct = each(lambda x_, v_, k_: x_[LANES:] + _dot(v_.T.astype(BF16), k_), bb, vs, kes)
    e_gl = each(jnp.exp, gl)
    return rq, oc, bta, nct, e_gl


def _rwkv_kernel(r_ref, k_ref, v_ref, low_ref, mur_ref, muk_ref, muv_ref, mul_ref,
                 ww_ref, wa_ref, wg_ref, w0_ref, a0_ref, kk_ref, ka_ref, rk_ref, lnw_ref, lnb_ref,
                 y_ref, cr_ref, ck_ref, cv_ref, cl_ref, st_ref, feat_ref,
                 rq_ref, oc_ref, bta_ref, nct_ref, egl_ref, bonus_ref, gate_ref, *, tb, nt):
    s = pl.program_id(0)
    nc = tb // RWKV_CHUNK

    @pl.when(s == 0)
    def _():
        for ref in (cr_ref, ck_ref, cv_ref, cl_ref, st_ref, feat_ref, rq_ref, oc_ref, bta_ref, nct_ref, egl_ref,
                    bonus_ref, gate_ref):
            ref[...] = jnp.zeros_like(ref)

    lane = lax.broadcasted_iota(jnp.int32, (1, LANES), 1)
    m0 = (lane < RWKV_HEAD_DIM).astype(F32)
    m1 = 1.0 - m0
    ri = lax.broadcasted_iota(jnp.int32, (LANES, LANES), 0)
    ci = lax.broadcasted_iota(jnp.int32, (LANES, LANES), 1)
    same = (ri < RWKV_HEAD_DIM) == (ci < RWKV_HEAD_DIM)
    ones_bd = same.astype(F32).astype(BF16)
    ti = lax.broadcasted_iota(jnp.int32, (RWKV_CHUNK, RWKV_CHUNK), 0)
    si = lax.broadcasted_iota(jnp.int32, (RWKV_CHUNK, RWKV_CHUNK), 1)
    tw = lax.broadcasted_iota(jnp.int32, (RWKV_CHUNK, LANES), 0)
    sw = lax.broadcasted_iota(jnp.int32, (RWKV_CHUNK, LANES), 1) & (RWKV_HEAD_DIM - 1)
    consts = dict(
        m0=m0, m1=m1, own=same.astype(F32),
        strict=same & (ri > ci), incl=same & (ri >= ci),
        eye=(tw == sw).astype(F32),
        tri=(ti >= si).astype(F32).astype(BF16),
    )

    rd = lax.rem(s, 2)
    rd3 = lax.rem(s + 1, 3)
    seq_start = lax.rem(jnp.maximum(s - 2, 0), nt) == 0
    state = [st_ref[...] * jnp.where(seq_start, 0.0, 1.0)]
    outs = []
    inv_n = 1.0 / RWKV_HEAD_DIM
    own = same.astype(F32)

    def recurrence_step():
        i = len(outs)
        if i == nc:
            return
        st = state[0]
        stb = st.astype(BF16)
        o_stack = _dot_nt(rq_ref[rd, i], stb) + oc_ref[rd, i]
        state[0] = st * egl_ref[rd, i][0:1, :] + _dot(stb, bta_ref[rd, i]) + nct_ref[rd, i]
        mean = jnp.sum(o_stack, axis=-1, keepdims=True) * inv_n
        cen = (o_stack - mean) * own
        var = jnp.sum(cen * cen, axis=-1, keepdims=True) * inv_n
        on = cen * lax.rsqrt(var + RWKV_GN_EPS)
        outs.append(on[:RWKV_CHUNK] + on[RWKV_CHUNK:])

    wf = lax.rem(s, 2)
    wf3 = lax.rem(s, 3)
    keep = jnp.where(lax.rem(jnp.minimum(s, pl.num_programs(0) - 3), nt) == 0, 0.0, 1.0)
    f = {}

    def lerp(x_ref, c_ref, mu_ref):
        x = x_ref[0].astype(F32)
        prev = _prev_rows(x, c_ref[...] * keep)
        c_ref[...] = x[tb - SUBLANES:tb, :]
        return x + (prev - x) * mu_ref[...]

    def feat_low():
        low = lerp(low_ref, cl_ref, mul_ref)
        f["wa_in"] = low[:, :LANES]
        gate_ref[wf3] = _dot(jax.nn.sigmoid(low[:, LANES:]).astype(BF16), wg_ref[...])

    def feat_decay():
        xw = w0_ref[...] + _dot(jnp.tanh(f["wa_in"]).astype(BF16), ww_ref[...])
        feat_ref[wf, 0] = -jnp.exp(jnp.float32(-0.5)) * jax.nn.sigmoid(xw)
        f["a"] = jax.nn.sigmoid(a0_ref[...] + _dot(f["wa_in"].astype(BF16), wa_ref[...]))

    def feat_k():
        k = lerp(k_ref, ck_ref, muk_ref)
        kk = k * kk_ref[...]
        kk = kk * lax.rsqrt(jnp.maximum(_seg_sum(kk * kk, ones_bd), 1e-24))
        f["k2"] = k * (1.0 + (f["a"] - 1.0) * ka_ref[...])
        feat_ref[wf, 2] = f["k2"]
        feat_ref[wf, 4] = -kk
        feat_ref[wf, 5] = kk * f["a"]

    def feat_rv():
        r = lerp(r_ref, cr_ref, mur_ref)
        v = lerp(v_ref, cv_ref, muv_ref)
        feat_ref[wf, 1] = r
        feat_ref[wf, 3] = v
        bonus_ref[wf3] = _seg_sum(r * f["k2"] * rk_ref[...], ones_bd) * v

    pieces = [feat_low, feat_decay, feat_k, feat_rv]

    def tick():
        if pieces:
            pieces.pop(0)()
        recurrence_step()

    rf = lax.rem(s + 1, 2)
    wr = lax.rem(s + 1, 2)
    chunks = lambda j: [feat_ref[rf, j, i * RWKV_CHUNK:(i + 1) * RWKV_CHUNK, :] for i in range(nc)]
    rq, oc, bta, nct, e_gl = _rwkv_prepare(chunks(0), chunks(1), chunks(2), chunks(3), chunks(4), chunks(5),
                                           consts, tick)
    while pieces or len(outs) < nc:
        tick()

    st_ref[...] = state[0]
    o = jnp.concatenate(outs, axis=0) * lnw_ref[...] + lnb_ref[...]
    y_ref[0] = ((o + bonus_ref[rd3]) * gate_ref[rd3]).astype(y_ref.dtype)

    for i in range(nc):
        rq_ref[wr, i] = rq[i]
        oc_ref[wr, i] = oc[i]
        bta_ref[wr, i] = bta[i]
        nct_ref[wr, i] = nct[i]
        egl_ref[wr, i] = jnp.broadcast_to(e_gl[i], (SUBLANES, LANES))


def _rwkv(proj3, mu, ww, wa, wg, w0, a0, k_k, k_a, r_k, ln_w, ln_b, *, tb=512):
    b, t, _ = proj3.shape
    npairs = RWKV_WIDTH // LANES
    nt = t // tb
    nc = tb // RWKV_CHUNK
    nblocks = b * npairs * nt
    off_k = RWKV_WIDTH // LANES
    off_v = 2 * RWKV_WIDTH // LANES
    off_low = 3 * RWKV_WIDTH // RWKV_LOW

    def where(item):
        return item // (nt * npairs), item % nt, (item // nt) % npairs

    cur = lambda s: where(jnp.minimum(s, nblocks - 1))
    lag = lambda s: where(jnp.maximum(s - 2, 0))
    act = lambda off: pl.BlockSpec((1, tb, LANES), lambda s: (cur(s)[0], cur(s)[1], off + cur(s)[2]))
    vec = lambda off: pl.BlockSpec((1, LANES), lambda s: (0, off + cur(s)[2]))
    lagvec = pl.BlockSpec((1, LANES), lambda s: (0, lag(s)[2]))
    wspec = pl.BlockSpec((LANES, LANES), lambda s: (0, cur(s)[2]))
    return pl.pallas_call(
        functools.partial(_rwkv_kernel, tb=tb, nt=nt),
        grid=(nblocks + 2,),
        in_specs=[
            act(0), act(off_k), act(off_v),
            pl.BlockSpec((1, tb, RWKV_LOW), lambda s: (cur(s)[0], cur(s)[1], off_low)),
            vec(0), vec(off_k), vec(off_v),
            pl.BlockSpec((1, RWKV_LOW), lambda s: (0, off_low)),
            wspec, wspec, wspec,
            vec(0), vec(0), vec(0), vec(0), vec(0), lagvec, lagvec,
        ],
        out_specs=pl.BlockSpec((1, tb, LANES), lambda s: lag(s)),
        out_shape=jax.ShapeDtypeStruct((b, t, RWKV_WIDTH), BF16),
        scratch_shapes=[pltpu.VMEM((SUBLANES, LANES), F32), pltpu.VMEM((SUBLANES, LANES), F32),
                        pltpu.VMEM((SUBLANES, LANES), F32), pltpu.VMEM((SUBLANES, RWKV_LOW), F32),
                        pltpu.VMEM((LANES, LANES), F32),
                        pltpu.VMEM((2, 6, tb, LANES), F32),
                        pltpu.VMEM((2, nc, LANES, LANES), BF16), pltpu.VMEM((2, nc, LANES, LANES), F32),
                        pltpu.VMEM((2, nc, LANES, LANES), BF16), pltpu.VMEM((2, nc, LANES, LANES), F32),
                        pltpu.VMEM((2, nc, SUBLANES, LANES), F32),
                        pltpu.VMEM((3, tb, LANES), F32), pltpu.VMEM((3, tb, LANES), F32)],
        compiler_params=pltpu.CompilerParams(
            dimension_semantics=("arbitrary",), vmem_limit_bytes=VMEM_LIMIT),
        name="rwkv7",
    )(proj3, proj3, proj3, proj3, mu, mu, mu, mu, ww, wa, wg, w0, a0, k_k, k_a, r_k, ln_w, ln_b)


def _conv_kernel(c_ref, x_ref, b_ref, w_ref, y_ref, carry_ref):
    @pl.when(pl.program_id(2) == 0)
    def _():
        carry_ref[...] = jnp.zeros_like(carry_ref)

    z = c_ref[0].astype(F32) * x_ref[0].astype(F32)
    tb = z.shape[0]
    w = w_ref[...]
    y = z * w[CONV_K - 1:CONV_K, :]
    for j in range(1, CONV_K):
        y = y + _prev_rows(z, carry_ref[...], j) * w[CONV_K - 1 - j:CONV_K - j, :]
    carry_ref[...] = z[tb - SUBLANES:tb, :]
    y_ref[0] = (b_ref[0].astype(F32) * y).astype(y_ref.dtype)


def _conv(proj3, conv_w, *, tb=1024, cw=2 * LANES):
    b, t, _ = proj3.shape
    tb = min(tb, t)
    nblk = CONV_WIDTH // cw
    off = RWKV_SHIFT_WIDTH // cw
    spec = lambda o: pl.BlockSpec((1, tb, cw), lambda bi, p, ti: (bi, ti, o + p))
    return pl.pallas_call(
        _conv_kernel,
        grid=(b, nblk, t // tb),
        in_specs=[spec(off), spec(off + nblk), spec(off + 2 * nblk),
                  pl.BlockSpec((CONV_K, cw), lambda bi, p, ti: (0, p))],
        out_specs=pl.BlockSpec((1, tb, cw), lambda bi, p, ti: (bi, ti, p)),
        out_shape=jax.ShapeDtypeStruct((b, t, CONV_WIDTH), BF16),
        scratch_shapes=[pltpu.VMEM((SUBLANES, cw), F32)],
        compiler_params=pltpu.CompilerParams(
            dimension_semantics=("parallel", "parallel", "arbitrary"), vmem_limit_bytes=VMEM_LIMIT),
        name="shortconv",
    )(proj3, proj3, proj3, conv_w)


def _hgrn_kernel(q_ref, f_ref, i_ref, og_ref, lb_ref, nw_ref, y_ref, st_ref, *, layer, tb):
    @pl.when(pl.program_id(2) == 0)
    def _():
        st_ref[...] = jnp.zeros_like(st_ref)

    lbp = lb_ref[...]
    e = jnp.exp(lbp - jnp.max(lbp, axis=0, keepdims=True))
    pr = e / jnp.sum(e, axis=0, keepdims=True)
    lb = jnp.zeros((1, LANES), F32)
    for l in range(1, layer + 1):
        lb = lb + pr[l:l + 1, :]

    qr = q_ref[0].astype(F32)
    q = qr * jax.nn.sigmoid(qr)
    f = f_ref[0].astype(F32)
    sg = jax.nn.sigmoid(f)
    forget = lb + (1.0 - lb) * sg
    logf = jnp.log(jnp.maximum(forget, HGRN_MIN_FORGET))
    kx = (1.0 - lb) * (1.0 - sg)
    iv = i_ref[0].astype(F32)

    n = HGRN_BLOCK
    nb = tb // n
    ti = lax.broadcasted_iota(jnp.int32, (tb, tb), 0)
    si = lax.broadcasted_iota(jnp.int32, (tb, tb), 1)
    same = (ti ^ si) < n
    sums = jnp.concatenate([(same & (ti >= si)).astype(F32).astype(BF16), same.astype(F32).astype(BF16)], axis=0)
    hi, lo = _split_bf16(logf)
    gg = _dot(sums, jnp.concatenate([hi, lo], axis=1))
    g = gg[:tb, :LANES] + gg[:tb, LANES:]
    gl = gg[tb:, :LANES] + gg[tb:, LANES:]
    qd = (q * jnp.exp(g)).astype(BF16)
    kd = (kx * jnp.exp(gl - g)).astype(BF16)
    dec = jnp.exp(gl)
    trow = lax.broadcasted_iota(jnp.int32, (n, 1), 0)
    blocks = lambda x: [x[i * n:(i + 1) * n] for i in range(nb)]
    g2 = g * 1.4426950408889634
    gb, hb, qb, ib, qdb, kdb = (blocks(x) for x in (g2, g2 - jnp.log2(kx), q, iv, qd, kd))

    upd = [_dot(i_.T.astype(BF16), k_) for i_, k_ in zip(ib, kdb)]
    st = st_ref[...]
    sts = []
    for i in range(nb):
        sts.append(st.astype(BF16))
        st = st * dec[i * n:i * n + 1, :] + upd[i]
    st_ref[...] = st
    outs = [_dot_nt(q_, s_) for q_, s_ in zip(qdb, sts)]
    lane = lax.broadcasted_iota(jnp.int32, (n, LANES), 1)
    scores = [jnp.zeros((n, LANES), F32) for _ in range(nb)]
    for s in range(n):
        for i in range(nb):
            w = jnp.sum(qb[i] * jnp.exp2(gb[i] - hb[i][s:s + 1, :]), axis=-1, keepdims=True)
            scores[i] = jnp.where(lane == s, w, scores[i])
    causal = lane <= trow
    for i in range(nb):
        sc = jnp.where(causal, scores[i], 0.0)[:, :n].astype(BF16)
        outs[i] = outs[i] + _dot(sc, ib[i].astype(BF16))
    o = jnp.concatenate(outs, axis=0)
    o = o * lax.rsqrt(jnp.mean(o * o, axis=-1, keepdims=True) + HGRN_NORM_EPS) * nw_ref[...]
    og = og_ref[0].astype(F32)
    y_ref[0] = (o * (og * jax.nn.sigmoid(og))).astype(y_ref.dtype)


def _hgrn(proj3, hgrn_lb, norm_w, *, layer, tb=512):
    b, t, _ = proj3.shape
    tb = min(tb, t)
    nh = HGRN_WIDTH // HGRN_HEAD_DIM
    off = (RWKV_SHIFT_WIDTH + 3 * CONV_WIDTH) // LANES
    spec = lambda o: pl.BlockSpec((1, tb, LANES), lambda bi, p, ti: (bi, ti, o + p))
    depth = hgrn_lb.shape[0]
    return pl.pallas_call(
        functools.partial(_hgrn_kernel, layer=layer, tb=tb),
        grid=(b, nh, t // tb),
        in_specs=[spec(off), spec(off + nh), spec(off + 2 * nh), spec(off + 3 * nh),
                  pl.BlockSpec((depth, LANES), lambda bi, p, ti: (0, p)),
                  pl.BlockSpec((1, LANES), lambda bi, p, ti: (0, p))],
        out_specs=pl.BlockSpec((1, tb, LANES), lambda bi, p, ti: (bi, ti, p)),
        out_shape=jax.ShapeDtypeStruct((b, t, HGRN_WIDTH), BF16),
        scratch_shapes=[pltpu.VMEM((LANES, LANES), F32)],
        compiler_params=pltpu.CompilerParams(
            dimension_semantics=("parallel", "parallel", "arbitrary"), vmem_limit_bytes=VMEM_LIMIT),
        name="hgrn2",
    )(proj3, proj3, proj3, proj3, hgrn_lb, norm_w)


def _cast_kernel(x_ref, o_ref, *, valid_rows, rows):
    x = x_ref[...]
    if valid_rows % rows:
        r = pl.program_id(1) * rows + lax.broadcasted_iota(jnp.int32, x.shape, 1)
        x = jnp.where(r < valid_rows, x, 0.0)
    o_ref[...] = x.astype(o_ref.dtype)


def _cast_rows(w, out_rows, *, rows=256):
    nl, r, c = w.shape
    return pl.pallas_call(
        functools.partial(_cast_kernel, valid_rows=r, rows=rows),
        grid=(nl, out_rows // rows),
        in_specs=[pl.BlockSpec((1, rows, c), lambda l, i: (l, i, 0))],
        out_specs=pl.BlockSpec((1, rows, c), lambda l, i: (l, i, 0)),
        out_shape=jax.ShapeDtypeStruct((nl, out_rows, c), BF16),
        compiler_params=pltpu.CompilerParams(
            dimension_semantics=("parallel", "parallel"), vmem_limit_bytes=VMEM_LIMIT),
        name="cast_rows",
    )(w)


def _split_gu_kernel(x_ref, g_ref, u_ref, *, d_ff):
    x = x_ref[0]
    pad = jnp.zeros((x.shape[0], g_ref.shape[2] - d_ff), g_ref.dtype)
    g_ref[0] = jnp.concatenate([x[:, :d_ff].astype(g_ref.dtype), pad], axis=1)
    u_ref[0] = jnp.concatenate([x[:, d_ff:].astype(u_ref.dtype), pad], axis=1)


def _split_gu(w_gu, d_ff, ffp, *, rows=256):
    nl, d, _ = w_gu.shape
    out = jax.ShapeDtypeStruct((nl, d, ffp), BF16)
    spec = pl.BlockSpec((1, rows, ffp), lambda l, i: (l, i, 0))
    return pl.pallas_call(
        functools.partial(_split_gu_kernel, d_ff=d_ff),
        grid=(nl, d // rows),
        in_specs=[pl.BlockSpec((1, rows, 2 * d_ff), lambda l, i: (l, i, 0))],
        out_specs=(spec, spec),
        out_shape=(out, out),
        compiler_params=pltpu.CompilerParams(
            dimension_semantics=("parallel", "parallel"), vmem_limit_bytes=VMEM_LIMIT),
        name="split_gate_up",
    )(w_gu)


def kernel(x, ffn1_norm, ffn1_w_gu, ffn1_w_down, mix_norm, w_in, rwkv_mu, rwkv_w_up, rwkv_w0,
           rwkv_a_up, rwkv_a0, rwkv_g_up, rwkv_k_k, rwkv_k_a, rwkv_r_k, rwkv_ln_w, rwkv_ln_b,
           conv_w, hgrn_lb, hgrn_norm, w_out, ffn2_norm, ffn2_w_gu, ffn2_w_down, final_norm):
    b, t, d = x.shape
    depth = ffn1_norm.shape[0]
    d_ff = ffn1_w_down.shape[1]
    ffp = -(-d_ff // 512) * 512
    m = b * t
    h = x.reshape(m, d)
    row = lambda v: v.reshape(1, -1)
    final_w = row(final_norm)

    wg1, wu1 = _split_gu(ffn1_w_gu, d_ff, ffp)
    wg2, wu2 = _split_gu(ffn2_w_gu, d_ff, ffp)
    wd1 = _cast_rows(ffn1_w_down, ffp)
    wd2 = _cast_rows(ffn2_w_down, ffp)
    w_in_b = _cast_rows(w_in, d)
    w_out_b = _cast_rows(w_out, d)
    zeros_low = jnp.zeros((depth, LANES - DECAY_RANK, RWKV_WIDTH), BF16)
    ww = jnp.concatenate([rwkv_w_up.astype(BF16), zeros_low], axis=1)
    wa = jnp.concatenate([zeros_low, rwkv_a_up.astype(BF16)], axis=1)
    wgate = rwkv_g_up.astype(BF16)

    for l in range(depth):
        h = _ffn(h, row(ffn1_norm[l]), wg1, wu1, wd1, final_w, layer=l, final=False)

        proj = _proj(h, row(mix_norm[l]), w_in_b, layer=l)
        proj3 = proj.reshape(b, t, -1)
        y_r = _rwkv(proj3, row(rwkv_mu[l]), ww[l], wa[l], wgate[l],
                    row(rwkv_w0[l]), row(rwkv_a0[l]), row(rwkv_k_k[l]), row(rwkv_k_a[l]),
                    row(rwkv_r_k[l]), row(rwkv_ln_w[l]), row(rwkv_ln_b[l]))
        y_c = _conv(proj3, conv_w[l])
        y_h = _hgrn(proj3, hgrn_lb, row(hgrn_norm[l]), layer=l)
        h = _outproj(h, y_r.reshape(m, -1), y_c.reshape(m, -1), y_h.reshape(m, -1), w_out_b, layer=l)

        h = _ffn(h, row(ffn2_norm[l]), wg2, wu2, wd2, final_w, layer=l, final=(l == depth - 1))
    return h.reshape(b, t, d)
```

```python
import functools

import jax
import jax.numpy as jnp
from jax import lax
from jax.experimental import pallas as pl
from jax.experimental.pallas import tpu as pltpu

F32 = jnp.float32
BF16 = jnp.bfloat16

RWKV_HEAD_DIM = 64
RWKV_WIDTH = 1024
DECAY_RANK = 64
AAA_RANK = 64
GATE_RANK = 128
RWKV_LOW = DECAY_RANK + AAA_RANK + GATE_RANK
RWKV_SHIFT_WIDTH = 3 * RWKV_WIDTH + RWKV_LOW
RWKV_GN_EPS = 64e-5
CONV_WIDTH = 512
CONV_K = 3
HGRN_HEAD_DIM = 128
HGRN_WIDTH = 512
HGRN_NORM_EPS = 1e-5
HGRN_MIN_FORGET = 1e-30
FFN_RESIDUAL_SCALE = 0.5
RMS_EPS = 1e-6

LANES = 128
SUBLANES = 8
RWKV_CHUNK = 64
HGRN_BLOCK = 16
VMEM_LIMIT = 60000 * 1024


def _dot(a, b):
    return jnp.dot(a, b, preferred_element_type=F32)


def _dot_nt(a, b):
    return lax.dot_general(a, b, (((1,), (1,)), ((), ())), preferred_element_type=F32)


def _rms(x, w):
    return x * lax.rsqrt(jnp.mean(x * x, axis=-1, keepdims=True) + RMS_EPS) * w


def _split_bf16(x):
    hi = x.astype(BF16)
    lo = (x - hi.astype(F32)).astype(BF16)
    return hi, lo


def _ffn_kernel(x_ref, nw_ref, wg_ref, wu_ref, wd_ref, fw_ref, o_ref, xn_ref, *, final, tail_rows):
    j = pl.program_id(1)

    def partial_sum(tail=False):
        xn = xn_ref[...]
        g = _dot(xn, wg_ref[...])
        u = _dot(xn, wu_ref[...])
        a = (g * jax.nn.sigmoid(g) * u).astype(BF16)
        wd = wd_ref[...]
        if tail and tail_rows != wd.shape[0]:
            wd = jnp.where(lax.broadcasted_iota(jnp.int32, wd.shape, 0) < tail_rows, wd, 0.0)
        return _dot(a, wd.astype(BF16))

    @pl.when(j == 0)
    def _():
        xn_ref[...] = _rms(x_ref[...], nw_ref[...]).astype(BF16)
        o_ref[...] = partial_sum()

    last = pl.num_programs(1) - 1

    @pl.when((j > 0) & (j < last))
    def _():
        o_ref[...] += partial_sum()

    @pl.when(j == last)
    def _():
        h = x_ref[...] + FFN_RESIDUAL_SCALE * (o_ref[...] + partial_sum(tail=True))
        if final:
            h = _rms(h, fw_ref[...])
        o_ref[...] = h


def _ffn(h, norm_w, wg, wu, wd, final_w, *, layer, final, wd_layer=None, tm=1024, tf=512):
    wd_layer = layer if wd_layer is None else wd_layer
    m, d = h.shape
    ffp = wg.shape[2]
    grid = (m // tm, ffp // tf)
    assert grid[1] >= 2
    return pl.pallas_call(
        functools.partial(_ffn_kernel, final=final, tail_rows=wd.shape[1] - (grid[1] - 1) * tf),
        grid=grid,
        in_specs=[
            pl.BlockSpec((tm, d), lambda i, j: (i, 0)),
            pl.BlockSpec((1, d), lambda i, j: (0, 0)),
            pl.BlockSpec((None, d, tf), lambda i, j: (layer, 0, j)),
            pl.BlockSpec((None, d, tf), lambda i, j: (layer, 0, j)),
            pl.BlockSpec((None, tf, d), lambda i, j: (wd_layer, j, 0)),
            pl.BlockSpec((1, d), lambda i, j: (0, 0)),
        ],
        out_specs=pl.BlockSpec((tm, d), lambda i, j: (i, 0)),
        out_shape=jax.ShapeDtypeStruct((m, d), F32),
        scratch_shapes=[pltpu.VMEM((tm, d), BF16)],
        compiler_params=pltpu.CompilerParams(
            dimension_semantics=("parallel", "arbitrary"), vmem_limit_bytes=VMEM_LIMIT),
        name="ffn",
    )(h, norm_w, wg, wu, wd, final_w)


def _proj_kernel(x_ref, nw_ref, w_ref, o_ref, xn_ref):
    @pl.when(pl.program_id(1) == 0)
    def _():
        xn_ref[...] = _rms(x_ref[...], nw_ref[...]).astype(BF16)
        o_ref[...] = _dot(xn_ref[...], w_ref[...]).astype(o_ref.dtype)

    @pl.when(pl.program_id(1) > 0)
    def _():
        o_ref[...] = _dot(xn_ref[...], w_ref[...]).astype(o_ref.dtype)


def _proj(h, norm_w, w, *, layer, tm=1024, tn=2304):
    m, d = h.shape
    n = w.shape[2]
    return pl.pallas_call(
        _proj_kernel,
        grid=(m // tm, n // tn),
        in_specs=[
            pl.BlockSpec((tm, d), lambda i, j: (i, 0)),
            pl.BlockSpec((1, d), lambda i, j: (0, 0)),
            pl.BlockSpec((None, d, tn), lambda i, j: (layer, 0, j)),
        ],
        out_specs=pl.BlockSpec((tm, tn), lambda i, j: (i, j)),
        out_shape=jax.ShapeDtypeStruct((m, n), BF16),
        scratch_shapes=[pltpu.VMEM((tm, d), BF16)],
        compiler_params=pltpu.CompilerParams(
            dimension_semantics=("parallel", "arbitrary"), vmem_limit_bytes=VMEM_LIMIT),
        name="proj",
    )(h, norm_w, w)


def _outproj_kernel(h_ref, yr_ref, yc_ref, yh_ref, wr_ref, wc_ref, wh_ref, o_ref):
    acc = _dot(yr_ref[...], wr_ref[...])
    acc += _dot(yc_ref[...], wc_ref[...])
    acc += _dot(yh_ref[...], wh_ref[...])
    o_ref[...] = h_ref[...] + acc


def _outproj(h, yr, yc, yh, wo, *, layer, tm=512):
    m, d = h.shape
    nr, ncv, nh = yr.shape[1], yc.shape[1], yh.shape[1]
    row = lambda i: (i, 0)
    return pl.pallas_call(
        _outproj_kernel,
        grid=(m // tm,),
        in_specs=[
            pl.BlockSpec((tm, d), row),
            pl.BlockSpec((tm, yr.shape[1]), row),
            pl.BlockSpec((tm, yc.shape[1]), row),
            pl.BlockSpec((tm, yh.shape[1]), row),
            pl.BlockSpec((None, nr, d), lambda i: (layer, 0, 0)),
            pl.BlockSpec((None, ncv, d), lambda i: (layer, nr // ncv, 0)),
            pl.BlockSpec((None, nh, d), lambda i: (layer, (nr + ncv) // nh, 0)),
        ],
        out_specs=pl.BlockSpec((tm, d), row),
        out_shape=jax.ShapeDtypeStruct((m, d), F32),
        compiler_params=pltpu.CompilerParams(
            dimension_semantics=("parallel",), vmem_limit_bytes=VMEM_LIMIT),
        name="outproj",
    )(h, yr, yc, yh, wo, wo, wo)


def _prev_rows(x, carry, k=1):
    rows = lax.broadcasted_iota(jnp.int32, (SUBLANES, x.shape[1]), 0)
    prev = pltpu.roll(x, k, 0)
    head = jnp.where(rows < k, pltpu.roll(carry, k, 0), prev[:SUBLANES])
    return jnp.concatenate([head, prev[SUBLANES:]], axis=0)


def _seg_sum(x, ones_bd):
    return _dot(x.astype(BF16), ones_bd)


def _rwkv_prepare(lw, r, k2, v, av, bv, c, tick):
    n = 2 * RWKV_CHUNK
    m0, m1 = c["m0"], c["m1"]
    stack = lambda x: jnp.concatenate([x * m0, x * m1], axis=0)
    each = lambda f, *ls: [f(*xs) for xs in zip(*ls)]

    def cumsum(x):
        hi, lo = _split_bf16(x)
        gcat = _dot(c["tri"], jnp.concatenate([hi, lo], axis=1))
        return gcat[:, :LANES] + gcat[:, LANES:]

    g = each(cumsum, lw)
    tick()
    gl = [x[RWKV_CHUNK - 1:RWKV_CHUNK, :] for x in g]
    e_g = each(jnp.exp, g)
    e_ng = each(lambda x: jnp.exp(-x), g)
    e_d = each(lambda x, y: jnp.exp(x - y), gl, g)
    ats = each(lambda a_, g_, l_: stack(a_ * jnp.exp(g_ - l_)), av, g, lw)
    rts = each(lambda r_, e_: stack(r_ * e_), r, e_g)
    bts = each(lambda b_, e_: stack(b_ * e_), bv, e_ng)
    kts = each(lambda k_, e_: stack(k_ * e_), k2, e_ng)
    bes = each(lambda b_, e_: stack(b_ * e_).astype(BF16), bv, e_d)
    kes = each(lambda k_, e_: stack(k_ * e_).astype(BF16), k2, e_d)
    vs = each(stack, v)
    vsb = [x.astype(BF16) for x in vs]

    p = each(lambda a_, r_, b_, k_: _dot_nt(jnp.concatenate([a_, r_], axis=0).astype(BF16),
                                            jnp.concatenate([b_, k_], axis=0).astype(BF16)),
             ats, rts, bts, kts)
    a_ab = [jnp.where(c["strict"], x[:n, :n], 0.0) for x in p]
    a_kr = [jnp.concatenate([jnp.where(c["strict"], x[:n, n:], 0.0),
                             jnp.where(c["incl"], x[n:, n:], 0.0)], axis=0).astype(BF16) for x in p]
    a_rb = [jnp.where(c["incl"], x[n:, :n], 0.0).astype(BF16) for x in p]
    tick()

    h = RWKV_CHUNK
    side = lambda x: x[:h] + x[h:]
    diag = lambda x: jnp.concatenate([x, x], axis=0) * c["own"]
    a_sbs = [side(x) for x in a_ab]
    inv = [c["eye"] + x for x in a_sbs]
    q = [_dot(x.astype(BF16), y.astype(BF16)) for x, y in zip(a_sbs, a_ab)]
    tick()
    for _ in range(4):
        z = each(lambda i_, q_: _dot(jnp.concatenate([i_, q_], axis=0).astype(BF16), diag(q_).astype(BF16)), inv, q)
        inv = each(lambda i_, z_: i_ + z_[:h], inv, z)
        q = [x[h:] for x in z]
        tick()
    inv = each(lambda i_, q_: diag(i_ + _dot(i_.astype(BF16), diag(q_).astype(BF16))), inv, q)
    tick()

    akv_rkv = each(_dot, a_kr, vsb)
    tick()
    tt = each(lambda i_, a_, x_: _dot(i_.astype(BF16), jnp.concatenate([a_, x_[:n]], axis=1).astype(BF16)),
              inv, ats, akv_rkv)
    tick()
    rr = each(lambda a_, t_: _dot(a_, t_.astype(BF16)), a_rb, tt)
    rq = each(lambda r_, x_: (r_ + x_[:, :LANES]).astype(BF16), rts, rr)
    oc = each(lambda x_, y_: x_[:, LANES:] + y_[n:], rr, akv_rkv)
    bb = each(lambda t_, b_: _dot(t_.T.astype(BF16), b_), tt, bes)
    bta = [x[:LANES].astype(BF16) for x in bb]
    nct = each(lambda x_, v_, k_: x_[LANES:] + _dot(v_.T.astype(BF16), k_), bb, vs, kes)
    e_gl = each(jnp.exp, gl)
    return rq, oc, bta, nct, e_gl


def _rwkv_kernel(r_ref, k_ref, v_ref, low_ref, mur_ref, muk_ref, muv_ref, mul_ref,
                 ww_ref, wa_ref, wg_ref, w0_ref, a0_ref, kk_ref, ka_ref, rk_ref, lnw_ref, lnb_ref,
                 y_ref, cr_ref, ck_ref, cv_ref, cl_ref, st_ref, feat_ref,
                 rq_ref, oc_ref, bta_ref, nct_ref, egl_ref, bonus_ref, gate_ref, *, tb, nt):
    s = pl.program_id(0)
    nc = tb // RWKV_CHUNK

    @pl.when(s == 0)
    def _():
        for ref in (cr_ref, ck_ref, cv_ref, cl_ref, st_ref, feat_ref, rq_ref, oc_ref, bta_ref, nct_ref, egl_ref,
                    bonus_ref, gate_ref):
            ref[...] = jnp.zeros_like(ref)

    lane = lax.broadcasted_iota(jnp.int32, (1, LANES), 1)
    m0 = (lane < RWKV_HEAD_DIM).astype(F32)
    m1 = 1.0 - m0
    ri = lax.broadcasted_iota(jnp.int32, (LANES, LANES), 0)
    ci = lax.broadcasted_iota(jnp.int32, (LANES, LANES), 1)
    same = (ri < RWKV_HEAD_DIM) == (ci < RWKV_HEAD_DIM)
    ones_bd = same.astype(F32).astype(BF16)
    ti = lax.broadcasted_iota(jnp.int32, (RWKV_CHUNK, RWKV_CHUNK), 0)
    si = lax.broadcasted_iota(jnp.int32, (RWKV_CHUNK, RWKV_CHUNK), 1)
    tw = lax.broadcasted_iota(jnp.int32, (RWKV_CHUNK, LANES), 0)
    sw = lax.broadcasted_iota(jnp.int32, (RWKV_CHUNK, LANES), 1) & (RWKV_HEAD_DIM - 1)
    consts = dict(
        m0=m0, m1=m1, own=same.astype(F32),
        strict=same & (ri > ci), incl=same & (ri >= ci),
        eye=(tw == sw).astype(F32),
        tri=(ti >= si).astype(F32).astype(BF16),
    )

    rd = lax.rem(s, 2)
    rd3 = lax.rem(s + 1, 3)
    seq_start = lax.rem(jnp.maximum(s - 2, 0), nt) == 0
    state = [st_ref[...] * jnp.where(seq_start, 0.0, 1.0)]
    outs = []
    inv_n = 1.0 / RWKV_HEAD_DIM
    own = same.astype(F32)

    def recurrence_step():
        i = len(outs)
        if i == nc:
            return
        st = state[0]
        stb = st.astype(BF16)
        o_stack = _dot_nt(rq_ref[rd, i], stb) + oc_ref[rd, i]
        state[0] = st * egl_ref[rd, i][0:1, :] + _dot(stb, bta_ref[rd, i]) + nct_ref[rd, i]
        mean = jnp.sum(o_stack, axis=-1, keepdims=True) * inv_n
        cen = (o_stack - mean) * own
        var = jnp.sum(cen * cen, axis=-1, keepdims=True) * inv_n
        on = cen * lax.rsqrt(var + RWKV_GN_EPS)
        outs.append(on[:RWKV_CHUNK] + on[RWKV_CHUNK:])

    wf = lax.rem(s, 2)
    wf3 = lax.rem(s, 3)
    keep = jnp.where(lax.rem(jnp.minimum(s, pl.num_programs(0) - 3), nt) == 0, 0.0, 1.0)
    f = {}

    def lerp(x_ref, c_ref, mu_ref):
        x = x_ref[0].astype(F32)
        prev = _prev_rows(x, c_ref[...] * keep)
        c_ref[...] = x[tb - SUBLANES:tb, :]
        return x + (prev - x) * mu_ref[...]

    def feat_low():
        low = lerp(low_ref, cl_ref, mul_ref)
        f["wa_in"] = low[:, :LANES]
        gate_ref[wf3] = _dot(jax.nn.sigmoid(low[:, LANES:]).astype(BF16), wg_ref[...])

    def feat_decay():
        xw = w0_ref[...] + _dot(jnp.tanh(f["wa_in"]).astype(BF16), ww_ref[...])
        feat_ref[wf, 0] = -jnp.exp(jnp.float32(-0.5)) * jax.nn.sigmoid(xw)
        f["a"] = jax.nn.sigmoid(a0_ref[...] + _dot(f["wa_in"].astype(BF16), wa_ref[...]))

    def feat_k():
        k = lerp(k_ref, ck_ref, muk_ref)
        kk = k * kk_ref[...]
        kk = kk * lax.rsqrt(jnp.maximum(_seg_sum(kk * kk, ones_bd), 1e-24))
        f["k2"] = k * (1.0 + (f["a"] - 1.0) * ka_ref[...])
        feat_ref[wf, 2] = f["k2"]
        feat_ref[wf, 4] = -kk
        feat_ref[wf, 5] = kk * f["a"]

    def feat_rv():
        r = lerp(r_ref, cr_ref, mur_ref)
        v = lerp(v_ref, cv_ref, muv_ref)
        feat_ref[wf, 1] = r
        feat_ref[wf, 3] = v
        bonus_ref[wf3] = _seg_sum(r * f["k2"] * rk_ref[...], ones_bd) * v

    pieces = [feat_low, feat_decay, feat_k, feat_rv]

    def tick():
        if pieces:
            pieces.pop(0)()
        recurrence_step()

    rf = lax.rem(s + 1, 2)
    wr = lax.rem(s + 1, 2)
    chunks = lambda j: [feat_ref[rf, j, i * RWKV_CHUNK:(i + 1) * RWKV_CHUNK, :] for i in range(nc)]
    rq, oc, bta, nct, e_gl = _rwkv_prepare(chunks(0), chunks(1), chunks(2), chunks(3), chunks(4), chunks(5),
                                           consts, tick)
    while pieces or len(outs) < nc:
        tick()

    st_ref[...] = state[0]
    o = jnp.concatenate(outs, axis=0) * lnw_ref[...] + lnb_ref[...]
    y_ref[0] = ((o + bonus_ref[rd3]) * gate_ref[rd3]).astype(y_ref.dtype)

    for i in range(nc):
        rq_ref[wr, i] = rq[i]
        oc_ref[wr, i] = oc[i]
        bta_ref[wr, i] = bta[i]
        nct_ref[wr, i] = nct[i]
        egl_ref[wr, i] = jnp.broadcast_to(e_gl[i], (SUBLANES, LANES))


def _rwkv(proj3, mu, ww, wa, wg, w0, a0, k_k, k_a, r_k, ln_w, ln_b, *, tb=512):
    b, t, _ = proj3.shape
    npairs = RWKV_WIDTH // LANES
    nt = t // tb
    nc = tb // RWKV_CHUNK
    nblocks = b * npairs * nt
    off_k = RWKV_WIDTH // LANES
    off_v = 2 * RWKV_WIDTH // LANES
    off_low = 3 * RWKV_WIDTH // RWKV_LOW

    def where(item):
        return item // (nt * npairs), item % nt, (item // nt) % npairs

    cur = lambda s: where(jnp.minimum(s, nblocks - 1))
    lag = lambda s: where(jnp.maximum(s - 2, 0))
    act = lambda off: pl.BlockSpec((1, tb, LANES), lambda s: (cur(s)[0], cur(s)[1], off + cur(s)[2]))
    vec = lambda off: pl.BlockSpec((1, LANES), lambda s: (0, off + cur(s)[2]))
    lagvec = pl.BlockSpec((1, LANES), lambda s: (0, lag(s)[2]))
    wspec = pl.BlockSpec((LANES, LANES), lambda s: (0, cur(s)[2]))
    return pl.pallas_call(
        functools.partial(_rwkv_kernel, tb=tb, nt=nt),
        grid=(nblocks + 2,),
        in_specs=[
            act(0), act(off_k), act(off_v),
            pl.BlockSpec((1, tb, RWKV_LOW), lambda s: (cur(s)[0], cur(s)[1], off_low)),
            vec(0), vec(off_k), vec(off_v),
            pl.BlockSpec((1, RWKV_LOW), lambda s: (0, off_low)),
            wspec, wspec, wspec,
            vec(0), vec(0), vec(0), vec(0), vec(0), lagvec, lagvec,
        ],
        out_specs=pl.BlockSpec((1, tb, LANES), lambda s: lag(s)),
        out_shape=jax.ShapeDtypeStruct((b, t, RWKV_WIDTH), BF16),
        scratch_shapes=[pltpu.VMEM((SUBLANES, LANES), F32), pltpu.VMEM((SUBLANES, LANES), F32),
                        pltpu.VMEM((SUBLANES, LANES), F32), pltpu.VMEM((SUBLANES, RWKV_LOW), F32),
                        pltpu.VMEM((LANES, LANES), F32),
                        pltpu.VMEM((2, 6, tb, LANES), F32),
                        pltpu.VMEM((2, nc, LANES, LANES), BF16), pltpu.VMEM((2, nc, LANES, LANES), F32),
                        pltpu.VMEM((2, nc, LANES, LANES), BF16), pltpu.VMEM((2, nc, LANES, LANES), F32),
                        pltpu.VMEM((2, nc, SUBLANES, LANES), F32),
                        pltpu.VMEM((3, tb, LANES), F32), pltpu.VMEM((3, tb, LANES), F32)],
        compiler_params=pltpu.CompilerParams(
            dimension_semantics=("arbitrary",), vmem_limit_bytes=VMEM_LIMIT),
        name="rwkv7",
    )(proj3, proj3, proj3, proj3, mu, mu, mu, mu, ww, wa, wg, w0, a0, k_k, k_a, r_k, ln_w, ln_b)


def _conv_kernel(c_ref, x_ref, b_ref, w_ref, y_ref, carry_ref):
    @pl.when(pl.program_id(2) == 0)
    def _():
        carry_ref[...] = jnp.zeros_like(carry_ref)

    z = c_ref[0].astype(F32) * x_ref[0].astype(F32)
    tb = z.shape[0]
    w = w_ref[...]
    y = z * w[CONV_K - 1:CONV_K, :]
    for j in range(1, CONV_K):
        y = y + _prev_rows(z, carry_ref[...], j) * w[CONV_K - 1 - j:CONV_K - j, :]
    carry_ref[...] = z[tb - SUBLANES:tb, :]
    y_ref[0] = (b_ref[0].astype(F32) * y).astype(y_ref.dtype)


def _conv(proj3, conv_w, *, tb=1024, cw=2 * LANES):
    b, t, _ = proj3.shape
    tb = min(tb, t)
    nblk = CONV_WIDTH // cw
    off = RWKV_SHIFT_WIDTH // cw
    spec = lambda o: pl.BlockSpec((1, tb, cw), lambda bi, p, ti: (bi, ti, o + p))
    return pl.pallas_call(
        _conv_kernel,
        grid=(b, nblk, t // tb),
        in_specs=[spec(off), spec(off + nblk), spec(off + 2 * nblk),
                  pl.BlockSpec((CONV_K, cw), lambda bi, p, ti: (0, p))],
        out_specs=pl.BlockSpec((1, tb, cw), lambda bi, p, ti: (bi, ti, p)),
        out_shape=jax.ShapeDtypeStruct((b, t, CONV_WIDTH), BF16),
        scratch_shapes=[pltpu.VMEM((SUBLANES, cw), F32)],
        compiler_params=pltpu.CompilerParams(
            dimension_semantics=("parallel", "parallel", "arbitrary"), vmem_limit_bytes=VMEM_LIMIT),
        name="shortconv",
    )(proj3, proj3, proj3, conv_w)


def _hgrn_kernel(q_ref, f_ref, i_ref, og_ref, lb_ref, nw_ref, y_ref, st_ref, *, layer, tb):
    @pl.when(pl.program_id(2) == 0)
    def _():
        st_ref[...] = jnp.zeros_like(st_ref)

    lbp = lb_ref[...]
    e = jnp.exp(lbp - jnp.max(lbp, axis=0, keepdims=True))
    pr = e / jnp.sum(e, axis=0, keepdims=True)
    lb = jnp.zeros((1, LANES), F32)
    for l in range(1, layer + 1):
        lb = lb + pr[l:l + 1, :]

    qr = q_ref[0].astype(F32)
    q = qr * jax.nn.sigmoid(qr)
    f = f_ref[0].astype(F32)
    sg = jax.nn.sigmoid(f)
    forget = lb + (1.0 - lb) * sg
    logf = jnp.log(jnp.maximum(forget, HGRN_MIN_FORGET))
    kx = (1.0 - lb) * (1.0 - sg)
    iv = i_ref[0].astype(F32)

    n = HGRN_BLOCK
    nb = tb // n
    ti = lax.broadcasted_iota(jnp.int32, (tb, tb), 0)
    si = lax.broadcasted_iota(jnp.int32, (tb, tb), 1)
    same = (ti ^ si) < n
    sums = jnp.concatenate([(same & (ti >= si)).astype(F32).astype(BF16), same.astype(F32).astype(BF16)], axis=0)
    hi, lo = _split_bf16(logf)
    gg = _dot(sums, jnp.concatenate([hi, lo], axis=1))
    g = gg[:tb, :LANES] + gg[:tb, LANES:]
    gl = gg[tb:, :LANES] + gg[tb:, LANES:]
    qd = (q * jnp.exp(g)).astype(BF16)
    kd = (kx * jnp.exp(gl - g)).astype(BF16)
    dec = jnp.exp(gl)
    trow = lax.broadcasted_iota(jnp.int32, (n, 1), 0)
    blocks = lambda x: [x[i * n:(i + 1) * n] for i in range(nb)]
    g2 = g * 1.4426950408889634
    gb, hb, qb, ib, qdb, kdb = (blocks(x) for x in (g2, g2 - jnp.log2(kx), q, iv, qd, kd))

    upd = [_dot(i_.T.astype(BF16), k_) for i_, k_ in zip(ib, kdb)]
    st = st_ref[...]
    sts = []
    for i in range(nb):
        sts.append(st.astype(BF16))
        st = st * dec[i * n:i * n + 1, :] + upd[i]
    st_ref[...] = st
    outs = [_dot_nt(q_, s_) for q_, s_ in zip(qdb, sts)]
    lane = lax.broadcasted_iota(jnp.int32, (n, LANES), 1)
    scores = [jnp.zeros((n, LANES), F32) for _ in range(nb)]
    for s in range(n):
        for i in range(nb):
            w = jnp.sum(qb[i] * jnp.exp2(gb[i] - hb[i][s:s + 1, :]), axis=-1, keepdims=True)
            scores[i] = jnp.where(lane == s, w, scores[i])
    causal = lane <= trow
    for i in range(nb):
        sc = jnp.where(causal, scores[i], 0.0)[:, :n].astype(BF16)
        outs[i] = outs[i] + _dot(sc, ib[i].astype(BF16))
    o = jnp.concatenate(outs, axis=0)
    o = o * lax.rsqrt(jnp.mean(o * o, axis=-1, keepdims=True) + HGRN_NORM_EPS) * nw_ref[...]
    og = og_ref[0].astype(F32)
    y_ref[0] = (o * (og * jax.nn.sigmoid(og))).astype(y_ref.dtype)


def _hgrn(proj3, hgrn_lb, norm_w, *, layer, tb=512):
    b, t, _ = proj3.shape
    tb = min(tb, t)
    nh = HGRN_WIDTH // HGRN_HEAD_DIM
    off = (RWKV_SHIFT_WIDTH + 3 * CONV_WIDTH) // LANES
    spec = lambda o: pl.BlockSpec((1, tb, LANES), lambda bi, p, ti: (bi, ti, o + p))
    depth = hgrn_lb.shape[0]
    return pl.pallas_call(
        functools.partial(_hgrn_kernel, layer=layer, tb=tb),
        grid=(b, nh, t // tb),
        in_specs=[spec(off), spec(off + nh), spec(off + 2 * nh), spec(off + 3 * nh),
                  pl.BlockSpec((depth, LANES), lambda bi, p, ti: (0, p)),
                  pl.BlockSpec((1, LANES), lambda bi, p, ti: (0, p))],
        out_specs=pl.BlockSpec((1, tb, LANES), lambda bi, p, ti: (bi, ti, p)),
        out_shape=jax.ShapeDtypeStruct((b, t, HGRN_WIDTH), BF16),
        scratch_shapes=[pltpu.VMEM((LANES, LANES), F32)],
        compiler_params=pltpu.CompilerParams(
            dimension_semantics=("parallel", "parallel", "arbitrary"), vmem_limit_bytes=VMEM_LIMIT),
        name="hgrn2",
    )(proj3, proj3, proj3, proj3, hgrn_lb, norm_w)


def _cast_kernel(x_ref, o_ref, *, valid_rows, rows):
    x = x_ref[...]
    if valid_rows % rows:
        r = pl.program_id(1) * rows + lax.broadcasted_iota(jnp.int32, x.shape, 1)
        x = jnp.where(r < valid_rows, x, 0.0)
    o_ref[...] = x.astype(o_ref.dtype)


def _cast_rows(w, out_rows, *, rows=256):
    nl, r, c = w.shape
    return pl.pallas_call(
        functools.partial(_cast_kernel, valid_rows=r, rows=rows),
        grid=(nl, out_rows // rows),
        in_specs=[pl.BlockSpec((1, rows, c), lambda l, i: (l, i, 0))],
        out_specs=pl.BlockSpec((1, rows, c), lambda l, i: (l, i, 0)),
        out_shape=jax.ShapeDtypeStruct((nl, out_rows, c), BF16),
        compiler_params=pltpu.CompilerParams(
            dimension_semantics=("parallel", "parallel"), vmem_limit_bytes=VMEM_LIMIT),
        name="cast_rows",
    )(w)


def _split_gu_kernel(x_ref, g_ref, u_ref, *, d_ff):
    x = x_ref[0]
    pad = jnp.zeros((x.shape[0], g_ref.shape[2] - d_ff), g_ref.dtype)
    g_ref[0] = jnp.concatenate([x[:, :d_ff].astype(g_ref.dtype), pad], axis=1)
    u_ref[0] = jnp.concatenate([x[:, d_ff:].astype(u_ref.dtype), pad], axis=1)


def _split_gu(w_gu, d_ff, ffp, *, rows=256):
    nl, d, _ = w_gu.shape
    out = jax.ShapeDtypeStruct((nl, d, ffp), BF16)
    spec = pl.BlockSpec((1, rows, ffp), lambda l, i: (l, i, 0))
    return pl.pallas_call(
        functools.partial(_split_gu_kernel, d_ff=d_ff),
        grid=(nl, d // rows),
        in_specs=[pl.BlockSpec((1, rows, 2 * d_ff), lambda l, i: (l, i, 0))],
        out_specs=(spec, spec),
        out_shape=(out, out),
        compiler_params=pltpu.CompilerParams(
            dimension_semantics=("parallel", "parallel"), vmem_limit_bytes=VMEM_LIMIT),
        name="split_gate_up",
    )(w_gu)


def kernel(x, ffn1_norm, ffn1_w_gu, ffn1_w_down, mix_norm, w_in, rwkv_mu, rwkv_w_up, rwkv_w0,
           rwkv_a_up, rwkv_a0, rwkv_g_up, rwkv_k_k, rwkv_k_a, rwkv_r_k, rwkv_ln_w, rwkv_ln_b,
           conv_w, hgrn_lb, hgrn_norm, w_out, ffn2_norm, ffn2_w_gu, ffn2_w_down, final_norm):
    b, t, d = x.shape
    depth = ffn1_norm.shape[0]
    d_ff = ffn1_w_down.shape[1]
    ffp = -(-d_ff // 512) * 512
    m = b * t
    h = x.reshape(m, d)
    row = lambda v: v.reshape(1, -1)
    final_w = row(final_norm)

    wg1, wu1 = _split_gu(ffn1_w_gu, d_ff, ffp)
    wg2, wu2 = _split_gu(ffn2_w_gu, d_ff, ffp)
    wd_last = _cast_rows(ffn2_w_down[depth - 1:], ffp)
    w_in_b = _cast_rows(w_in, d)
    w_out_b = _cast_rows(w_out, d)
    zeros_low = jnp.zeros((depth, LANES - DECAY_RANK, RWKV_WIDTH), BF16)
    ww = jnp.concatenate([rwkv_w_up.astype(BF16), zeros_low], axis=1)
    wa = jnp.concatenate([zeros_low, rwkv_a_up.astype(BF16)], axis=1)
    wgate = rwkv_g_up.astype(BF16)

    for l in range(depth):
        h = _ffn(h, row(ffn1_norm[l]), wg1, wu1, ffn1_w_down, final_w, layer=l, final=False)

        proj = _proj(h, row(mix_norm[l]), w_in_b, layer=l)
        proj3 = proj.reshape(b, t, -1)
        y_r = _rwkv(proj3, row(rwkv_mu[l]), ww[l], wa[l], wgate[l],
                    row(rwkv_w0[l]), row(rwkv_a0[l]), row(rwkv_k_k[l]), row(rwkv_k_a[l]),
                    row(rwkv_r_k[l]), row(rwkv_ln_w[l]), row(rwkv_ln_b[l]))
        y_c = _conv(proj3, conv_w[l])
        y_h = _hgrn(proj3, hgrn_lb, row(hgrn_norm[l]), layer=l)
        h = _outproj(h, y_r.reshape(m, -1), y_c.reshape(m, -1), y_h.reshape(m, -1), w_out_b, layer=l)

        if l < depth - 1:
            h = _ffn(h, row(ffn2_norm[l]), wg2, wu2, ffn2_w_down, final_w, layer=l, final=False)
        else:
            h = _ffn(h, row(ffn2_norm[l]), wg2, wu2, wd_last, final_w, layer=l, wd_layer=0, final=True)
    return h.reshape(b, t, d)
```

```python
import functools

import jax
import jax.numpy as jnp
from jax import lax
from jax.experimental import pallas as pl
from jax.experimental.pallas import tpu as pltpu

F32 = jnp.float32
BF16 = jnp.bfloat16

RWKV_HEAD_DIM = 64
RWKV_WIDTH = 1024
DECAY_RANK = 64
AAA_RANK = 64
GATE_RANK = 128
RWKV_LOW = DECAY_RANK + AAA_RANK + GATE_RANK
RWKV_SHIFT_WIDTH = 3 * RWKV_WIDTH + RWKV_LOW
RWKV_GN_EPS = 64e-5
CONV_WIDTH = 512
CONV_K = 3
HGRN_HEAD_DIM = 128
HGRN_WIDTH = 512
HGRN_NORM_EPS = 1e-5
HGRN_MIN_FORGET = 1e-30
FFN_RESIDUAL_SCALE = 0.5
RMS_EPS = 1e-6

LANES = 128
SUBLANES = 8
RWKV_CHUNK = 64
HGRN_BLOCK = 16
VMEM_LIMIT = 60000 * 1024


def _dot(a, b):
    return jnp.dot(a, b, preferred_element_type=F32)


def _dot_nt(a, b):
    return lax.dot_general(a, b, (((1,), (1,)), ((), ())), preferred_element_type=F32)


def _rms(x, w):
    return x * lax.rsqrt(jnp.mean(x * x, axis=-1, keepdims=True) + RMS_EPS) * w


def _split_bf16(x):
    hi = x.astype(BF16)
    lo = (x - hi.astype(F32)).astype(BF16)
    return hi, lo


def _ffn_kernel(x_ref, nw_ref, wg_ref, wu_ref, wd_ref, fw_ref, o_ref, xn_ref, *, final, tail_rows):
    j = pl.program_id(1)

    def partial_sum(tail=False):
        xn = xn_ref[...]
        g = _dot(xn, wg_ref[...])
        u = _dot(xn, wu_ref[...])
        a = (g * jax.nn.sigmoid(g) * u).astype(BF16)
        wd = wd_ref[...]
        if tail and tail_rows != wd.shape[0]:
            wd = jnp.where(lax.broadcasted_iota(jnp.int32, wd.shape, 0) < tail_rows, wd, 0.0)
        return _dot(a, wd.astype(BF16))

    @pl.when(j == 0)
    def _():
        xn_ref[...] = _rms(x_ref[...], nw_ref[...]).astype(BF16)
        o_ref[...] = partial_sum()

    last = pl.num_programs(1) - 1

    @pl.when((j > 0) & (j < last))
    def _():
        o_ref[...] += partial_sum()

    @pl.when(j == last)
    def _():
        h = x_ref[...] + FFN_RESIDUAL_SCALE * (o_ref[...] + partial_sum(tail=True))
        if final:
            h = _rms(h, fw_ref[...])
        o_ref[...] = h


def _ffn(h, norm_w, wg, wu, wd, final_w, *, layer, final, wd_layer=None, tm=1024, tf=512):
    wd_layer = layer if wd_layer is None else wd_layer
    m, d = h.shape
    ffp = wg.shape[2]
    grid = (m // tm, ffp // tf)
    assert grid[1] >= 2
    return pl.pallas_call(
        functools.partial(_ffn_kernel, final=final, tail_rows=wd.shape[1] - (grid[1] - 1) * tf),
        grid=grid,
        in_specs=[
            pl.BlockSpec((tm, d), lambda i, j: (i, 0)),
            pl.BlockSpec((1, d), lambda i, j: (0, 0)),
            pl.BlockSpec((None, d, tf), lambda i, j: (layer, 0, j)),
            pl.BlockSpec((None, d, tf), lambda i, j: (layer, 0, j)),
            pl.BlockSpec((None, tf, d), lambda i, j: (wd_layer, j, 0)),
            pl.BlockSpec((1, d), lambda i, j: (0, 0)),
        ],
        out_specs=pl.BlockSpec((tm, d), lambda i, j: (i, 0)),
        out_shape=jax.ShapeDtypeStruct((m, d), F32),
        scratch_shapes=[pltpu.VMEM((tm, d), BF16)],
        compiler_params=pltpu.CompilerParams(
            dimension_semantics=("parallel", "arbitrary"), vmem_limit_bytes=VMEM_LIMIT),
        name="ffn",
    )(h, norm_w, wg, wu, wd, final_w)


def _proj_kernel(x_ref, nw_ref, w_ref, o_ref, xn_ref):
    @pl.when(pl.program_id(1) == 0)
    def _():
        xn_ref[...] = _rms(x_ref[...], nw_ref[...]).astype(BF16)
        o_ref[...] = _dot(xn_ref[...], w_ref[...]).astype(o_ref.dtype)

    @pl.when(pl.program_id(1) > 0)
    def _():
        o_ref[...] = _dot(xn_ref[...], w_ref[...]).astype(o_ref.dtype)


def _proj(h, norm_w, w, *, layer, tm=512):
    m, d = h.shape
    n = w.shape[2]
    tn = n
    return pl.pallas_call(
        _proj_kernel,
        grid=(m // tm, n // tn),
        in_specs=[
            pl.BlockSpec((tm, d), lambda i, j: (i, 0)),
            pl.BlockSpec((1, d), lambda i, j: (0, 0)),
            pl.BlockSpec((None, d, tn), lambda i, j: (layer, 0, j), pipeline_mode=pl.Buffered(1)),
        ],
        out_specs=pl.BlockSpec((tm, tn), lambda i, j: (i, j)),
        out_shape=jax.ShapeDtypeStruct((m, n), BF16),
        scratch_shapes=[pltpu.VMEM((tm, d), BF16)],
        compiler_params=pltpu.CompilerParams(
            dimension_semantics=("parallel", "arbitrary"), vmem_limit_bytes=VMEM_LIMIT),
        name="proj",
    )(h, norm_w, w)


def _outproj_kernel(h_ref, yr_ref, yc_ref, yh_ref, wr_ref, wc_ref, wh_ref, o_ref):
    acc = _dot(yr_ref[...], wr_ref[...])
    acc += _dot(yc_ref[...], wc_ref[...])
    acc += _dot(yh_ref[...], wh_ref[...])
    o_ref[...] = h_ref[...] + acc


def _outproj(h, yr, yc, yh, wo, *, layer, tm=512):
    m, d = h.shape
    nr, ncv, nh = yr.shape[1], yc.shape[1], yh.shape[1]
    row = lambda i: (i, 0)
    return pl.pallas_call(
        _outproj_kernel,
        grid=(m // tm,),
        in_specs=[
            pl.BlockSpec((tm, d), row),
            pl.BlockSpec((tm, yr.shape[1]), row),
            pl.BlockSpec((tm, yc.shape[1]), row),
            pl.BlockSpec((tm, yh.shape[1]), row),
            pl.BlockSpec((None, nr, d), lambda i: (layer, 0, 0)),
            pl.BlockSpec((None, ncv, d), lambda i: (layer, nr // ncv, 0)),
            pl.BlockSpec((None, nh, d), lambda i: (layer, (nr + ncv) // nh, 0)),
        ],
        out_specs=pl.BlockSpec((tm, d), row),
        out_shape=jax.ShapeDtypeStruct((m, d), F32),
        compiler_params=pltpu.CompilerParams(
            dimension_semantics=("parallel",), vmem_limit_bytes=VMEM_LIMIT),
        name="outproj",
    )(h, yr, yc, yh, wo, wo, wo)


def _prev_rows(x, carry, k=1):
    rows = lax.broadcasted_iota(jnp.int32, (SUBLANES, x.shape[1]), 0)
    prev = pltpu.roll(x, k, 0)
    head = jnp.where(rows < k, pltpu.roll(carry, k, 0), prev[:SUBLANES])
    return jnp.concatenate([head, prev[SUBLANES:]], axis=0)


def _seg_sum(x, ones_bd):
    return _dot(x.astype(BF16), ones_bd)


def _rwkv_prepare(lw, r, k2, v, av, bv, c, tick):
    n = 2 * RWKV_CHUNK
    m0, m1 = c["m0"], c["m1"]
    stack = lambda x: jnp.concatenate([x * m0, x * m1], axis=0)
    each = lambda f, *ls: [f(*xs) for xs in zip(*ls)]

    def cumsum(x):
        hi, lo = _split_bf16(x)
        gcat = _dot(c["tri"], jnp.concatenate([hi, lo], axis=1))
        return gcat[:, :LANES] + gcat[:, LANES:]

    g = each(cumsum, lw)
    tick()
    gl = [x[RWKV_CHUNK - 1:RWKV_CHUNK, :] for x in g]
    e_g = each(jnp.exp, g)
    e_ng = each(lambda x: jnp.exp(-x), g)
    e_d = each(lambda x, y: jnp.exp(x - y), gl, g)
    ats = each(lambda a_, g_, l_: stack(a_ * jnp.exp(g_ - l_)), av, g, lw)
    rts = each(lambda r_, e_: stack(r_ * e_), r, e_g)
    bts = each(lambda b_, e_: stack(b_ * e_), bv, e_ng)
    kts = each(lambda k_, e_: stack(k_ * e_), k2, e_ng)
    bes = each(lambda b_, e_: stack(b_ * e_).astype(BF16), bv, e_d)
    kes = each(lambda k_, e_: stack(k_ * e_).astype(BF16), k2, e_d)
    vs = each(stack, v)
    vsb = [x.astype(BF16) for x in vs]

    p = each(lambda a_, r_, b_, k_: _dot_nt(jnp.concatenate([a_, r_], axis=0).astype(BF16),
                                            jnp.concatenate([b_, k_], axis=0).astype(BF16)),
             ats, rts, bts, kts)
    a_ab = [jnp.where(c["strict"], x[:n, :n], 0.0) for x in p]
    a_kr = [jnp.concatenate([jnp.where(c["strict"], x[:n, n:], 0.0),
                             jnp.where(c["incl"], x[n:, n:], 0.0)], axis=0).astype(BF16) for x in p]
    a_rb = [jnp.where(c["incl"], x[n:, :n], 0.0).astype(BF16) for x in p]
    tick()

    h = RWKV_CHUNK
    side = lambda x: x[:h] + x[h:]
    diag = lambda x: jnp.concatenate([x, x], axis=0) * c["own"]
    a_sbs = [side(x) for x in a_ab]
    inv = [c["eye"] + x for x in a_sbs]
    q = [_dot(x.astype(BF16), y.astype(BF16)) for x, y in zip(a_sbs, a_ab)]
    tick()
    for _ in range(4):
        z = each(lambda i_, q_: _dot(jnp.concatenate([i_, q_], axis=0).astype(BF16), diag(q_).astype(BF16)), inv, q)
        inv = each(lambda i_, z_: i_ + z_[:h], inv, z)
        q = [x[h:] for x in z]
        tick()
    inv = each(lambda i_, q_: diag(i_ + _dot(i_.astype(BF16), diag(q_).astype(BF16))), inv, q)
    tick()

    akv_rkv = each(_dot, a_kr, vsb)
    tick()
    tt = each(lambda i_, a_, x_: _dot(i_.astype(BF16), jnp.concatenate([a_, x_[:n]], axis=1).astype(BF16)),
              inv, ats, akv_rkv)
    tick()
    rr = each(lambda a_, t_: _dot(a_, t_.astype(BF16)), a_rb, tt)
    rq = each(lambda r_, x_: (r_ + x_[:, :LANES]).astype(BF16), rts, rr)
    oc = each(lambda x_, y_: x_[:, LANES:] + y_[n:], rr, akv_rkv)
    bb = each(lambda t_, b_: _dot(t_.T.astype(BF16), b_), tt, bes)
    bta = [x[:LANES].astype(BF16) for x in bb]
    nct = each(lambda x_, v_, k_: x_[LANES:] + _dot(v_.T.astype(BF16), k_), bb, vs, kes)
    e_gl = each(jnp.exp, gl)
    return rq, oc, bta, nct, e_gl


def _rwkv_kernel(r_ref, k_ref, v_ref, low_ref, mur_ref, muk_ref, muv_ref, mul_ref,
                 ww_ref, wa_ref, wg_ref, w0_ref, a0_ref, kk_ref, ka_ref, rk_ref, lnw_ref, lnb_ref,
                 y_ref, cr_ref, ck_ref, cv_ref, cl_ref, st_ref, feat_ref,
                 rq_ref, oc_ref, bta_ref, nct_ref, egl_ref, bonus_ref, gate_ref, *, tb, nt):
    s = pl.program_id(0)
    nc = tb // RWKV_CHUNK

    @pl.when(s == 0)
    def _():
        for ref in (cr_ref, ck_ref, cv_ref, cl_ref, st_ref, feat_ref, rq_ref, oc_ref, bta_ref, nct_ref, egl_ref,
                    bonus_ref, gate_ref):
            ref[...] = jnp.zeros_like(ref)

    lane = lax.broadcasted_iota(jnp.int32, (1, LANES), 1)
    m0 = (lane < RWKV_HEAD_DIM).astype(F32)
    m1 = 1.0 - m0
    ri = lax.broadcasted_iota(jnp.int32, (LANES, LANES), 0)
    ci = lax.broadcasted_iota(jnp.int32, (LANES, LANES), 1)
    same = (ri < RWKV_HEAD_DIM) == (ci < RWKV_HEAD_DIM)
    ones_bd = same.astype(F32).astype(BF16)
    ti = lax.broadcasted_iota(jnp.int32, (RWKV_CHUNK, RWKV_CHUNK), 0)
    si = lax.broadcasted_iota(jnp.int32, (RWKV_CHUNK, RWKV_CHUNK), 1)
    tw = lax.broadcasted_iota(jnp.int32, (RWKV_CHUNK, LANES), 0)
    sw = lax.broadcasted_iota(jnp.int32, (RWKV_CHUNK, LANES), 1) & (RWKV_HEAD_DIM - 1)
    consts = dict(
        m0=m0, m1=m1, own=same.astype(F32),
        strict=same & (ri > ci), incl=same & (ri >= ci),
        eye=(tw == sw).astype(F32),
        tri=(ti >= si).astype(F32).astype(BF16),
    )

    rd = lax.rem(s, 2)
    rd3 = lax.rem(s + 1, 3)
    seq_start = lax.rem(jnp.maximum(s - 2, 0), nt) == 0
    state = [st_ref[...] * jnp.where(seq_start, 0.0, 1.0)]
    outs = []
    inv_n = 1.0 / RWKV_HEAD_DIM
    own = same.astype(F32)

    def recurrence_step():
        i = len(outs)
        if i == nc:
            return
        st = state[0]
        stb = st.astype(BF16)
        o_stack = _dot_nt(rq_ref[rd, i], stb) + oc_ref[rd, i]
        state[0] = st * egl_ref[rd, i][0:1, :] + _dot(stb, bta_ref[rd, i]) + nct_ref[rd, i]
        mean = jnp.sum(o_stack, axis=-1, keepdims=True) * inv_n
        cen = (o_stack - mean) * own
        var = jnp.sum(cen * cen, axis=-1, keepdims=True) * inv_n
        on = cen * lax.rsqrt(var + RWKV_GN_EPS)
        outs.append(on[:RWKV_CHUNK] + on[RWKV_CHUNK:])

    wf = lax.rem(s, 2)
    wf3 = lax.rem(s, 3)
    keep = jnp.where(lax.rem(jnp.minimum(s, pl.num_programs(0) - 3), nt) == 0, 0.0, 1.0)
    f = {}

    def lerp(x_ref, c_ref, mu_ref):
        x = x_ref[0].astype(F32)
        prev = _prev_rows(x, c_ref[...] * keep)
        c_ref[...] = x[tb - SUBLANES:tb, :]
        return x + (prev - x) * mu_ref[...]

    def feat_low():
        low = lerp(low_ref, cl_ref, mul_ref)
        f["wa_in"] = low[:, :LANES]
        gate_ref[wf3] = _dot(jax.nn.sigmoid(low[:, LANES:]).astype(BF16), wg_ref[...])

    def feat_decay():
        xw = w0_ref[...] + _dot(jnp.tanh(f["wa_in"]).astype(BF16), ww_ref[...])
        feat_ref[wf, 0] = -jnp.exp(jnp.float32(-0.5)) * jax.nn.sigmoid(xw)
        f["a"] = jax.nn.sigmoid(a0_ref[...] + _dot(f["wa_in"].astype(BF16), wa_ref[...]))

    def feat_k():
        k = lerp(k_ref, ck_ref, muk_ref)
        kk = k * kk_ref[...]
        kk = kk * lax.rsqrt(jnp.maximum(_seg_sum(kk * kk, ones_bd), 1e-24))
        f["k2"] = k * (1.0 + (f["a"] - 1.0) * ka_ref[...])
        feat_ref[wf, 2] = f["k2"]
        feat_ref[wf, 4] = -kk
        feat_ref[wf, 5] = kk * f["a"]

    def feat_rv():
        r = lerp(r_ref, cr_ref, mur_ref)
        v = lerp(v_ref, cv_ref, muv_ref)
        feat_ref[wf, 1] = r
        feat_ref[wf, 3] = v
        bonus_ref[wf3] = _seg_sum(r * f["k2"] * rk_ref[...], ones_bd) * v

    pieces = [feat_low, feat_decay, feat_k, feat_rv]

    def tick():
        if pieces:
            pieces.pop(0)()
        recurrence_step()

    rf = lax.rem(s + 1, 2)
    wr = lax.rem(s + 1, 2)
    chunks = lambda j: [feat_ref[rf, j, i * RWKV_CHUNK:(i + 1) * RWKV_CHUNK, :] for i in range(nc)]
    rq, oc, bta, nct, e_gl = _rwkv_prepare(chunks(0), chunks(1), chunks(2), chunks(3), chunks(4), chunks(5),
                                           consts, tick)
    while pieces or len(outs) < nc:
        tick()

    st_ref[...] = state[0]
    o = jnp.concatenate(outs, axis=0) * lnw_ref[...] + lnb_ref[...]
    y_ref[0] = ((o + bonus_ref[rd3]) * gate_ref[rd3]).astype(y_ref.dtype)

    for i in range(nc):
        rq_ref[wr, i] = rq[i]
        oc_ref[wr, i] = oc[i]
        bta_ref[wr, i] = bta[i]
        nct_ref[wr, i] = nct[i]
        egl_ref[wr, i] = jnp.broadcast_to(e_gl[i], (SUBLANES, LANES))


def _rwkv(proj3, mu, ww, wa, wg, w0, a0, k_k, k_a, r_k, ln_w, ln_b, *, tb=512):
    b, t, _ = proj3.shape
    npairs = RWKV_WIDTH // LANES
    nt = t // tb
    nc = tb // RWKV_CHUNK
    nblocks = b * npairs * nt
    off_k = RWKV_WIDTH // LANES
    off_v = 2 * RWKV_WIDTH // LANES
    off_low = 3 * RWKV_WIDTH // RWKV_LOW

    def where(item):
        return item // (nt * npairs), item % nt, (item // nt) % npairs

    cur = lambda s: where(jnp.minimum(s, nblocks - 1))
    lag = lambda s: where(jnp.maximum(s - 2, 0))
    act = lambda off: pl.BlockSpec((1, tb, LANES), lambda s: (cur(s)[0], cur(s)[1], off + cur(s)[2]))
    vec = lambda off: pl.BlockSpec((1, LANES), lambda s: (0, off + cur(s)[2]))
    lagvec = pl.BlockSpec((1, LANES), lambda s: (0, lag(s)[2]))
    wspec = pl.BlockSpec((LANES, LANES), lambda s: (0, cur(s)[2]))
    return pl.pallas_call(
        functools.partial(_rwkv_kernel, tb=tb, nt=nt),
        grid=(nblocks + 2,),
        in_specs=[
            act(0), act(off_k), act(off_v),
            pl.BlockSpec((1, tb, RWKV_LOW), lambda s: (cur(s)[0], cur(s)[1], off_low)),
            vec(0), vec(off_k), vec(off_v),
            pl.BlockSpec((1, RWKV_LOW), lambda s: (0, off_low)),
            wspec, wspec, wspec,
            vec(0), vec(0), vec(0), vec(0), vec(0), lagvec, lagvec,
        ],
        out_specs=pl.BlockSpec((1, tb, LANES), lambda s: lag(s)),
        out_shape=jax.ShapeDtypeStruct((b, t, RWKV_WIDTH), BF16),
        scratch_shapes=[pltpu.VMEM((SUBLANES, LANES), F32), pltpu.VMEM((SUBLANES, LANES), F32),
                        pltpu.VMEM((SUBLANES, LANES), F32), pltpu.VMEM((SUBLANES, RWKV_LOW), F32),
                        pltpu.VMEM((LANES, LANES), F32),
                        pltpu.VMEM((2, 6, tb, LANES), F32),
                        pltpu.VMEM((2, nc, LANES, LANES), BF16), pltpu.VMEM((2, nc, LANES, LANES), F32),
                        pltpu.VMEM((2, nc, LANES, LANES), BF16), pltpu.VMEM((2, nc, LANES, LANES), F32),
                        pltpu.VMEM((2, nc, SUBLANES, LANES), F32),
                        pltpu.VMEM((3, tb, LANES), F32), pltpu.VMEM((3, tb, LANES), F32)],
        compiler_params=pltpu.CompilerParams(
            dimension_semantics=("arbitrary",), vmem_limit_bytes=VMEM_LIMIT),
        name="rwkv7",
    )(proj3, proj3, proj3, proj3, mu, mu, mu, mu, ww, wa, wg, w0, a0, k_k, k_a, r_k, ln_w, ln_b)


def _conv_kernel(c_ref, x_ref, b_ref, w_ref, y_ref, carry_ref):
    @pl.when(pl.program_id(2) == 0)
    def _():
        carry_ref[...] = jnp.zeros_like(carry_ref)

    z = c_ref[0].astype(F32) * x_ref[0].astype(F32)
    tb = z.shape[0]
    w = w_ref[...]
    y = z * w[CONV_K - 1:CONV_K, :]
    for j in range(1, CONV_K):
        y = y + _prev_rows(z, carry_ref[...], j) * w[CONV_K - 1 - j:CONV_K - j, :]
    carry_ref[...] = z[tb - SUBLANES:tb, :]
    y_ref[0] = (b_ref[0].astype(F32) * y).astype(y_ref.dtype)


def _conv(proj3, conv_w, *, tb=1024, cw=2 * LANES):
    b, t, _ = proj3.shape
    tb = min(tb, t)
    nblk = CONV_WIDTH // cw
    off = RWKV_SHIFT_WIDTH // cw
    spec = lambda o: pl.BlockSpec((1, tb, cw), lambda bi, p, ti: (bi, ti, o + p))
    return pl.pallas_call(
        _conv_kernel,
        grid=(b, nblk, t // tb),
        in_specs=[spec(off), spec(off + nblk), spec(off + 2 * nblk),
                  pl.BlockSpec((CONV_K, cw), lambda bi, p, ti: (0, p))],
        out_specs=pl.BlockSpec((1, tb, cw), lambda bi, p, ti: (bi, ti, p)),
        out_shape=jax.ShapeDtypeStruct((b, t, CONV_WIDTH), BF16),
        scratch_shapes=[pltpu.VMEM((SUBLANES, cw), F32)],
        compiler_params=pltpu.CompilerParams(
            dimension_semantics=("parallel", "parallel", "arbitrary"), vmem_limit_bytes=VMEM_LIMIT),
        name="shortconv",
    )(proj3, proj3, proj3, conv_w)


def _hgrn_kernel(q_ref, f_ref, i_ref, og_ref, lb_ref, nw_ref, y_ref, st_ref, *, layer, tb):
    @pl.when(pl.program_id(2) == 0)
    def _():
        st_ref[...] = jnp.zeros_like(st_ref)

    lbp = lb_ref[...]
    e = jnp.exp(lbp - jnp.max(lbp, axis=0, keepdims=True))
    pr = e / jnp.sum(e, axis=0, keepdims=True)
    lb = jnp.zeros((1, LANES), F32)
    for l in range(1, layer + 1):
        lb = lb + pr[l:l + 1, :]

    qr = q_ref[0].astype(F32)
    q = qr * jax.nn.sigmoid(qr)
    f = f_ref[0].astype(F32)
    sg = jax.nn.sigmoid(f)
    forget = lb + (1.0 - lb) * sg
    logf = jnp.log(jnp.maximum(forget, HGRN_MIN_FORGET))
    kx = (1.0 - lb) * (1.0 - sg)
    iv = i_ref[0].astype(F32)

    n = HGRN_BLOCK
    nb = tb // n
    ti = lax.broadcasted_iota(jnp.int32, (tb, tb), 0)
    si = lax.broadcasted_iota(jnp.int32, (tb, tb), 1)
    same = (ti ^ si) < n
    sums = jnp.concatenate([(same & (ti >= si)).astype(F32).astype(BF16), same.astype(F32).astype(BF16)], axis=0)
    hi, lo = _split_bf16(logf)
    gg = _dot(sums, jnp.concatenate([hi, lo], axis=1))
    g = gg[:tb, :LANES] + gg[:tb, LANES:]
    gl = gg[tb:, :LANES] + gg[tb:, LANES:]
    qd = (q * jnp.exp(g)).astype(BF16)
    kd = (kx * jnp.exp(gl - g)).astype(BF16)
    dec = jnp.exp(gl)
    trow = lax.broadcasted_iota(jnp.int32, (n, 1), 0)
    blocks = lambda x: [x[i * n:(i + 1) * n] for i in range(nb)]
    g2 = g * 1.4426950408889634
    gb, hb, qb, ib, qdb, kdb = (blocks(x) for x in (g2, g2 - jnp.log2(kx), q, iv, qd, kd))

    upd = [_dot(i_.T.astype(BF16), k_) for i_, k_ in zip(ib, kdb)]
    st = st_ref[...]
    sts = []
    for i in range(nb):
        sts.append(st.astype(BF16))
        st = st * dec[i * n:i * n + 1, :] + upd[i]
    st_ref[...] = st
    outs = [_dot_nt(q_, s_) for q_, s_ in zip(qdb, sts)]
    lane = lax.broadcasted_iota(jnp.int32, (n, LANES), 1)
    scores = [jnp.zeros((n, LANES), F32) for _ in range(nb)]
    for s in range(n):
        for i in range(nb):
            w = jnp.sum(qb[i] * jnp.exp2(gb[i] - hb[i][s:s + 1, :]), axis=-1, keepdims=True)
            scores[i] = jnp.where(lane == s, w, scores[i])
    causal = lane <= trow
    for i in range(nb):
        sc = jnp.where(causal, scores[i], 0.0)[:, :n].astype(BF16)
        outs[i] = outs[i] + _dot(sc, ib[i].astype(BF16))
    o = jnp.concatenate(outs, axis=0)
    o = o * lax.rsqrt(jnp.mean(o * o, axis=-1, keepdims=True) + HGRN_NORM_EPS) * nw_ref[...]
    og = og_ref[0].astype(F32)
    y_ref[0] = (o * (og * jax.nn.sigmoid(og))).astype(y_ref.dtype)


def _hgrn(proj3, hgrn_lb, norm_w, *, layer, tb=512):
    b, t, _ = proj3.shape
    tb = min(tb, t)
    nh = HGRN_WIDTH // HGRN_HEAD_DIM
    off = (RWKV_SHIFT_WIDTH + 3 * CONV_WIDTH) // LANES
    spec = lambda o: pl.BlockSpec((1, tb, LANES), lambda bi, p, ti: (bi, ti, o + p))
    depth = hgrn_lb.shape[0]
    return pl.pallas_call(
        functools.partial(_hgrn_kernel, layer=layer, tb=tb),
        grid=(b, nh, t // tb),
        in_specs=[spec(off), spec(off + nh), spec(off + 2 * nh), spec(off + 3 * nh),
                  pl.BlockSpec((depth, LANES), lambda bi, p, ti: (0, p)),
                  pl.BlockSpec((1, LANES), lambda bi, p, ti: (0, p))],
        out_specs=pl.BlockSpec((1, tb, LANES), lambda bi, p, ti: (bi, ti, p)),
        out_shape=jax.ShapeDtypeStruct((b, t, HGRN_WIDTH), BF16),
        scratch_shapes=[pltpu.VMEM((LANES, LANES), F32)],
        compiler_params=pltpu.CompilerParams(
            dimension_semantics=("parallel", "parallel", "arbitrary"), vmem_limit_bytes=VMEM_LIMIT),
        name="hgrn2",
    )(proj3, proj3, proj3, proj3, hgrn_lb, norm_w)


def _cast_kernel(x_ref, o_ref, *, valid_rows, rows):
    x = x_ref[...]
    if valid_rows % rows:
        r = pl.program_id(1) * rows + lax.broadcasted_iota(jnp.int32, x.shape, 1)
        x = jnp.where(r < valid_rows, x, 0.0)
    o_ref[...] = x.astype(o_ref.dtype)


def _cast_rows(w, out_rows, *, rows=256):
    nl, r, c = w.shape
    return pl.pallas_call(
        functools.partial(_cast_kernel, valid_rows=r, rows=rows),
        grid=(nl, out_rows // rows),
        in_specs=[pl.BlockSpec((1, rows, c), lambda l, i: (l, i, 0))],
        out_specs=pl.BlockSpec((1, rows, c), lambda l, i: (l, i, 0)),
        out_shape=jax.ShapeDtypeStruct((nl, out_rows, c), BF16),
        compiler_params=pltpu.CompilerParams(
            dimension_semantics=("parallel", "parallel"), vmem_limit_bytes=VMEM_LIMIT),
        name="cast_rows",
    )(w)


def _split_gu_kernel(x_ref, g_ref, u_ref, *, d_ff):
    x = x_ref[0]
    pad = jnp.zeros((x.shape[0], g_ref.shape[2] - d_ff), g_ref.dtype)
    g_ref[0] = jnp.concatenate([x[:, :d_ff].astype(g_ref.dtype), pad], axis=1)
    u_ref[0] = jnp.concatenate([x[:, d_ff:].astype(u_ref.dtype), pad], axis=1)


def _split_gu(w_gu, d_ff, ffp, *, rows=256):
    nl, d, _ = w_gu.shape
    out = jax.ShapeDtypeStruct((nl, d, ffp), BF16)
    spec = pl.BlockSpec((1, rows, ffp), lambda l, i: (l, i, 0))
    return pl.pallas_call(
        functools.partial(_split_gu_kernel, d_ff=d_ff),
        grid=(nl, d // rows),
        in_specs=[pl.BlockSpec((1, rows, 2 * d_ff), lambda l, i: (l, i, 0))],
        out_specs=(spec, spec),
        out_shape=(out, out),
        compiler_params=pltpu.CompilerParams(
            dimension_semantics=("parallel", "parallel"), vmem_limit_bytes=VMEM_LIMIT),
        name="split_gate_up",
    )(w_gu)


def kernel(x, ffn1_norm, ffn1_w_gu, ffn1_w_down, mix_norm, w_in, rwkv_mu, rwkv_w_up, rwkv_w0,
           rwkv_a_up, rwkv_a0, rwkv_g_up, rwkv_k_k, rwkv_k_a, rwkv_r_k, rwkv_ln_w, rwkv_ln_b,
           conv_w, hgrn_lb, hgrn_norm, w_out, ffn2_norm, ffn2_w_gu, ffn2_w_down, final_norm):
    b, t, d = x.shape
    depth = ffn1_norm.shape[0]
    d_ff = ffn1_w_down.shape[1]
    ffp = -(-d_ff // 512) * 512
    m = b * t
    h = x.reshape(m, d)
    row = lambda v: v.reshape(1, -1)
    final_w = row(final_norm)

    wg1, wu1 = _split_gu(ffn1_w_gu, d_ff, ffp)
    wg2, wu2 = _split_gu(ffn2_w_gu, d_ff, ffp)
    wd_last = _cast_rows(ffn2_w_down[depth - 1:], ffp)
    w_in_b = _cast_rows(w_in, d)
    w_out_b = _cast_rows(w_out, d)
    zeros_low = jnp.zeros((depth, LANES - DECAY_RANK, RWKV_WIDTH), BF16)
    ww = jnp.concatenate([rwkv_w_up.astype(BF16), zeros_low], axis=1)
    wa = jnp.concatenate([zeros_low, rwkv_a_up.astype(BF16)], axis=1)
    wgate = rwkv_g_up.astype(BF16)

    for l in range(depth):
        h = _ffn(h, row(ffn1_norm[l]), wg1, wu1, ffn1_w_down, final_w, layer=l, final=False)

        proj = _proj(h, row(mix_norm[l]), w_in_b, layer=l)
        proj3 = proj.reshape(b, t, -1)
        y_r = _rwkv(proj3, row(rwkv_mu[l]), ww[l], wa[l], wgate[l],
                    row(rwkv_w0[l]), row(rwkv_a0[l]), row(rwkv_k_k[l]), row(rwkv_k_a[l]),
                    row(rwkv_r_k[l]), row(rwkv_ln_w[l]), row(rwkv_ln_b[l]))
        y_c = _conv(proj3, conv_w[l])
        y_h = _hgrn(proj3, hgrn_lb, row(hgrn_norm[l]), layer=l)
        h = _outproj(h, y_r.reshape(m, -1), y_c.reshape(m, -1), y_h.reshape(m, -1), w_out_b, layer=l)

        if l < depth - 1:
            h = _ffn(h, row(ffn2_norm[l]), wg2, wu2, ffn2_w_down, final_w, layer=l, final=False)
        else:
            h = _ffn(h, row(ffn2_norm[l]), wg2, wu2, wd_last, final_w, layer=l, wd_layer=0, final=True)
    return h.reshape(b, t, d)
```

```python
import functools

import jax
import jax.numpy as jnp
from jax import lax
from jax.experimental import pallas as pl
from jax.experimental.pallas import tpu as pltpu

F32 = jnp.float32
BF16 = jnp.bfloat16

RWKV_HEAD_DIM = 64
RWKV_WIDTH = 1024
DECAY_RANK = 64
AAA_RANK = 64
GATE_RANK = 128
RWKV_LOW = DECAY_RANK + AAA_RANK + GATE_RANK
RWKV_SHIFT_WIDTH = 3 * RWKV_WIDTH + RWKV_LOW
RWKV_GN_EPS = 64e-5
CONV_WIDTH = 512
CONV_K = 3
HGRN_HEAD_DIM = 128
HGRN_WIDTH = 512
HGRN_NORM_EPS = 1e-5
HGRN_MIN_FORGET = 1e-30
FFN_RESIDUAL_SCALE = 0.5
RMS_EPS = 1e-6

LANES = 128
SUBLANES = 8
RWKV_CHUNK = 64
HGRN_BLOCK = 16
VMEM_LIMIT = 60000 * 1024


def _dot(a, b):
    return jnp.dot(a, b, preferred_element_type=F32)


def _dot_nt(a, b):
    return lax.dot_general(a, b, (((1,), (1,)), ((), ())), preferred_element_type=F32)


def _rms(x, w):
    return x * lax.rsqrt(jnp.mean(x * x, axis=-1, keepdims=True) + RMS_EPS) * w


def _split_bf16(x):
    hi = x.astype(BF16)
    lo = (x - hi.astype(F32)).astype(BF16)
    return hi, lo


def _ffn_kernel(x_ref, nw_ref, wg_ref, wu_ref, wd_ref, fw_ref, o_ref, xn_ref, *, final, tail_rows):
    j = pl.program_id(1)

    def partial_sum(tail=False):
        xn = xn_ref[...]
        g = _dot(xn, wg_ref[...])
        u = _dot(xn, wu_ref[...])
        a = (g * jax.nn.sigmoid(g) * u).astype(BF16)
        wd = wd_ref[...]
        if tail and tail_rows != wd.shape[0]:
            wd = jnp.where(lax.broadcasted_iota(jnp.int32, wd.shape, 0) < tail_rows, wd, 0.0)
        return _dot(a, wd.astype(BF16))

    @pl.when(j == 0)
    def _():
        xn_ref[...] = _rms(x_ref[...], nw_ref[...]).astype(BF16)
        o_ref[...] = partial_sum()

    last = pl.num_programs(1) - 1

    @pl.when((j > 0) & (j < last))
    def _():
        o_ref[...] += partial_sum()

    @pl.when(j == last)
    def _():
        h = x_ref[...] + FFN_RESIDUAL_SCALE * (o_ref[...] + partial_sum(tail=True))
        if final:
            h = _rms(h, fw_ref[...])
        o_ref[...] = h


def _ffn(h, norm_w, wg, wu, wd, final_w, *, layer, final, wd_layer=None, tm=1024, tf=512):
    wd_layer = layer if wd_layer is None else wd_layer
    m, d = h.shape
    ffp = wg.shape[2]
    grid = (m // tm, ffp // tf)
    assert grid[1] >= 2
    return pl.pallas_call(
        functools.partial(_ffn_kernel, final=final, tail_rows=wd.shape[1] - (grid[1] - 1) * tf),
        grid=grid,
        in_specs=[
            pl.BlockSpec((tm, d), lambda i, j: (i, 0)),
            pl.BlockSpec((1, d), lambda i, j: (0, 0)),
            pl.BlockSpec((None, d, tf), lambda i, j: (layer, 0, j)),
            pl.BlockSpec((None, d, tf), lambda i, j: (layer, 0, j)),
            pl.BlockSpec((None, tf, d), lambda i, j: (wd_layer, j, 0)),
            pl.BlockSpec((1, d), lambda i, j: (0, 0)),
        ],
        out_specs=pl.BlockSpec((tm, d), lambda i, j: (i, 0)),
        out_shape=jax.ShapeDtypeStruct((m, d), F32),
        scratch_shapes=[pltpu.VMEM((tm, d), BF16)],
        compiler_params=pltpu.CompilerParams(
            dimension_semantics=("parallel", "arbitrary"), vmem_limit_bytes=VMEM_LIMIT),
        name="ffn",
    )(h, norm_w, wg, wu, wd, final_w)


def _proj_kernel(x_ref, nw_ref, w_ref, o_ref, xn_ref):
    @pl.when(pl.program_id(1) == 0)
    def _():
        xn_ref[...] = _rms(x_ref[...], nw_ref[...]).astype(BF16)
        o_ref[...] = _dot(xn_ref[...], w_ref[...]).astype(o_ref.dtype)

    @pl.when(pl.program_id(1) > 0)
    def _():
        o_ref[...] = _dot(xn_ref[...], w_ref[...]).astype(o_ref.dtype)


def _proj(h, norm_w, w, *, layer, tm=512):
    m, d = h.shape
    n = w.shape[2]
    tn = n
    return pl.pallas_call(
        _proj_kernel,
        grid=(m // tm, n // tn),
        in_specs=[
            pl.BlockSpec((tm, d), lambda i, j: (i, 0)),
            pl.BlockSpec((1, d), lambda i, j: (0, 0)),
            pl.BlockSpec((None, d, tn), lambda i, j: (layer, 0, j), pipeline_mode=pl.Buffered(1)),
        ],
        out_specs=pl.BlockSpec((tm, tn), lambda i, j: (i, j)),
        out_shape=jax.ShapeDtypeStruct((m, n), BF16),
        scratch_shapes=[pltpu.VMEM((tm, d), BF16)],
        compiler_params=pltpu.CompilerParams(
            dimension_semantics=("parallel", "arbitrary"), vmem_limit_bytes=VMEM_LIMIT),
        name="proj",
    )(h, norm_w, w)


def _outproj_kernel(h_ref, yr_ref, yc_ref, yh_ref, wr_ref, wc_ref, wh_ref, o_ref):
    acc = _dot(yr_ref[...], wr_ref[...])
    acc += _dot(yc_ref[...], wc_ref[...])
    acc += _dot(yh_ref[...], wh_ref[...])
    o_ref[...] = h_ref[...] + acc


def _outproj(h, yr, yc, yh, wo, *, layer, tm=512):
    m, d = h.shape
    nr, ncv, nh = yr.shape[1], yc.shape[1], yh.shape[1]
    row = lambda i: (i, 0)
    return pl.pallas_call(
        _outproj_kernel,
        grid=(m // tm,),
        in_specs=[
            pl.BlockSpec((tm, d), row),
            pl.BlockSpec((tm, yr.shape[1]), row),
            pl.BlockSpec((tm, yc.shape[1]), row),
            pl.BlockSpec((tm, yh.shape[1]), row),
            pl.BlockSpec((None, nr, d), lambda i: (layer, 0, 0)),
            pl.BlockSpec((None, ncv, d), lambda i: (layer, nr // ncv, 0)),
            pl.BlockSpec((None, nh, d), lambda i: (layer, (nr + ncv) // nh, 0)),
        ],
        out_specs=pl.BlockSpec((tm, d), row),
        out_shape=jax.ShapeDtypeStruct((m, d), F32),
        compiler_params=pltpu.CompilerParams(
            dimension_semantics=("parallel",), vmem_limit_bytes=VMEM_LIMIT),
        name="outproj",
    )(h, yr, yc, yh, wo, wo, wo)


def _prev_rows(x, carry, k=1):
    rows = lax.broadcasted_iota(jnp.int32, (SUBLANES, x.shape[1]), 0)
    prev = pltpu.roll(x, k, 0)
    head = jnp.where(rows < k, pltpu.roll(carry, k, 0), prev[:SUBLANES])
    return jnp.concatenate([head, prev[SUBLANES:]], axis=0)


def _seg_sum(x, ones_bd):
    return _dot(x.astype(BF16), ones_bd)


def _rwkv_prepare(lw, r, k2, v, av, bv, c, tick):
    n = 2 * RWKV_CHUNK
    m0, m1 = c["m0"], c["m1"]
    stack = lambda x: jnp.concatenate([x * m0, x * m1], axis=0)
    each = lambda f, *ls: [f(*xs) for xs in zip(*ls)]

    def cumsum(x):
        hi, lo = _split_bf16(x)
        gcat = _dot(c["tri"], jnp.concatenate([hi, lo], axis=1))
        return gcat[:, :LANES] + gcat[:, LANES:]

    g = each(cumsum, lw)
    tick()
    gl = [x[RWKV_CHUNK - 1:RWKV_CHUNK, :] for x in g]
    e_g = each(jnp.exp, g)
    e_ng = each(lambda x: jnp.exp(-x), g)
    e_d = each(lambda x, y: jnp.exp(x - y), gl, g)
    ats = each(lambda a_, g_, l_: stack(a_ * jnp.exp(g_ - l_)), av, g, lw)
    rts = each(lambda r_, e_: stack(r_ * e_), r, e_g)
    bts = each(lambda b_, e_: stack(b_ * e_), bv, e_ng)
    kts = each(lambda k_, e_: stack(k_ * e_), k2, e_ng)
    bes = each(lambda b_, e_: stack(b_ * e_).astype(BF16), bv, e_d)
    kes = each(lambda k_, e_: stack(k_ * e_).astype(BF16), k2, e_d)
    vs = each(stack, v)
    vsb = [x.astype(BF16) for x in vs]

    p = each(lambda a_, r_, b_, k_: _dot_nt(jnp.concatenate([a_, r_], axis=0).astype(BF16),
                                            jnp.concatenate([b_, k_], axis=0).astype(BF16)),
             ats, rts, bts, kts)
    a_ab = [jnp.where(c["strict"], x[:n, :n], 0.0) for x in p]
    a_kr = [jnp.concatenate([jnp.where(c["strict"], x[:n, n:], 0.0),
                             jnp.where(c["incl"], x[n:, n:], 0.0)], axis=0).astype(BF16) for x in p]
    a_rb = [jnp.where(c["incl"], x[n:, :n], 0.0).astype(BF16) for x in p]
    tick()

    h = RWKV_CHUNK
    side = lambda x: x[:h] + x[h:]
    diag = lambda x: jnp.concatenate([x, x], axis=0) * c["own"]
    a_sbs = [side(x) for x in a_ab]
    inv = [(c["eye"] + x).astype(BF16) for x in a_sbs]
    q = [_dot(x.astype(BF16), y.astype(BF16)).astype(BF16) for x, y in zip(a_sbs, a_ab)]
    tick()
    for _ in range(4):
        z = each(lambda i_, q_: _dot(jnp.concatenate([i_, q_], axis=0), diag(q_.astype(F32)).astype(BF16)), inv, q)
        inv = each(lambda i_, z_: (i_.astype(F32) + z_[:h]).astype(BF16), inv, z)
        q = [x[h:].astype(BF16) for x in z]
        tick()
    inv = each(lambda i_, q_: diag(i_.astype(F32) + _dot(i_, diag(q_.astype(F32)).astype(BF16))), inv, q)
    tick()

    akv_rkv = each(_dot, a_kr, vsb)
    tick()
    tt = each(lambda i_, a_, x_: _dot(i_.astype(BF16), jnp.concatenate([a_, x_[:n]], axis=1).astype(BF16)),
              inv, ats, akv_rkv)
    tick()
    rr = each(lambda a_, t_: _dot(a_, t_.astype(BF16)), a_rb, tt)
    rq = each(lambda r_, x_: (r_ + x_[:, :LANES]).astype(BF16), rts, rr)
    oc = each(lambda x_, y_: x_[:, LANES:] + y_[n:], rr, akv_rkv)
    bb = each(lambda t_, b_: _dot(t_.T.astype(BF16), b_), tt, bes)
    bta = [x[:LANES].astype(BF16) for x in bb]
    nct = each(lambda x_, v_, k_: x_[LANES:] + _dot(v_.T.astype(BF16), k_), bb, vs, kes)
    e_gl = each(jnp.exp, gl)
    return rq, oc, bta, nct, e_gl


def _rwkv_kernel(r_ref, k_ref, v_ref, low_ref, mur_ref, muk_ref, muv_ref, mul_ref,
                 ww_ref, wa_ref, wg_ref, w0_ref, a0_ref, kk_ref, ka_ref, rk_ref, lnw_ref, lnb_ref,
                 y_ref, cr_ref, ck_ref, cv_ref, cl_ref, st_ref, feat_ref,
                 rq_ref, oc_ref, bta_ref, nct_ref, egl_ref, bonus_ref, gate_ref, *, tb, nt):
    s = pl.program_id(0)
    nc = tb // RWKV_CHUNK

    @pl.when(s == 0)
    def _():
        for ref in (cr_ref, ck_ref, cv_ref, cl_ref, st_ref, feat_ref, rq_ref, oc_ref, bta_ref, nct_ref, egl_ref,
                    bonus_ref, gate_ref):
            ref[...] = jnp.zeros_like(ref)

    lane = lax.broadcasted_iota(jnp.int32, (1, LANES), 1)
    m0 = (lane < RWKV_HEAD_DIM).astype(F32)
    m1 = 1.0 - m0
    ri = lax.broadcasted_iota(jnp.int32, (LANES, LANES), 0)
    ci = lax.broadcasted_iota(jnp.int32, (LANES, LANES), 1)
    same = (ri < RWKV_HEAD_DIM) == (ci < RWKV_HEAD_DIM)
    ones_bd = same.astype(F32).astype(BF16)
    ti = lax.broadcasted_iota(jnp.int32, (RWKV_CHUNK, RWKV_CHUNK), 0)
    si = lax.broadcasted_iota(jnp.int32, (RWKV_CHUNK, RWKV_CHUNK), 1)
    tw = lax.broadcasted_iota(jnp.int32, (RWKV_CHUNK, LANES), 0)
    sw = lax.broadcasted_iota(jnp.int32, (RWKV_CHUNK, LANES), 1) & (RWKV_HEAD_DIM - 1)
    consts = dict(
        m0=m0, m1=m1, own=same.astype(F32),
        strict=same & (ri > ci), incl=same & (ri >= ci),
        eye=(tw == sw).astype(F32),
        tri=(ti >= si).astype(F32).astype(BF16),
    )

    rd = lax.rem(s, 2)
    rd3 = lax.rem(s + 1, 3)
    seq_start = lax.rem(jnp.maximum(s - 2, 0), nt) == 0
    state = [st_ref[...] * jnp.where(seq_start, 0.0, 1.0)]
    outs = []
    inv_n = 1.0 / RWKV_HEAD_DIM
    own = same.astype(F32)

    def recurrence_step():
        i = len(outs)
        if i == nc:
            return
        st = state[0]
        stb = st.astype(BF16)
        o_stack = _dot_nt(rq_ref[rd, i], stb) + oc_ref[rd, i]
        state[0] = st * egl_ref[rd, i][0:1, :] + _dot(stb, bta_ref[rd, i]) + nct_ref[rd, i]
        mean = jnp.sum(o_stack, axis=-1, keepdims=True) * inv_n
        cen = (o_stack - mean) * own
        var = jnp.sum(cen * cen, axis=-1, keepdims=True) * inv_n
        on = cen * lax.rsqrt(var + RWKV_GN_EPS)
        outs.append(on[:RWKV_CHUNK] + on[RWKV_CHUNK:])

    wf = lax.rem(s, 2)
    wf3 = lax.rem(s, 3)
    keep = jnp.where(lax.rem(jnp.minimum(s, pl.num_programs(0) - 3), nt) == 0, 0.0, 1.0)
    f = {}

    def lerp(x_ref, c_ref, mu_ref):
        x = x_ref[0].astype(F32)
        prev = _prev_rows(x, c_ref[...] * keep)
        c_ref[...] = x[tb - SUBLANES:tb, :]
        return x + (prev - x) * mu_ref[...]

    def feat_low():
        low = lerp(low_ref, cl_ref, mul_ref)
        f["wa_in"] = low[:, :LANES]
        gate_ref[wf3] = _dot(jax.nn.sigmoid(low[:, LANES:]).astype(BF16), wg_ref[...])

    def feat_decay():
        xw = w0_ref[...] + _dot(jnp.tanh(f["wa_in"]).astype(BF16), ww_ref[...])
        feat_ref[wf, 0] = -jnp.exp(jnp.float32(-0.5)) * jax.nn.sigmoid(xw)
        f["a"] = jax.nn.sigmoid(a0_ref[...] + _dot(f["wa_in"].astype(BF16), wa_ref[...]))

    def feat_k():
        k = lerp(k_ref, ck_ref, muk_ref)
        kk = k * kk_ref[...]
        kk = kk * lax.rsqrt(jnp.maximum(_seg_sum(kk * kk, ones_bd), 1e-24))
        f["k2"] = k * (1.0 + (f["a"] - 1.0) * ka_ref[...])
        feat_ref[wf, 2] = f["k2"]
        feat_ref[wf, 4] = -kk
        feat_ref[wf, 5] = kk * f["a"]

    def feat_rv():
        r = lerp(r_ref, cr_ref, mur_ref)
        v = lerp(v_ref, cv_ref, muv_ref)
        feat_ref[wf, 1] = r
        feat_ref[wf, 3] = v
        bonus_ref[wf3] = _seg_sum(r * f["k2"] * rk_ref[...], ones_bd) * v

    pieces = [feat_low, feat_decay, feat_k, feat_rv]

    def tick():
        if pieces:
            pieces.pop(0)()
        recurrence_step()

    rf = lax.rem(s + 1, 2)
    wr = lax.rem(s + 1, 2)
    chunks = lambda j: [feat_ref[rf, j, i * RWKV_CHUNK:(i + 1) * RWKV_CHUNK, :] for i in range(nc)]
    rq, oc, bta, nct, e_gl = _rwkv_prepare(chunks(0), chunks(1), chunks(2), chunks(3), chunks(4), chunks(5),
                                           consts, tick)
    while pieces or len(outs) < nc:
        tick()

    st_ref[...] = state[0]
    o = jnp.concatenate(outs, axis=0) * lnw_ref[...] + lnb_ref[...]
    y_ref[0] = ((o + bonus_ref[rd3]) * gate_ref[rd3]).astype(y_ref.dtype)

    for i in range(nc):
        rq_ref[wr, i] = rq[i]
        oc_ref[wr, i] = oc[i]
        bta_ref[wr, i] = bta[i]
        nct_ref[wr, i] = nct[i]
        egl_ref[wr, i] = jnp.broadcast_to(e_gl[i], (SUBLANES, LANES))


def _rwkv(proj3, mu, ww, wa, wg, w0, a0, k_k, k_a, r_k, ln_w, ln_b, *, tb=512):
    b, t, _ = proj3.shape
    npairs = RWKV_WIDTH // LANES
    nt = t // tb
    nc = tb // RWKV_CHUNK
    nblocks = b * npairs * nt
    off_k = RWKV_WIDTH // LANES
    off_v = 2 * RWKV_WIDTH // LANES
    off_low = 3 * RWKV_WIDTH // RWKV_LOW

    def where(item):
        return item // (nt * npairs), item % nt, (item // nt) % npairs

    cur = lambda s: where(jnp.minimum(s, nblocks - 1))
    lag = lambda s: where(jnp.maximum(s - 2, 0))
    act = lambda off: pl.BlockSpec((1, tb, LANES), lambda s: (cur(s)[0], cur(s)[1], off + cur(s)[2]))
    vec = lambda off: pl.BlockSpec((1, LANES), lambda s: (0, off + cur(s)[2]))
    lagvec = pl.BlockSpec((1, LANES), lambda s: (0, lag(s)[2]))
    wspec = pl.BlockSpec((LANES, LANES), lambda s: (0, cur(s)[2]))
    return pl.pallas_call(
        functools.partial(_rwkv_kernel, tb=tb, nt=nt),
        grid=(nblocks + 2,),
        in_specs=[
            act(0), act(off_k), act(off_v),
            pl.BlockSpec((1, tb, RWKV_LOW), lambda s: (cur(s)[0], cur(s)[1], off_low)),
            vec(0), vec(off_k), vec(off_v),
            pl.BlockSpec((1, RWKV_LOW), lambda s: (0, off_low)),
            wspec, wspec, wspec,
            vec(0), vec(0), vec(0), vec(0), vec(0), lagvec, lagvec,
        ],
        out_specs=pl.BlockSpec((1, tb, LANES), lambda s: lag(s)),
        out_shape=jax.ShapeDtypeStruct((b, t, RWKV_WIDTH), BF16),
        scratch_shapes=[pltpu.VMEM((SUBLANES, LANES), F32), pltpu.VMEM((SUBLANES, LANES), F32),
                        pltpu.VMEM((SUBLANES, LANES), F32), pltpu.VMEM((SUBLANES, RWKV_LOW), F32),
                        pltpu.VMEM((LANES, LANES), F32),
                        pltpu.VMEM((2, 6, tb, LANES), F32),
                        pltpu.VMEM((2, nc, LANES, LANES), BF16), pltpu.VMEM((2, nc, LANES, LANES), F32),
                        pltpu.VMEM((2, nc, LANES, LANES), BF16), pltpu.VMEM((2, nc, LANES, LANES), F32),
                        pltpu.VMEM((2, nc, SUBLANES, LANES), F32),
                        pltpu.VMEM((3, tb, LANES), F32), pltpu.VMEM((3, tb, LANES), F32)],
        compiler_params=pltpu.CompilerParams(
            dimension_semantics=("arbitrary",), vmem_limit_bytes=VMEM_LIMIT),
        name="rwkv7",
    )(proj3, proj3, proj3, proj3, mu, mu, mu, mu, ww, wa, wg, w0, a0, k_k, k_a, r_k, ln_w, ln_b)


def _conv_kernel(c_ref, x_ref, b_ref, w_ref, y_ref, carry_ref):
    @pl.when(pl.program_id(2) == 0)
    def _():
        carry_ref[...] = jnp.zeros_like(carry_ref)

    z = c_ref[0].astype(F32) * x_ref[0].astype(F32)
    tb = z.shape[0]
    w = w_ref[...]
    y = z * w[CONV_K - 1:CONV_K, :]
    for j in range(1, CONV_K):
        y = y + _prev_rows(z, carry_ref[...], j) * w[CONV_K - 1 - j:CONV_K - j, :]
    carry_ref[...] = z[tb - SUBLANES:tb, :]
    y_ref[0] = (b_ref[0].astype(F32) * y).astype(y_ref.dtype)


def _conv(proj3, conv_w, *, tb=1024, cw=2 * LANES):
    b, t, _ = proj3.shape
    tb = min(tb, t)
    nblk = CONV_WIDTH // cw
    off = RWKV_SHIFT_WIDTH // cw
    spec = lambda o: pl.BlockSpec((1, tb, cw), lambda bi, p, ti: (bi, ti, o + p))
    return pl.pallas_call(
        _conv_kernel,
        grid=(b, nblk, t // tb),
        in_specs=[spec(off), spec(off + nblk), spec(off + 2 * nblk),
                  pl.BlockSpec((CONV_K, cw), lambda bi, p, ti: (0, p))],
        out_specs=pl.BlockSpec((1, tb, cw), lambda bi, p, ti: (bi, ti, p)),
        out_shape=jax.ShapeDtypeStruct((b, t, CONV_WIDTH), BF16),
        scratch_shapes=[pltpu.VMEM((SUBLANES, cw), F32)],
        compiler_params=pltpu.CompilerParams(
            dimension_semantics=("parallel", "parallel", "arbitrary"), vmem_limit_bytes=VMEM_LIMIT),
        name="shortconv",
    )(proj3, proj3, proj3, conv_w)


def _hgrn_kernel(q_ref, f_ref, i_ref, og_ref, lb_ref, nw_ref, y_ref, st_ref, *, layer, tb):
    @pl.when(pl.program_id(2) == 0)
    def _():
        st_ref[...] = jnp.zeros_like(st_ref)

    lbp = lb_ref[...]
    e = jnp.exp(lbp - jnp.max(lbp, axis=0, keepdims=True))
    pr = e / jnp.sum(e, axis=0, keepdims=True)
    lb = jnp.zeros((1, LANES), F32)
    for l in range(1, layer + 1):
        lb = lb + pr[l:l + 1, :]

    qr = q_ref[0].astype(F32)
    q = qr * jax.nn.sigmoid(qr)
    f = f_ref[0].astype(F32)
    sg = jax.nn.sigmoid(f)
    forget = lb + (1.0 - lb) * sg
    logf = jnp.log(jnp.maximum(forget, HGRN_MIN_FORGET))
    kx = (1.0 - lb) * (1.0 - sg)
    iv = i_ref[0].astype(F32)

    n = HGRN_BLOCK
    nb = tb // n
    ti = lax.broadcasted_iota(jnp.int32, (tb, tb), 0)
    si = lax.broadcasted_iota(jnp.int32, (tb, tb), 1)
    same = (ti ^ si) < n
    sums = jnp.concatenate([(same & (ti >= si)).astype(F32).astype(BF16), same.astype(F32).astype(BF16)], axis=0)
    hi, lo = _split_bf16(logf)
    gg = _dot(sums, jnp.concatenate([hi, lo], axis=1))
    g = gg[:tb, :LANES] + gg[:tb, LANES:]
    gl = gg[tb:, :LANES] + gg[tb:, LANES:]
    qd = (q * jnp.exp(g)).astype(BF16)
    kd = (kx * jnp.exp(gl - g)).astype(BF16)
    dec = jnp.exp(gl)
    trow = lax.broadcasted_iota(jnp.int32, (n, 1), 0)
    blocks = lambda x: [x[i * n:(i + 1) * n] for i in range(nb)]
    g2 = g * 1.4426950408889634
    gb, hb, qb, ib, qdb, kdb = (blocks(x) for x in (g2, g2 - jnp.log2(kx), q, iv, qd, kd))

    upd = [_dot(i_.T.astype(BF16), k_) for i_, k_ in zip(ib, kdb)]
    st = st_ref[...]
    sts = []
    for i in range(nb):
        sts.append(st.astype(BF16))
        st = st * dec[i * n:i * n + 1, :] + upd[i]
    st_ref[...] = st
    outs = [_dot_nt(q_, s_) for q_, s_ in zip(qdb, sts)]
    lane = lax.broadcasted_iota(jnp.int32, (n, LANES), 1)
    scores = [jnp.zeros((n, LANES), F32) for _ in range(nb)]
    for s in range(n):
        for i in range(nb):
            w = jnp.sum(qb[i] * jnp.exp2(gb[i] - hb[i][s:s + 1, :]), axis=-1, keepdims=True)
            scores[i] = jnp.where(lane == s, w, scores[i])
    causal = lane <= trow
    for i in range(nb):
        sc = jnp.where(causal, scores[i], 0.0)[:, :n].astype(BF16)
        outs[i] = outs[i] + _dot(sc, ib[i].astype(BF16))
    o = jnp.concatenate(outs, axis=0)
    o = o * lax.rsqrt(jnp.mean(o * o, axis=-1, keepdims=True) + HGRN_NORM_EPS) * nw_ref[...]
    og = og_ref[0].astype(F32)
    y_ref[0] = (o * (og * jax.nn.sigmoid(og))).astype(y_ref.dtype)


def _hgrn(proj3, hgrn_lb, norm_w, *, layer, tb=512):
    b, t, _ = proj3.shape
    tb = min(tb, t)
    nh = HGRN_WIDTH // HGRN_HEAD_DIM
    off = (RWKV_SHIFT_WIDTH + 3 * CONV_WIDTH) // LANES
    spec = lambda o: pl.BlockSpec((1, tb, LANES), lambda bi, p, ti: (bi, ti, o + p))
    depth = hgrn_lb.shape[0]
    return pl.pallas_call(
        functools.partial(_hgrn_kernel, layer=layer, tb=tb),
        grid=(b, nh, t // tb),
        in_specs=[spec(off), spec(off + nh), spec(off + 2 * nh), spec(off + 3 * nh),
                  pl.BlockSpec((depth, LANES), lambda bi, p, ti: (0, p)),
                  pl.BlockSpec((1, LANES), lambda bi, p, ti: (0, p))],
        out_specs=pl.BlockSpec((1, tb, LANES), lambda bi, p, ti: (bi, ti, p)),
        out_shape=jax.ShapeDtypeStruct((b, t, HGRN_WIDTH), BF16),
        scratch_shapes=[pltpu.VMEM((LANES, LANES), F32)],
        compiler_params=pltpu.CompilerParams(
            dimension_semantics=("parallel", "parallel", "arbitrary"), vmem_limit_bytes=VMEM_LIMIT),
        name="hgrn2",
    )(proj3, proj3, proj3, proj3, hgrn_lb, norm_w)


def _cast_kernel(x_ref, o_ref, *, valid_rows, rows):
    x = x_ref[...]
    if valid_rows % rows:
        r = pl.program_id(1) * rows + lax.broadcasted_iota(jnp.int32, x.shape, 1)
        x = jnp.where(r < valid_rows, x, 0.0)
    o_ref[...] = x.astype(o_ref.dtype)


def _cast_rows(w, out_rows, *, rows=256):
    nl, r, c = w.shape
    return pl.pallas_call(
        functools.partial(_cast_kernel, valid_rows=r, rows=rows),
        grid=(nl, out_rows // rows),
        in_specs=[pl.BlockSpec((1, rows, c), lambda l, i: (l, i, 0))],
        out_specs=pl.BlockSpec((1, rows, c), lambda l, i: (l, i, 0)),
        out_shape=jax.ShapeDtypeStruct((nl, out_rows, c), BF16),
        compiler_params=pltpu.CompilerParams(
            dimension_semantics=("parallel", "parallel"), vmem_limit_bytes=VMEM_LIMIT),
        name="cast_rows",
    )(w)


def _split_gu_kernel(x_ref, g_ref, u_ref, *, d_ff):
    x = x_ref[0]
    pad = jnp.zeros((x.shape[0], g_ref.shape[2] - d_ff), g_ref.dtype)
    g_ref[0] = jnp.concatenate([x[:, :d_ff].astype(g_ref.dtype), pad], axis=1)
    u_ref[0] = jnp.concatenate([x[:, d_ff:].astype(u_ref.dtype), pad], axis=1)


def _split_gu(w_gu, d_ff, ffp, *, rows=256):
    nl, d, _ = w_gu.shape
    out = jax.ShapeDtypeStruct((nl, d, ffp), BF16)
    spec = pl.BlockSpec((1, rows, ffp), lambda l, i: (l, i, 0))
    return pl.pallas_call(
        functools.partial(_split_gu_kernel, d_ff=d_ff),
        grid=(nl, d // rows),
        in_specs=[pl.BlockSpec((1, rows, 2 * d_ff), lambda l, i: (l, i, 0))],
        out_specs=(spec, spec),
        out_shape=(out, out),
        compiler_params=pltpu.CompilerParams(
            dimension_semantics=("parallel", "parallel"), vmem_limit_bytes=VMEM_LIMIT),
        name="split_gate_up",
    )(w_gu)


def kernel(x, ffn1_norm, ffn1_w_gu, ffn1_w_down, mix_norm, w_in, rwkv_mu, rwkv_w_up, rwkv_w0,
           rwkv_a_up, rwkv_a0, rwkv_g_up, rwkv_k_k, rwkv_k_a, rwkv_r_k, rwkv_ln_w, rwkv_ln_b,
           conv_w, hgrn_lb, hgrn_norm, w_out, ffn2_norm, ffn2_w_gu, ffn2_w_down, final_norm):
    b, t, d = x.shape
    depth = ffn1_norm.shape[0]
    d_ff = ffn1_w_down.shape[1]
    ffp = -(-d_ff // 512) * 512
    m = b * t
    h = x.reshape(m, d)
    row = lambda v: v.reshape(1, -1)
    final_w = row(final_norm)

    wg1, wu1 = _split_gu(ffn1_w_gu, d_ff, ffp)
    wg2, wu2 = _split_gu(ffn2_w_gu, d_ff, ffp)
    wd_last = _cast_rows(ffn2_w_down[depth - 1:], ffp)
    w_in_b = _cast_rows(w_in, d)
    w_out_b = _cast_rows(w_out, d)
    zeros_low = jnp.zeros((depth, LANES - DECAY_RANK, RWKV_WIDTH), BF16)
    ww = jnp.concatenate([rwkv_w_up.astype(BF16), zeros_low], axis=1)
    wa = jnp.concatenate([zeros_low, rwkv_a_up.astype(BF16)], axis=1)
    wgate = rwkv_g_up.astype(BF16)

    for l in range(depth):
        h = _ffn(h, row(ffn1_norm[l]), wg1, wu1, ffn1_w_down, final_w, layer=l, final=False)

        proj = _proj(h, row(mix_norm[l]), w_in_b, layer=l)
        proj3 = proj.reshape(b, t, -1)
        y_r = _rwkv(proj3, row(rwkv_mu[l]), ww[l], wa[l], wgate[l],
                    row(rwkv_w0[l]), row(rwkv_a0[l]), row(rwkv_k_k[l]), row(rwkv_k_a[l]),
                    row(rwkv_r_k[l]), row(rwkv_ln_w[l]), row(rwkv_ln_b[l]))
        y_c = _conv(proj3, conv_w[l])
        y_h = _hgrn(proj3, hgrn_lb, row(hgrn_norm[l]), layer=l)
        h = _outproj(h, y_r.reshape(m, -1), y_c.reshape(m, -1), y_h.reshape(m, -1), w_out_b, layer=l)

        if l < depth - 1:
            h = _ffn(h, row(ffn2_norm[l]), wg2, wu2, ffn2_w_down, final_w, layer=l, final=False)
        else:
            h = _ffn(h, row(ffn2_norm[l]), wg2, wu2, wd_last, final_w, layer=l, wd_layer=0, final=True)
    return h.reshape(b, t, d)
```
